```python
import math
import jax, jax.numpy as jnp
from jax import lax
import numpy as np

D_MODEL = 1024
BATCH = 16
SEQ = 4096
DEPTH = 2

GRID_W = 64
CTX_LEN = 256
HEAD_DIM = 64
Q_BLOCK = 128
ROPE_THETA = 10000.0
NORM_EPS = 1e-6
GQA_HEADS = 8
GQA_KV_HEADS = 2
RET_HEADS = 4
RET_DK = HEAD_DIM
RET_DV = 2 * HEAD_DIM
RET_CHUNK = 128
DIFF_HEADS = D_MODEL // (2 * HEAD_DIM)
DIFF_DV = 2 * HEAD_DIM
A_Q = GQA_HEADS * HEAD_DIM
A_KV = GQA_KV_HEADS * HEAD_DIM
B_QK = RET_HEADS * RET_DK
B_V = RET_HEADS * RET_DV
EVEN_IN = A_Q + 2 * A_KV + 2 * B_QK + 2 * B_V
EVEN_MIX = A_Q + B_V
C_QK = 2 * DIFF_HEADS * HEAD_DIM
C_V = DIFF_HEADS * DIFF_DV
ODD_IN = 2 * C_QK + C_V
ODD_MIX = C_V
PEER_HEADS = 8
PEER_NKEYS = 128
PEER_EXPERTS = PEER_NKEYS * PEER_NKEYS
PEER_DKEY = 256
PEER_DHALF = PEER_DKEY // 2
PEER_TOPK = 16
PEER_BLOCK = 128
N_EVEN = (DEPTH + 1) // 2
N_ODD = DEPTH // 2

kernel_name = "hybrid_gqa_retention_diffattn_peer_dit"


def _split(p, sizes):
    out, off = [], 0
    for s in sizes:
        out.append(p[..., off:off + s])
        off += s
    return out


def _rmsnorm(x, g=None):
    xf = x.astype(jnp.float32)
    y = xf * lax.rsqrt(jnp.mean(xf * xf, axis=-1, keepdims=True) + NORM_EPS)
    if g is not None:
        y = y * g.astype(jnp.float32)
    return y.astype(x.dtype)


def _modulate(h, shift, scale):
    return h * (1.0 + scale) + shift


def _axial_angles(n_tokens):
    rows = n_tokens // GRID_W
    row_id = jnp.broadcast_to(jnp.arange(rows)[:, None], (rows, GRID_W)).reshape(-1).astype(jnp.float32)
    col_id = jnp.broadcast_to(jnp.arange(GRID_W)[None, :], (rows, GRID_W)).reshape(-1).astype(jnp.float32)
    axis_dim = HEAD_DIM // 2
    inv = ROPE_THETA ** (-jnp.arange(0, axis_dim, 2, dtype=jnp.float32) / axis_dim)
    ang_r = row_id[:, None] * inv[None, :]
    ang_c = col_id[:, None] * inv[None, :]
    return (jnp.cos(ang_r), jnp.sin(ang_r), jnp.cos(ang_c), jnp.sin(ang_c))


def _rot_half(u, cos, sin):
    u1, u2 = jnp.split(u, 2, axis=-1)
    return jnp.concatenate([u1 * cos - u2 * sin, u1 * sin + u2 * cos], axis=-1)


def _axial_rope(x, rope):
    cr, sr, cc, sc = (t[None, :, None, :] for t in rope)
    xf = x.astype(jnp.float32)
    half = HEAD_DIM // 2
    out = jnp.concatenate([_rot_half(xf[..., :half], cr, sr),
                           _rot_half(xf[..., half:], cc, sc)], axis=-1)
    return out.astype(x.dtype)


def _gqa_attend(q, k, v):
    b, t, kvh, g, d = q.shape
    nb = t // Q_BLOCK
    qb = jnp.swapaxes(q.reshape(b, nb, Q_BLOCK, kvh, g, d), 0, 1)
    scale = d ** -0.5

    def block(qi):
        s = jnp.einsum('bqhgd,bkhd->bhgqk', qi, k).astype(jnp.float32) * scale
        p = jax.nn.softmax(s, axis=-1).astype(v.dtype)
        return jnp.einsum('bhgqk,bkhd->bqhgd', p, v)

    o = lax.map(block, qb)
    return jnp.swapaxes(o, 0, 1).reshape(b, t, kvh * g * d)


def _diff_attend(q1, q2, k1, k2, v, lam):
    b, t, h, d = q1.shape
    nb = t // Q_BLOCK
    scale = d ** -0.5

    def split(a):
        return jnp.swapaxes(a.reshape(b, nb, Q_BLOCK, h, d), 0, 1)

    def block(qs):
        qa, qb = qs
        p1 = jax.nn.softmax(jnp.einsum('bqhd,bkhd->bhqk', qa, k1).astype(jnp.float32) * scale, axis=-1)
        p2 = jax.nn.softmax(jnp.einsum('bqhd,bkhd->bhqk', qb, k2).astype(jnp.float32) * scale, axis=-1)
        p = (p1 - lam * p2).astype(v.dtype)
        return jnp.einsum('bhqk,bkhe->bqhe', p, v)

    o = lax.map(block, (split(q1), split(q2)))
    return jnp.swapaxes(o, 0, 1).reshape(b, t, h, v.shape[-1])


def _retention(q, k, v, log_g, s0):
    b, t, h, dk = q.shape
    dv = v.shape[-1]
    n = t // RET_CHUNK
    qc = q.reshape(b, n, RET_CHUNK, h, dk)
    kc = k.reshape(b, n, RET_CHUNK, h, dk)
    vc = v.reshape(b, n, RET_CHUNK, h, dv)
    i = jnp.arange(RET_CHUNK, dtype=jnp.float32)
    rel = i[:, None] - i[None, :]
    dmask = jnp.where(rel >= 0, jnp.exp(log_g[:, None, None] * jnp.maximum(rel, 0.0)), 0.0)
    sc = jnp.einsum('bnihd,bnjhd->bnhij', qc, kc) * dmask
    intra = jnp.einsum('bnhij,bnjhe->bnihe', sc, vc)
    k_dec = jnp.exp(log_g[None, :] * (RET_CHUNK - 1.0 - i)[:, None])
    chunk_kv = jnp.einsum('bnjhd,jh,bnjhe->nbhde', kc, k_dec, vc)
    chunk_decay = jnp.exp(log_g * RET_CHUNK)[None, :, None, None]

    def step(state, kv):
        return chunk_decay * state + kv, state

    s_final, s_prev = lax.scan(step, s0, chunk_kv)
    q_dec = jnp.exp(log_g[None, :] * (i + 1.0)[:, None])
    cross = jnp.einsum('bnihd,ih,nbhde->bnihe', qc, q_dec, s_prev)
    return (intra + cross).reshape(b, t, h, dv), s_final


def _even_mixer(h, hc, w_in, w_out, q_norm_g, k_norm_g, ret_log_rate, rope, need_ctx):
    b = h.shape[0]
    log_gf = -jnp.exp(ret_log_rate[0].astype(jnp.float32))
    log_gb = -jnp.exp(ret_log_rate[1].astype(jnp.float32))

    def project(z, use_rope):
        bz, t, _ = z.shape
        qa, ka, va, qb, kb, vb, gb = _split(z @ w_in, (A_Q, A_KV, A_KV, B_QK, B_QK, B_V, B_V))
        qa = _rmsnorm(qa.reshape(bz, t, GQA_HEADS, HEAD_DIM), q_norm_g)
        ka = _rmsnorm(ka.reshape(bz, t, GQA_KV_HEADS, HEAD_DIM), k_norm_g)
        va = va.reshape(bz, t, GQA_KV_HEADS, HEAD_DIM)
        qb = qb.reshape(bz, t, RET_HEADS, RET_DK)
        kb = kb.reshape(bz, t, RET_HEADS, RET_DK) * (RET_DK ** -0.5)
        vb = vb.reshape(bz, t, RET_HEADS, RET_DV)
        if use_rope:
            qa, ka, qb, kb = (_axial_rope(u, rope) for u in (qa, ka, qb, kb))
        qa = qa.reshape(bz, t, GQA_KV_HEADS, GQA_HEADS // GQA_KV_HEADS, HEAD_DIM)
        return qa, ka, va, qb, kb, vb, gb

    def flip(u):
        return jnp.flip(u, axis=1)

    def ret_out(o, g):
        bz, t = o.shape[:2]
        return _rmsnorm(o).reshape(bz, t, B_V).astype(g.dtype) * jax.nn.silu(g)

    qa, ka, va, qb, kb, vb, gb = project(h, True)
    qac, kac, vac, qbc, kbc, vbc, gbc = project(hc, False)
    ya = _gqa_attend(qa, jnp.concatenate([kac, ka], axis=1), jnp.concatenate([vac, va], axis=1))
    zeros = jnp.zeros((b, RET_HEADS, RET_DK, RET_DV), jnp.float32)
    oc_f, sc_f = _retention(qbc, kbc, vbc, log_gf, zeros)
    oc_b, sc_b = _retention(flip(qbc), flip(kbc), flip(vbc), log_gb, zeros)
    o_f, _ = _retention(qb, kb, vb, log_gf, sc_f)
    o_b, _ = _retention(flip(qb), flip(kb), flip(vb), log_gb, sc_b)
    y = jnp.concatenate([ya, ret_out(o_f + flip(o_b), gb)], axis=-1) @ w_out
    yc = None
    if need_ctx:
        yac = _gqa_attend(qac, kac, vac)
        yc = jnp.concatenate([yac, ret_out(oc_f + flip(oc_b), gbc)], axis=-1) @ w_out
    return y, yc


def _odd_mixer(h, hc, w_in, w_out, lam_p, subln_g, lam_init, rope, need_ctx):
    lf = lam_p.astype(jnp.float32)
    lam = jnp.exp(jnp.sum(lf[0] * lf[1])) - jnp.exp(jnp.sum(lf[2] * lf[3])) + lam_init

    def project(z, use_rope):
        bz, t, _ = z.shape
        q, k, v = _split(z @ w_in, (C_QK, C_QK, C_V))
        q = q.reshape(bz, t, 2 * DIFF_HEADS, HEAD_DIM)
        k = k.reshape(bz, t, 2 * DIFF_HEADS, HEAD_DIM)
        if use_rope:
            q, k = _axial_rope(q, rope), _axial_rope(k, rope)
        q = q.reshape(bz, t, DIFF_HEADS, 2, HEAD_DIM)
        k = k.reshape(bz, t, DIFF_HEADS, 2, HEAD_DIM)
        return q[..., 0, :], q[..., 1, :], k[..., 0, :], k[..., 1, :], v.reshape(bz, t, DIFF_HEADS, DIFF_DV)

    def finish(o):
        bz, t = o.shape[:2]
        return (_rmsnorm(o, subln_g) * (1.0 - lam_init)).reshape(bz, t, ODD_MIX) @ w_out

    q1, q2, k1, k2, v = project(h, True)
    q1c, q2c, k1c, k2c, vc = project(hc, False)
    y = finish(_diff_attend(q1, q2,
                            jnp.concatenate([k1c, k1], axis=1),
                            jnp.concatenate([k2c, k2], axis=1),
                            jnp.concatenate([vc, v], axis=1), lam))
    yc = finish(_diff_attend(q1c, q2c, k1c, k2c, vc, lam)) if need_ctx else None
    return y, yc


def _peer(h, w_q, keys, u_tab, v_tab):
    n, d = h.shape
    hb = h.reshape(n // PEER_BLOCK, PEER_BLOCK, d)

    def block(xb):
        q = (xb @ w_q).reshape(PEER_BLOCK, PEER_HEADS, 2, PEER_DHALF)
        s = jnp.einsum('thpd,hpnd->thpn', q, keys).astype(jnp.float32)
        s_top, i_top = lax.top_k(s, PEER_TOPK)
        cand = (s_top[:, :, 0, :, None] + s_top[:, :, 1, None, :]).reshape(PEER_BLOCK, PEER_HEADS, PEER_TOPK * PEER_TOPK)
        cid = (i_top[:, :, 0, :, None] * PEER_NKEYS + i_top[:, :, 1, None, :]).reshape(PEER_BLOCK, PEER_HEADS, PEER_TOPK * PEER_TOPK)
        best, pos = lax.top_k(cand, PEER_TOPK)
        eid = jnp.take_along_axis(cid, pos, axis=-1)
        gate = jax.nn.softmax(best, axis=-1)
        act = jax.nn.gelu(jnp.einsum('td,thkd->thk', xb, u_tab[eid]).astype(jnp.float32), approximate=False)
        return jnp.einsum('thk,thkd->td', (gate * act).astype(xb.dtype), v_tab[eid])

    return lax.map(block, hb).reshape(n, d)


def setup_inputs(seed: int = 0) -> dict:
    key = jax.random.key(seed)
    ks = jax.random.split(key, 24)
    f32 = jnp.float32

    def nrm(k, shape, scale):
        return jax.random.normal(k, shape, f32) * scale

    ret_base = jnp.log(-jnp.log1p(-jnp.exp2(-5.0 - jnp.arange(RET_HEADS, dtype=f32))))
    return {
        "x": nrm(ks[0], (BATCH, SEQ, D_MODEL), 1.0),
        "c": nrm(ks[1], (BATCH, D_MODEL), 1.0),
        "ctx": nrm(ks[2], (BATCH, CTX_LEN, D_MODEL), 1.0),
        "c_ctx": nrm(ks[3], (D_MODEL,), 1.0),
        "ada_w": nrm(ks[4], (DEPTH, D_MODEL, 6 * D_MODEL), 0.5 * D_MODEL ** -0.5),
        "ada_b": nrm(ks[5], (DEPTH, 6 * D_MODEL), 0.01),
        "norm1_g": 1.0 + nrm(ks[6], (DEPTH, D_MODEL), 0.02),
        "norm2_g": 1.0 + nrm(ks[7], (DEPTH, D_MODEL), 0.02),
        "ev_w_in": nrm(ks[8], (N_EVEN, D_MODEL, EVEN_IN), D_MODEL ** -0.5),
        "ev_w_out": nrm(ks[9], (N_EVEN, EVEN_MIX, D_MODEL), EVEN_MIX ** -0.5),
        "gqa_q_norm_g": 1.0 + nrm(ks[10], (N_EVEN, HEAD_DIM), 0.02),
        "gqa_k_norm_g": 1.0 + nrm(ks[11], (N_EVEN, HEAD_DIM), 0.02),
        "ret_log_rate": ret_base[None, None, :] + nrm(ks[12], (N_EVEN, 2, RET_HEADS), 0.05),
        "od_w_in": nrm(ks[13], (N_ODD, D_MODEL, ODD_IN), D_MODEL ** -0.5),
        "od_w_out": nrm(ks[14], (N_ODD, ODD_MIX, D_MODEL), ODD_MIX ** -0.5),
        "diff_lambda": nrm(ks[15], (N_ODD, 4, HEAD_DIM), 0.1),
        "diff_subln_g": 1.0 + nrm(ks[16], (N_ODD, DIFF_DV), 0.02),
        "peer_w_q": nrm(ks[17], (DEPTH, D_MODEL, PEER_HEADS * PEER_DKEY), D_MODEL ** -0.5),
        "peer_keys": nrm(ks[18], (DEPTH, PEER_HEADS, 2, PEER_NKEYS, PEER_DHALF), PEER_DHALF ** -0.5),
        "peer_u": nrm(ks[19], (DEPTH, PEER_EXPERTS, D_MODEL), D_MODEL ** -0.5),
        "peer_v": nrm(ks[20], (DEPTH, PEER_EXPERTS, D_MODEL), 0.5),
        "final_g": 1.0 + nrm(ks[21], (D_MODEL,), 0.02),
    }


def reference(x, c, ctx, c_ctx, ada_w, ada_b, norm1_g, norm2_g, ev_w_in, ev_w_out,
              gqa_q_norm_g, gqa_k_norm_g, ret_log_rate, od_w_in, od_w_out, diff_lambda,
              diff_subln_g, peer_w_q, peer_keys, peer_u, peer_v, final_g):
    b, s, d = x.shape
    lc = ctx.shape[1]
    rope = _axial_angles(s)
    xc = ctx
    for l in range(DEPTH):
        need_ctx = l < DEPTH - 1
        mod = jax.nn.silu(c) @ ada_w[l] + ada_b[l]
        mod_c = jax.nn.silu(c_ctx) @ ada_w[l] + ada_b[l]
        sh1, sc1, g1, sh2, sc2, g2 = [m[:, None, :] for m in jnp.split(mod, 6, axis=-1)]
        sh1c, sc1c, g1c, sh2c, sc2c, g2c = [m[None, None, :] for m in jnp.split(mod_c, 6, axis=-1)]
        h = _modulate(_rmsnorm(x, norm1_g[l]), sh1, sc1)
        hc = _modulate(_rmsnorm(xc, norm1_g[l]), sh1c, sc1c)
        if l % 2 == 0:
            e = l // 2
            y, yc = _even_mixer(h, hc, ev_w_in[e], ev_w_out[e], gqa_q_norm_g[e], gqa_k_norm_g[e],
                                ret_log_rate[e], rope, need_ctx)
        else:
            o = l // 2
            lam_init = 0.8 - 0.6 * math.exp(-0.3 * l)
            y, yc = _odd_mixer(h, hc, od_w_in[o], od_w_out[o], diff_lambda[o], diff_subln_g[o],
                               lam_init, rope, need_ctx)
        x = x + (g1 * y).astype(x.dtype)
        h2 = _modulate(_rmsnorm(x, norm2_g[l]), sh2, sc2).reshape(b * s, d)
        if need_ctx:
            xc = xc + (g1c * yc).astype(xc.dtype)
            hc2 = _modulate(_rmsnorm(xc, norm2_g[l]), sh2c, sc2c).reshape(b * lc, d)
            f = _peer(jnp.concatenate([h2, hc2], axis=0), peer_w_q[l], peer_keys[l], peer_u[l], peer_v[l])
            xc = xc + (g2c * f[b * s:].reshape(b, lc, d)).astype(xc.dtype)
        else:
            f = _peer(h2, peer_w_q[l], peer_keys[l], peer_u[l], peer_v[l])
        x = x + (g2 * f[:b * s].reshape(b, s, d)).astype(x.dtype)
    return _rmsnorm(x, final_g)
```

```python
import functools
import math

import jax
import jax.numpy as jnp
from jax import lax
from jax.experimental import pallas as pl
from jax.experimental.pallas import tpu as pltpu

F32 = jnp.float32
BF16 = jnp.bfloat16
I32 = jnp.int32

LANES = 128
SUBLANES = 8
HEAD_DIM = 64
GRID_W = 64
ROPE_THETA = 10000.0
NORM_EPS = 1e-6
RET_CHUNK = 128
PEER_NKEYS = 128
PEER_TOPK = 16
PEER_HEADS = 8
ROW_TILE = 256
PEER_TOK = 128
NEG_BIG = -1e30
VMEM_LIMIT = 48 * 1024 * 1024
VMEM_LIMIT_TABLE = 56 * 1024 * 1024


def _cparams(sem, limit=VMEM_LIMIT):
    return pltpu.CompilerParams(dimension_semantics=sem, vmem_limit_bytes=limit)


def _rms(x):
    return x * lax.rsqrt(jnp.mean(x * x, axis=-1, keepdims=True) + NORM_EPS)


def _dot(a, b):
    return jnp.dot(a, b, preferred_element_type=F32)


def _dot_nt(a, b):
    return lax.dot_general(a, b, (((1,), (1,)), ((), ())), preferred_element_type=F32)


def _sigmoid(x):
    return 1.0 / (1.0 + jnp.exp(-x))


def _ada_kernel(c_ref, w_ref, b_ref, o_ref):
    c = c_ref[...]
    a = (c * _sigmoid(c)).astype(BF16)
    o_ref[...] = _dot(a, w_ref[...].astype(BF16)) + b_ref[...]


def _ada(cc, w, b):
    m, d = cc.shape
    n = w.shape[1]
    tn = 1024
    return pl.pallas_call(
        _ada_kernel,
        grid=(n // tn,),
        in_specs=[pl.BlockSpec((m, d), lambda j: (0, 0)),
                  pl.BlockSpec((d, tn), lambda j: (0, j)),
                  pl.BlockSpec((1, tn), lambda j: (0, j))],
        out_specs=pl.BlockSpec((m, tn), lambda j: (0, j)),
        out_shape=jax.ShapeDtypeStruct((m, n), F32),
        compiler_params=_cparams(("arbitrary",)),
        name="ada_mod",
    )(cc, w, b.reshape(1, n))


def _rope_cols(x, cos, sin_signed):
    lane = lax.broadcasted_iota(I32, x.shape, 1)
    partner = jnp.where((lane % 32) < 16, pltpu.roll(x, LANES - 16, 1), pltpu.roll(x, 16, 1))
    return x * cos + partner * sin_signed


def _head_mean_sq(x):
    r = lax.broadcasted_iota(I32, (LANES, LANES), 0) // HEAD_DIM
    c = lax.broadcasted_iota(I32, (LANES, LANES), 1) // HEAD_DIM
    ones_bd = jnp.where(r == c, 1.0, 0.0).astype(BF16)
    sq = x * x
    hi = sq.astype(BF16)
    lo = (sq - hi.astype(F32)).astype(BF16)
    return (_dot(hi, ones_bd) + _dot(lo, ones_bd)) * (1.0 / HEAD_DIM)


def _prologue(x_ref, g_ref, mod_ref, shift_row, scale_row):
    h = _rms(x_ref[...]) * g_ref[...]
    return h * (1.0 + mod_ref[scale_row:scale_row + 1, :]) + mod_ref[shift_row:shift_row + 1, :]


def _even_inproj_kernel(x_ref, g_ref, mod_ref, w_ref, cos_ref, sin_ref, qg_ref, kg_ref,
                        qa_ref, ka_ref, va_ref, qb_ref, kb_ref, vb_ref, gb_ref):
    h = _prologue(x_ref, g_ref, mod_ref, 0, 1).astype(BF16)
    y = _dot(h, w_ref[...])
    cos = cos_ref[...]
    sin = sin_ref[...]
    scale = HEAD_DIM ** -0.5

    def col(c):
        return y[:, c * LANES:(c + 1) * LANES]

    for c in range(4):
        x = col(c)
        x = x * lax.rsqrt(_head_mean_sq(x) + NORM_EPS) * qg_ref[...]
        qa_ref[:, c * LANES:(c + 1) * LANES] = (_rope_cols(x, cos, sin) * scale).astype(BF16)
    x = col(4)
    x = x * lax.rsqrt(_head_mean_sq(x) + NORM_EPS) * kg_ref[...]
    ka_ref[...] = _rope_cols(x, cos, sin).astype(BF16)
    va_ref[...] = col(5).astype(BF16)
    for c in range(2):
        qb_ref[:, c * LANES:(c + 1) * LANES] = _rope_cols(col(6 + c), cos, sin).astype(BF16)
        kb_ref[:, c * LANES:(c + 1) * LANES] = _rope_cols(col(8 + c) * scale, cos, sin).astype(BF16)
    vb_ref[...] = y[:, 10 * LANES:14 * LANES].astype(BF16)
    gb_ref[...] = y[:, 14 * LANES:18 * LANES]


def _mod_index(nb):
    return lambda b, t: (jnp.where(t == 0, nb, b), 0, 0)


def _even_inproj(x, g, mods, w, cos, sin, qg, kg):
    nb, l, d = x.shape
    tm = ROW_TILE
    n_in = w.shape[1]
    row = lambda width: pl.BlockSpec((None, tm, width), lambda b, t: (b, t, 0))
    widths = (512, 128, 128, 256, 256, 512, 512)
    dts = (BF16, BF16, BF16, BF16, BF16, BF16, F32)
    return pl.pallas_call(
        _even_inproj_kernel,
        grid=(nb, l // tm),
        in_specs=[row(d),
                  pl.BlockSpec((1, d), lambda b, t: (0, 0)),
                  pl.BlockSpec((None, 6, d), _mod_index(nb)),
                  pl.BlockSpec((d, n_in), lambda b, t: (0, 0)),
                  pl.BlockSpec((tm, LANES), lambda b, t: (t, 0)),
                  pl.BlockSpec((tm, LANES), lambda b, t: (t, 0)),
                  pl.BlockSpec((1, LANES), lambda b, t: (0, 0)),
                  pl.BlockSpec((1, LANES), lambda b, t: (0, 0))],
        out_specs=[row(wd) for wd in widths],
        out_shape=[jax.ShapeDtypeStruct((nb, l, wd), dt) for wd, dt in zip(widths, dts)],
        compiler_params=_cparams(("parallel", "parallel")),
        name="even_inproj",
    )(x, g, mods, w, cos, sin, qg, kg)


def _odd_inproj_kernel(x_ref, g_ref, mod_ref, w_ref, cos_ref, sin_ref, q_ref, k_ref, v_ref):
    h = _prologue(x_ref, g_ref, mod_ref, 0, 1).astype(BF16)
    y = _dot(h, w_ref[...])
    cos = cos_ref[...]
    sin = sin_ref[...]
    scale = HEAD_DIM ** -0.5
    for c in range(8):
        sl = slice(c * LANES, (c + 1) * LANES)
        q_ref[:, sl] = (_rope_cols(y[:, sl], cos, sin) * scale).astype(BF16)
        k_ref[:, sl] = _rope_cols(y[:, 8 * LANES + c * LANES:8 * LANES + (c + 1) * LANES], cos, sin).astype(BF16)
    v_ref[...] = y[:, 16 * LANES:24 * LANES].astype(BF16)


def _odd_inproj(x, g, mods, w, cos, sin):
    nb, l, d = x.shape
    tm = ROW_TILE
    n_in = w.shape[1]
    row = lambda width: pl.BlockSpec((None, tm, width), lambda b, t: (b, t, 0))
    return pl.pallas_call(
        _odd_inproj_kernel,
        grid=(nb, l // tm),
        in_specs=[row(d),
                  pl.BlockSpec((1, d), lambda b, t: (0, 0)),
                  pl.BlockSpec((None, 6, d), _mod_index(nb)),
                  pl.BlockSpec((d, n_in), lambda b, t: (0, 0)),
                  pl.BlockSpec((tm, LANES), lambda b, t: (t, 0)),
                  pl.BlockSpec((tm, LANES), lambda b, t: (t, 0))],
        out_specs=[row(d), row(d), row(d)],
        out_shape=[jax.ShapeDtypeStruct((nb, l, d), BF16)] * 3,
        compiler_params=_cparams(("parallel", "parallel")),
        name="odd_inproj",
    )(x, g, mods, w, cos, sin)


def _softmax_rows(s):
    m = jnp.max(s, axis=-1, keepdims=True)
    p = jnp.exp(s - m)
    return p, jnp.sum(p, axis=-1, keepdims=True)


def _gqa_kernel(q_ref, k_ref, v_ref, o_ref, *, ctx_len, tile_off):
    g = pl.program_id(1)
    qi = pl.program_id(2) + tile_off
    k = k_ref[...]
    v = v_ref[...]
    l = k.shape[0]
    tq = q_ref.shape[0]
    klen = jnp.where(qi == 0, ctx_len, l)
    key_ok = lax.broadcasted_iota(I32, (1, l), 1) < klen
    half = lax.broadcasted_iota(I32, (tq, LANES), 1) // HEAD_DIM
    cols = [jnp.zeros((tq, LANES), F32), jnp.zeros((tq, LANES), F32)]
    for i in range(4):
        qh = q_ref[:, (i // 2) * LANES:(i // 2 + 1) * LANES]
        qsel = jnp.where(half == (i % 2), qh, jnp.zeros_like(qh)).astype(F32)
        qpad = jnp.where(g == (i % 2), qsel, pltpu.roll(qsel, HEAD_DIM, 1)).astype(BF16)
        s = jnp.where(key_ok, _dot_nt(qpad, k), NEG_BIG)
        p, den = _softmax_rows(s)
        o = _dot(p.astype(BF16), v) / den
        osel = jnp.where(half == g, o, 0.0)
        cols[i // 2] = cols[i // 2] + jnp.where(g == (i % 2), osel, pltpu.roll(osel, HEAD_DIM, 1))
    o_ref[:, 0:LANES] = cols[0].astype(BF16)
    o_ref[:, LANES:2 * LANES] = cols[1].astype(BF16)


def _gqa(q, k, v, ctx_len, tile_off):
    nb, l, _ = q.shape
    tq = ROW_TILE
    nt = l // tq - tile_off
    return pl.pallas_call(
        functools.partial(_gqa_kernel, ctx_len=ctx_len, tile_off=tile_off),
        grid=(nb, 2, nt),
        in_specs=[pl.BlockSpec((None, tq, 2 * LANES), lambda b, g, t: (b, t + tile_off, g)),
                  pl.BlockSpec((None, l, LANES), lambda b, g, t: (b, 0, 0)),
                  pl.BlockSpec((None, l, LANES), lambda b, g, t: (b, 0, 0))],
        out_specs=pl.BlockSpec((None, tq, 2 * LANES), lambda b, g, t: (b, t + tile_off, g)),
        out_shape=jax.ShapeDtypeStruct((nb, l, 4 * LANES), BF16),
        compiler_params=_cparams(("parallel", "parallel", "parallel")),
        name="gqa_attention",
    )(q, k, v)


def _diff_kernel(q_ref, k_ref, v_ref, lam_ref, g_ref, o_ref, *, ctx_len, tile_off, lam_init):
    qi = pl.program_id(2) + tile_off
    k = k_ref[...]
    l = k.shape[0]
    tq = q_ref.shape[0]
    lf = lam_ref[...]
    lam = (jnp.exp(jnp.sum(lf[0:1, :] * lf[1:2, :], axis=-1, keepdims=True))
           - jnp.exp(jnp.sum(lf[2:3, :] * lf[3:4, :], axis=-1, keepdims=True)) + lam_init)
    klen = jnp.where(qi == 0, ctx_len, l)
    key_ok = lax.broadcasted_iota(I32, (1, l), 1) < klen
    half = lax.broadcasted_iota(I32, (tq, LANES), 1) // HEAD_DIM
    q = q_ref[...]
    zero = jnp.zeros_like(q)
    p1, d1 = _softmax_rows(jnp.where(key_ok, _dot_nt(jnp.where(half == 0, q, zero), k), NEG_BIG))
    p2, d2 = _softmax_rows(jnp.where(key_ok, _dot_nt(jnp.where(half == 1, q, zero), k), NEG_BIG))
    p = (p1 * (1.0 / d1) - (lam * (1.0 / d2)) * p2).astype(BF16)
    o = _dot(p, v_ref[...])
    o_ref[...] = (_rms(o) * g_ref[...] * (1.0 - lam_init)).astype(BF16)


def _diff_attention(q, k, v, lam_p, subln_g, ctx_len, tile_off, lam_init):
    nb, l, d = q.shape
    nh = d // LANES
    tq = ROW_TILE
    nt = l // tq - tile_off
    return pl.pallas_call(
        functools.partial(_diff_kernel, ctx_len=ctx_len, tile_off=tile_off, lam_init=lam_init),
        grid=(nb, nh, nt),
        in_specs=[pl.BlockSpec((None, tq, LANES), lambda b, h, t: (b, t + tile_off, h)),
                  pl.BlockSpec((None, l, LANES), lambda b, h, t: (b, 0, h)),
                  pl.BlockSpec((None, l, LANES), lambda b, h, t: (b, 0, h)),
                  pl.BlockSpec((4, HEAD_DIM), lambda b, h, t: (0, 0)),
                  pl.BlockSpec((1, LANES), lambda b, h, t: (0, 0))],
        out_specs=pl.BlockSpec((None, tq, LANES), lambda b, h, t: (b, t + tile_off, h)),
        out_shape=jax.ShapeDtypeStruct((nb, l, d), BF16),
        compiler_params=_cparams(("parallel", "parallel", "parallel")),
        name="diff_attention",
    )(q, k, v, lam_p, subln_g)


def _retention_kernel(rate_ref, qf_ref, kf_ref, vf_ref, qr_ref, kr_ref, vr_ref,
                      of_ref, or_ref, sf_ref, sr_ref):
    h = pl.program_id(1)
    n = pl.program_id(2)
    c = RET_CHUNK

    @pl.when(n == 0)
    def _():
        sf_ref[...] = jnp.zeros_like(sf_ref)
        sr_ref[...] = jnp.zeros_like(sr_ref)

    head_ok = (lax.broadcasted_iota(I32, (c, LANES), 1) // HEAD_DIM) == (h % 2)
    ri = lax.broadcasted_iota(I32, (c, c), 0)
    ci = lax.broadcasted_iota(I32, (c, c), 1)
    pos = lax.broadcasted_iota(I32, (c, 1), 0).astype(F32)

    def one(direction, q_ref, k_ref, v_ref, o_ref, s_ref):
        log_g = -jnp.exp(jnp.full((1, 1), rate_ref[direction, h], F32))
        rel = ((ri - ci) if direction == 0 else (ci - ri)).astype(F32)
        dmask = jnp.where(rel >= 0, jnp.exp(log_g * jnp.maximum(rel, 0.0)), 0.0)
        q = jnp.where(head_ok, q_ref[...], jnp.zeros((c, LANES), BF16))
        k = jnp.where(head_ok, k_ref[...], jnp.zeros((c, LANES), BF16))
        v = v_ref[...]
        sc = _dot_nt(q, k) * dmask
        intra = _dot(sc.astype(BF16), v)
        q_pow = (pos + 1.0) if direction == 0 else (c - pos)
        k_pow = (c - 1.0 - pos) if direction == 0 else pos
        state = s_ref[...]
        cross = _dot((q.astype(F32) * jnp.exp(log_g * q_pow)).astype(BF16), state.astype(BF16))
        o_ref[...] = intra + cross
        kd = (k.astype(F32) * jnp.exp(log_g * k_pow)).T.astype(BF16)
        s_ref[...] = jnp.exp(log_g * c) * state + _dot(kd, v)

    one(0, qf_ref, kf_ref, vf_ref, of_ref, sf_ref)
    one(1, qr_ref, kr_ref, vr_ref, or_ref, sr_ref)


def _retention(rates, q, k, v, ctx_len):
    nb, l, dv_all = v.shape
    c = RET_CHUNK
    nh = dv_all // LANES
    nctx = ctx_len // c
    ntot = l // c

    def rev(n):
        return jnp.where(n < nctx, nctx - 1 - n, ntot - 1 - (n - nctx))

    qk_f = pl.BlockSpec((None, c, LANES), lambda b, h, n: (b, n, h // 2))
    qk_r = pl.BlockSpec((None, c, LANES), lambda b, h, n: (b, rev(n), h // 2))
    v_f = pl.BlockSpec((None, c, LANES), lambda b, h, n: (b, n, h))
    v_r = pl.BlockSpec((None, c, LANES), lambda b, h, n: (b, rev(n), h))
    return pl.pallas_call(
        _retention_kernel,
        grid=(nb, nh, ntot),
        in_specs=[pl.BlockSpec(memory_space=pltpu.SMEM), qk_f, qk_f, v_f, qk_r, qk_r, v_r],
        out_specs=[v_f, v_r],
        out_shape=[jax.ShapeDtypeStruct((nb, l, dv_all), F32)] * 2,
        scratch_shapes=[pltpu.VMEM((LANES, LANES), F32), pltpu.VMEM((LANES, LANES), F32)],
        compiler_params=_cparams(("parallel", "parallel", "arbitrary")),
        name="retention",
    )(rates, q, k, v, q, k, v)


def _finish_outproj(y, x_ref, mod_ref, g2_ref, x1_ref, h2_ref):
    x1 = x_ref[...] + mod_ref[2:3, :] * y
    x1_ref[...] = x1
    h2_ref[...] = (_rms(x1) * g2_ref[...]) * (1.0 + mod_ref[4:5, :]) + mod_ref[3:4, :]


def _even_outproj_kernel(x_ref, ya_ref, of_ref, or_ref, gb_ref, w_ref, mod_ref, g2_ref, x1_ref, h2_ref):
    half = ya_ref.shape[1]
    y = _dot(ya_ref[...], w_ref[0:half, :])
    for c in range(half // LANES):
        sl = slice(c * LANES, (c + 1) * LANES)
        gate = gb_ref[:, sl]
        r = _rms(of_ref[:, sl] + or_ref[:, sl]) * (gate * _sigmoid(gate))
        y = y + _dot(r.astype(BF16), w_ref[half + c * LANES:half + (c + 1) * LANES, :])
    _finish_outproj(y, x_ref, mod_ref, g2_ref, x1_ref, h2_ref)


def _even_outproj(x, ya, o_f, o_r, gb, w, mods, g2):
    nb, l, d = x.shape
    tm = ROW_TILE
    row = lambda width: pl.BlockSpec((None, tm, width), lambda b, t: (b, t, 0))
    return pl.pallas_call(
        _even_outproj_kernel,
        grid=(nb, l // tm),
        in_specs=[row(d), row(512), row(512), row(512), row(512),
                  pl.BlockSpec((d, d), lambda b, t: (0, 0)),
                  pl.BlockSpec((None, 6, d), _mod_index(nb)),
                  pl.BlockSpec((1, d), lambda b, t: (0, 0))],
        out_specs=[row(d), row(d)],
        out_shape=[jax.ShapeDtypeStruct((nb, l, d), F32)] * 2,
        compiler_params=_cparams(("parallel", "parallel")),
        name="even_outproj",
    )(x, ya, o_f, o_r, gb, w, mods, g2)


def _odd_outproj_kernel(x_ref, mix_ref, w_ref, mod_ref, g2_ref, x1_ref, h2_ref):
    _finish_outproj(_dot(mix_ref[...], w_ref[...]), x_ref, mod_ref, g2_ref, x1_ref, h2_ref)


def _odd_outproj(x, mix, w, mods, g2, tile_off):
    nb, l, d = x.shape
    tm = ROW_TILE
    row = pl.BlockSpec((None, tm, d), lambda b, t: (b, t + tile_off, 0))
    return pl.pallas_call(
        _odd_outproj_kernel,
        grid=(nb, l // tm - tile_off),
        in_specs=[row, row,
                  pl.BlockSpec((d, d), lambda b, t: (0, 0)),
                  pl.BlockSpec((None, 6, d), lambda b, t: (b, 0, 0)),
                  pl.BlockSpec((1, d), lambda b, t: (0, 0))],
        out_specs=[row, row],
        out_shape=[jax.ShapeDtypeStruct((nb, l, d), F32)] * 2,
        compiler_params=_cparams(("parallel", "parallel")),
        name="odd_outproj",
    )(x, mix, w, mods, g2)


def _top_rows(s, k, payload=None):
    rows = lax.broadcasted_iota(I32, s.shape, 0)
    n = s.shape[0]
    vals, pays = [], []
    for _ in range(k):
        m = jnp.max(s, axis=0, keepdims=True)
        am = jnp.min(jnp.where(s == m, rows, n), axis=0, keepdims=True)
        hit = rows == am
        vals.append(m)
        pays.append(am if payload is None else jnp.max(jnp.where(hit, payload, -1), axis=0, keepdims=True))
        s = jnp.where(hit, -jnp.inf, s)
    return jnp.concatenate(vals, axis=0), jnp.concatenate(pays, axis=0)


def _peer_topk_kernel(h_ref, wq_ref, keys_ref, eid_ref, gate_ref):
    kk = PEER_TOPK
    q = _dot(h_ref[...].astype(BF16), wq_ref[...])
    for hd in range(PEER_HEADS):
        tops = []
        for part in range(2):
            c = 2 * hd + part
            qs = q[:, c * LANES:(c + 1) * LANES].astype(BF16)
            tops.append(_top_rows(_dot_nt(keys_ref[c], qs), kk))
        (s1, i1), (s2, i2) = tops
        cand = jnp.concatenate([s1[a:a + 1, :] + s2 for a in range(kk)], axis=0)
        cid = jnp.concatenate([i1[a:a + 1, :] * PEER_NKEYS + i2 for a in range(kk)], axis=0)
        best, eid = _top_rows(cand, kk, cid)
        e = jnp.exp(best - best[0:1, :])
        eid_ref[hd * kk:(hd + 1) * kk, :] = eid
        gate_ref[hd * kk:(hd + 1) * kk, :] = e / jnp.sum(e, axis=0, keepdims=True)


def _flat_tile(nt_all, nt, tile_off):
    return lambda i: (i // nt) * nt_all + (i % nt) + tile_off


def _peer_topk(h2, wq, keys, nb, tile_off):
    n, d = h2.shape
    tm = ROW_TILE
    nt_all = n // nb // tm
    nt = nt_all - tile_off
    ft = _flat_tile(nt_all, nt, tile_off)
    npick = PEER_HEADS * PEER_TOPK
    return pl.pallas_call(
        _peer_topk_kernel,
        grid=(nb * nt,),
        in_specs=[pl.BlockSpec((tm, d), lambda i: (ft(i), 0)),
                  pl.BlockSpec(wq.shape, lambda i: (0, 0)),
                  pl.BlockSpec(keys.shape, lambda i: (0, 0, 0))],
        out_specs=[pl.BlockSpec((npick, tm), lambda i: (0, ft(i))),
                   pl.BlockSpec((npick, tm), lambda i: (0, ft(i)))],
        out_shape=[jax.ShapeDtypeStruct((npick, n), I32), jax.ShapeDtypeStruct((npick, n), F32)],
        compiler_params=_cparams(("parallel",)),
        name="peer_topk",
    )(h2, wq, keys)


def _load_table_half(tab_hbm, tab_ref, sem, half_rows):
    hf = pl.program_id(0)

    @pl.when(pl.program_id(1) == 0)
    def _():
        cp = pltpu.make_async_copy(tab_hbm.at[pl.ds(hf * half_rows, half_rows)], tab_ref, sem)
        cp.start()
        cp.wait()


def _peer_dot_kernel(eid_ref, x_ref, tab_hbm, d_ref, tab_ref, red_ref, sem):
    half_rows = tab_ref.shape[0]
    _load_table_half(tab_hbm, tab_ref, sem, half_rows)
    npick, tb = d_ref.shape
    lane = lax.broadcasted_iota(I32, (npick, tb), 1)

    def token(t, acc):
        x = x_ref[t]
        for p in range(npick):
            row = tab_ref[eid_ref[p, t] & (half_rows - 1)]
            red_ref[p:p + 1, :] = jnp.sum(row * x, axis=0, keepdims=True)
        col = jnp.sum(red_ref[...], axis=1, keepdims=True)
        return jnp.where(lane == t, col, acc)

    d_ref[...] = lax.fori_loop(0, tb, token, jnp.zeros((npick, tb), F32))


def _peer_sum_kernel(eid_ref, w_ref, tab_hbm, f_ref, tab_ref, sem):
    half_rows = tab_ref.shape[0]
    _load_table_half(tab_hbm, tab_ref, sem, half_rows)
    npick, tb = eid_ref.shape
    nacc = 4

    def token(t, carry):
        accs = [jnp.zeros((SUBLANES, LANES), F32) for _ in range(nacc)]
        for p in range(npick):
            row = tab_ref[eid_ref[p, t] & (half_rows - 1)]
            accs[p % nacc] = accs[p % nacc] + w_ref[p, t] * row
        f_ref[t] = (accs[0] + accs[1]) + (accs[2] + accs[3])
        return carry

    lax.fori_loop(0, tb, token, 0)


def _peer_tiles(n, nb, tile_off_rows):
    tb = PEER_TOK
    nt_all = n // nb // tb
    off = tile_off_rows // tb
    nt = nt_all - off
    return tb, nb * nt, _flat_tile(nt_all, nt, off)


def _peer_dot(eid, h2v, tab3, nb, tile_off_rows):
    npick, n = eid.shape
    tb, steps, ft = _peer_tiles(n, nb, tile_off_rows)
    half_rows = tab3.shape[0] // 2
    return pl.pallas_call(
        _peer_dot_kernel,
        grid=(2, steps),
        in_specs=[pl.BlockSpec((npick, tb), lambda hf, i: (0, ft(i)), memory_space=pltpu.SMEM),
                  pl.BlockSpec((tb, SUBLANES, LANES), lambda hf, i: (ft(i), 0, 0)),
                  pl.BlockSpec(memory_space=pl.ANY)],
        out_specs=pl.BlockSpec((None, npick, tb), lambda hf, i: (hf, 0, ft(i))),
        out_shape=jax.ShapeDtypeStruct((2, npick, n), F32),
        scratch_shapes=[pltpu.VMEM((half_rows, SUBLANES, LANES), F32),
                        pltpu.VMEM((npick, LANES), F32),
                        pltpu.SemaphoreType.DMA],
        compiler_params=_cparams(("arbitrary", "arbitrary"), VMEM_LIMIT_TABLE),
        name="peer_dot",
    )(eid, h2v, tab3)


def _peer_weight_kernel(eid_ref, gate_ref, d_ref, w_ref, *, half_rows):
    lo = eid_ref[...] < half_rows
    d = jnp.where(lo, d_ref[0], d_ref[1])
    act = 0.5 * d * (1.0 + lax.erf(d * (2.0 ** -0.5)))
    w = gate_ref[...] * act
    w_ref[0] = jnp.where(lo, w, 0.0)
    w_ref[1] = jnp.where(lo, 0.0, w)


def _peer_weight(eid, gate, dpart, half_rows, nb, tile_off_rows):
    npick, n = eid.shape
    tb, steps, ft = _peer_tiles(n, nb, tile_off_rows)
    blk = pl.BlockSpec((npick, tb), lambda i: (0, ft(i)))
    blk2 = pl.BlockSpec((2, npick, tb), lambda i: (0, 0, ft(i)))
    return pl.pallas_call(
        functools.partial(_peer_weight_kernel, half_rows=half_rows),
        grid=(steps,),
        in_specs=[blk, blk, blk2],
        out_specs=blk2,
        out_shape=jax.ShapeDtypeStruct((2, npick, n), F32),
        compiler_params=_cparams(("parallel",)),
        name="peer_weight",
    )(eid, gate, dpart)


def _peer_sum(eid, w2, tab3, nb, tile_off_rows):
    npick, n = eid.shape
    tb, steps, ft = _peer_tiles(n, nb, tile_off_rows)
    half_rows = tab3.shape[0] // 2
    return pl.pallas_call(
        _peer_sum_kernel,
        grid=(2, steps),
        in_specs=[pl.BlockSpec((npick, tb), lambda hf, i: (0, ft(i)), memory_space=pltpu.SMEM),
                  pl.BlockSpec((None, npick, tb), lambda hf, i: (hf, 0, ft(i)), memory_space=pltpu.SMEM),
                  pl.BlockSpec(memory_space=pl.ANY)],
        out_specs=pl.BlockSpec((None, tb, SUBLANES, LANES), lambda hf, i: (hf, ft(i), 0, 0)),
        out_shape=jax.ShapeDtypeStruct((2, n, SUBLANES, LANES), F32),
        scratch_shapes=[pltpu.VMEM((half_rows, SUBLANES, LANES), F32),
                        pltpu.SemaphoreType.DMA],
        compiler_params=_cparams(("arbitrary", "arbitrary"), VMEM_LIMIT_TABLE),
        name="peer_sum",
    )(eid, w2, tab3)


def _residual_kernel(x_ref, f_ref, mod_ref, o_ref):
    o_ref[...] = x_ref[...] + mod_ref[5:6, :] * (f_ref[0] + f_ref[1])


def _final_kernel(x_ref, f_ref, mod_ref, g_ref, o_ref):
    o_ref[...] = _rms(x_ref[...] + mod_ref[5:6, :] * (f_ref[0] + f_ref[1])) * g_ref[...]


def _residual(x1, f2, mods):
    nb, l, d = x1.shape
    tm = ROW_TILE
    return pl.pallas_call(
        _residual_kernel,
        grid=(nb, l // tm),
        in_specs=[pl.BlockSpec((None, tm, d), lambda b, t: (b, t, 0)),
                  pl.BlockSpec((2, None, tm, d), lambda b, t: (0, b, t, 0)),
                  pl.BlockSpec((None, 6, d), _mod_index(nb))],
        out_specs=pl.BlockSpec((None, tm, d), lambda b, t: (b, t, 0)),
        out_shape=jax.ShapeDtypeStruct((nb, l, d), F32),
        compiler_params=_cparams(("parallel", "parallel")),
        name="peer_residual",
    )(x1, f2, mods)


def _final(x1, f2, mods, g, tile_off):
    nb, l, d = x1.shape
    tm = ROW_TILE
    nt = l // tm - tile_off
    return pl.pallas_call(
        _final_kernel,
        grid=(nb, nt),
        in_specs=[pl.BlockSpec((None, tm, d), lambda b, t: (b, t + tile_off, 0)),
                  pl.BlockSpec((2, None, tm, d), lambda b, t: (0, b, t + tile_off, 0)),
                  pl.BlockSpec((None, 6, d), lambda b, t: (b, 0, 0)),
                  pl.BlockSpec((1, d), lambda b, t: (0, 0))],
        out_specs=pl.BlockSpec((None, tm, d), lambda b, t: (b, t, 0)),
        out_shape=jax.ShapeDtypeStruct((nb, nt * tm, d), F32),
        compiler_params=_cparams(("parallel", "parallel")),
        name="final_norm",
    )(x1, f2, mods, g)


def _rope_tables(seq, ctx_len):
    t = jnp.arange(seq)
    row_id = (t // GRID_W).astype(F32)
    col_id = (t % GRID_W).astype(F32)
    axis_dim = HEAD_DIM // 2
    inv = ROPE_THETA ** (-jnp.arange(0, axis_dim, 2, dtype=F32) / axis_dim)
    ang_r = row_id[:, None] * inv[None, :]
    ang_c = col_id[:, None] * inv[None, :]
    cos = jnp.concatenate([jnp.cos(ang_r)] * 2 + [jnp.cos(ang_c)] * 2, axis=-1)
    sin = jnp.concatenate([-jnp.sin(ang_r), jnp.sin(ang_r), -jnp.sin(ang_c), jnp.sin(ang_c)], axis=-1)
    cos = jnp.concatenate([jnp.ones((ctx_len, HEAD_DIM), F32), cos], axis=0)
    sin = jnp.concatenate([jnp.zeros((ctx_len, HEAD_DIM), F32), sin], axis=0)
    return jnp.tile(cos, (1, 2)), jnp.tile(sin, (1, 2))


def _peer(h2, x1, mods, wq, keys, u_tab, v_tab, nb, tile_off_rows, final_g):
    b, l, d = x1.shape
    n = b * l
    nexp = u_tab.shape[0]
    eid, gate = _peer_topk(h2.reshape(n, d), wq.astype(BF16),
                           keys.reshape(PEER_HEADS * 2, PEER_NKEYS, -1).astype(BF16),
                           nb, tile_off_rows // ROW_TILE)
    h2v = h2.reshape(n, SUBLANES, LANES)
    dpart = _peer_dot(eid, h2v, u_tab.reshape(nexp, SUBLANES, LANES), nb, tile_off_rows)
    w2 = _peer_weight(eid, gate, dpart, nexp // 2, nb, tile_off_rows)
    f2 = _peer_sum(eid, w2, v_tab.reshape(nexp, SUBLANES, LANES), nb, tile_off_rows)
    f2 = f2.reshape(2, b, l, d)
    if final_g is None:
        return _residual(x1, f2, mods)
    return _final(x1, f2, mods, final_g.reshape(1, d), tile_off_rows // ROW_TILE)


def kernel(x, c, ctx, c_ctx, ada_w, ada_b, norm1_g, norm2_g, ev_w_in, ev_w_out, gqa_q_norm_g,
           gqa_k_norm_g, ret_log_rate, od_w_in, od_w_out, diff_lambda, diff_subln_g, peer_w_q,
           peer_keys, peer_u, peer_v, final_g):
    nb, seq, d = x.shape
    ctx_len = ctx.shape[1]
    depth = ada_w.shape[0]
    assert ctx_len == ROW_TILE and seq % ROW_TILE == 0 and d == SUBLANES * LANES
    xs = jnp.concatenate([ctx, x], axis=1)
    cos, sin = _rope_tables(seq, ctx_len)
    cc = jnp.concatenate([c, c_ctx[None, :]], axis=0)
    for layer in range(depth):
        last = layer == depth - 1
        mods = _ada(cc, ada_w[layer], ada_b[layer]).reshape(nb + 1, 6, d)
        g1 = norm1_g[layer].reshape(1, d)
        g2 = norm2_g[layer].reshape(1, d)
        lat_off = 1 if last else 0
        if layer % 2 == 0:
            e = layer // 2
            qa, ka, va, qb, kb, vb, gb = _even_inproj(
                xs, g1, mods, ev_w_in[e].astype(BF16), cos, sin,
                jnp.tile(gqa_q_norm_g[e], 2).reshape(1, LANES), jnp.tile(gqa_k_norm_g[e], 2).reshape(1, LANES))
            ya = _gqa(qa, ka, va, ctx_len, 0)
            o_f, o_r = _retention(ret_log_rate[e], qb, kb, vb, ctx_len)
            x1, h2 = _even_outproj(xs, ya, o_f, o_r, gb, ev_w_out[e].astype(BF16), mods, g2)
        else:
            o = layer // 2
            lam_init = 0.8 - 0.6 * math.exp(-0.3 * layer)
            q, k, v = _odd_inproj(xs, g1, mods, od_w_in[o].astype(BF16), cos, sin)
            mix = _diff_attention(q, k, v, diff_lambda[o], diff_subln_g[o].reshape(1, LANES),
                                  ctx_len, lat_off, lam_init)
            x1, h2 = _odd_outproj(xs, mix, od_w_out[o].astype(BF16), mods, g2, lat_off)
        xs = _peer(h2, x1, mods, peer_w_q[layer], peer_keys[layer], peer_u[layer], peer_v[layer],
                   nb, lat_off * ROW_TILE, final_g if last else None)
    return xs
```

```python
import functools
import math

import jax
import jax.numpy as jnp
from jax import lax
from jax.experimental import pallas as pl
from jax.experimental.pallas import tpu as pltpu

F32 = jnp.float32
BF16 = jnp.bfloat16
I32 = jnp.int32

LANES = 128
SUBLANES = 8
HEAD_DIM = 64
GRID_W = 64
ROPE_THETA = 10000.0
NORM_EPS = 1e-6
RET_CHUNK = 128
PEER_NKEYS = 128
PEER_TOPK = 16
PEER_HEADS = 8
ROW_TILE = 256
PEER_TOK = 128
NEG_BIG = -1e30
VMEM_LIMIT = 48 * 1024 * 1024
VMEM_LIMIT_TABLE = 56 * 1024 * 1024


def _cparams(sem, limit=VMEM_LIMIT):
    return pltpu.CompilerParams(dimension_semantics=sem, vmem_limit_bytes=limit)


def _rms(x):
    return x * lax.rsqrt(jnp.mean(x * x, axis=-1, keepdims=True) + NORM_EPS)


def _dot(a, b):
    return jnp.dot(a, b, preferred_element_type=F32)


def _dot_nt(a, b):
    return lax.dot_general(a, b, (((1,), (1,)), ((), ())), preferred_element_type=F32)


def _sigmoid(x):
    return 1.0 / (1.0 + jnp.exp(-x))


def _ada_kernel(c_ref, w_ref, b_ref, o_ref):
    c = c_ref[...]
    a = (c * _sigmoid(c)).astype(BF16)
    o_ref[...] = _dot(a, w_ref[...].astype(BF16)) + b_ref[...]


def _ada(cc, w, b):
    m, d = cc.shape
    n = w.shape[1]
    tn = 1024
    return pl.pallas_call(
        _ada_kernel,
        grid=(n // tn,),
        in_specs=[pl.BlockSpec((m, d), lambda j: (0, 0)),
                  pl.BlockSpec((d, tn), lambda j: (0, j)),
                  pl.BlockSpec((1, tn), lambda j: (0, j))],
        out_specs=pl.BlockSpec((m, tn), lambda j: (0, j)),
        out_shape=jax.ShapeDtypeStruct((m, n), F32),
        compiler_params=_cparams(("arbitrary",)),
        name="ada_mod",
    )(cc, w, b.reshape(1, n))


def _rope_cols(x, cos, sin_signed):
    lane = lax.broadcasted_iota(I32, x.shape, 1)
    partner = jnp.where((lane % 32) < 16, pltpu.roll(x, LANES - 16, 1), pltpu.roll(x, 16, 1))
    return x * cos + partner * sin_signed


def _head_mean_sq(x):
    r = lax.broadcasted_iota(I32, (LANES, LANES), 0) // HEAD_DIM
    c = lax.broadcasted_iota(I32, (LANES, LANES), 1) // HEAD_DIM
    ones_bd = jnp.where(r == c, 1.0, 0.0).astype(BF16)
    sq = x * x
    hi = sq.astype(BF16)
    lo = (sq - hi.astype(F32)).astype(BF16)
    return (_dot(hi, ones_bd) + _dot(lo, ones_bd)) * (1.0 / HEAD_DIM)


def _prologue(x_ref, g_ref, mod_ref, shift_row, scale_row):
    h = _rms(x_ref[...]) * g_ref[...]
    return h * (1.0 + mod_ref[scale_row:scale_row + 1, :]) + mod_ref[shift_row:shift_row + 1, :]


def _even_inproj_kernel(x_ref, g_ref, mod_ref, w_ref, cos_ref, sin_ref, qg_ref, kg_ref,
                        qa_ref, ka_ref, va_ref, qb_ref, kb_ref, vb_ref, gb_ref):
    h = _prologue(x_ref, g_ref, mod_ref, 0, 1).astype(BF16)
    y = _dot(h, w_ref[...])
    cos = cos_ref[...]
    sin = sin_ref[...]
    scale = HEAD_DIM ** -0.5

    def col(c):
        return y[:, c * LANES:(c + 1) * LANES]

    for c in range(4):
        x = col(c)
        x = x * lax.rsqrt(_head_mean_sq(x) + NORM_EPS) * qg_ref[...]
        qa_ref[:, c * LANES:(c + 1) * LANES] = (_rope_cols(x, cos, sin) * scale).astype(BF16)
    x = col(4)
    x = x * lax.rsqrt(_head_mean_sq(x) + NORM_EPS) * kg_ref[...]
    ka_ref[...] = _rope_cols(x, cos, sin).astype(BF16)
    va_ref[...] = col(5).astype(BF16)
    for c in range(2):
        qb_ref[:, c * LANES:(c + 1) * LANES] = _rope_cols(col(6 + c), cos, sin).astype(BF16)
        kb_ref[:, c * LANES:(c + 1) * LANES] = _rope_cols(col(8 + c) * scale, cos, sin).astype(BF16)
    vb_ref[...] = y[:, 10 * LANES:14 * LANES].astype(BF16)
    gb_ref[...] = y[:, 14 * LANES:18 * LANES]


def _mod_index(nb):
    return lambda b, t: (jnp.where(t == 0, nb, b), 0, 0)


def _even_inproj(x, g, mods, w, cos, sin, qg, kg):
    nb, l, d = x.shape
    tm = ROW_TILE
    n_in = w.shape[1]
    row = lambda width: pl.BlockSpec((None, tm, width), lambda b, t: (b, t, 0))
    widths = (512, 128, 128, 256, 256, 512, 512)
    dts = (BF16, BF16, BF16, BF16, BF16, BF16, F32)
    return pl.pallas_call(
        _even_inproj_kernel,
        grid=(nb, l // tm),
        in_specs=[row(d),
                  pl.BlockSpec((1, d), lambda b, t: (0, 0)),
                  pl.BlockSpec((None, 6, d), _mod_index(nb)),
                  pl.BlockSpec((d, n_in), lambda b, t: (0, 0)),
                  pl.BlockSpec((tm, LANES), lambda b, t: (t, 0)),
                  pl.BlockSpec((tm, LANES), lambda b, t: (t, 0)),
                  pl.BlockSpec((1, LANES), lambda b, t: (0, 0)),
                  pl.BlockSpec((1, LANES), lambda b, t: (0, 0))],
        out_specs=[row(wd) for wd in widths],
        out_shape=[jax.ShapeDtypeStruct((nb, l, wd), dt) for wd, dt in zip(widths, dts)],
        compiler_params=_cparams(("parallel", "parallel")),
        name="even_inproj",
    )(x, g, mods, w, cos, sin, qg, kg)


def _odd_inproj_kernel(x_ref, g_ref, mod_ref, w_ref, cos_ref, sin_ref, q_ref, k_ref, v_ref):
    h = _prologue(x_ref, g_ref, mod_ref, 0, 1).astype(BF16)
    y = _dot(h, w_ref[...])
    cos = cos_ref[...]
    sin = sin_ref[...]
    scale = HEAD_DIM ** -0.5
    for c in range(8):
        sl = slice(c * LANES, (c + 1) * LANES)
        q_ref[:, sl] = (_rope_cols(y[:, sl], cos, sin) * scale).astype(BF16)
        k_ref[:, sl] = _rope_cols(y[:, 8 * LANES + c * LANES:8 * LANES + (c + 1) * LANES], cos, sin).astype(BF16)
    v_ref[...] = y[:, 16 * LANES:24 * LANES].astype(BF16)


def _odd_inproj(x, g, mods, w, cos, sin):
    nb, l, d = x.shape
    tm = ROW_TILE
    n_in = w.shape[1]
    row = lambda width: pl.BlockSpec((None, tm, width), lambda b, t: (b, t, 0))
    return pl.pallas_call(
        _odd_inproj_kernel,
        grid=(nb, l // tm),
        in_specs=[row(d),
                  pl.BlockSpec((1, d), lambda b, t: (0, 0)),
                  pl.BlockSpec((None, 6, d), _mod_index(nb)),
                  pl.BlockSpec((d, n_in), lambda b, t: (0, 0)),
                  pl.BlockSpec((tm, LANES), lambda b, t: (t, 0)),
                  pl.BlockSpec((tm, LANES), lambda b, t: (t, 0))],
        out_specs=[row(d), row(d), row(d)],
        out_shape=[jax.ShapeDtypeStruct((nb, l, d), BF16)] * 3,
        compiler_params=_cparams(("parallel", "parallel")),
        name="odd_inproj",
    )(x, g, mods, w, cos, sin)


def _softmax_rows(s):
    m = jnp.max(s, axis=-1, keepdims=True)
    p = jnp.exp(s - m)
    return p, jnp.sum(p, axis=-1, keepdims=True)


def _gqa_kernel(q_ref, k_ref, v_ref, o_ref, *, ctx_len, tile_off):
    g = pl.program_id(1)
    qi = pl.program_id(2) + tile_off
    k = k_ref[...]
    v = v_ref[...]
    l = k.shape[0]
    tq = q_ref.shape[0]
    klen = jnp.where(qi == 0, ctx_len, l)
    key_ok = lax.broadcasted_iota(I32, (1, l), 1) < klen
    half = lax.broadcasted_iota(I32, (tq, LANES), 1) // HEAD_DIM
    cols = [jnp.zeros((tq, LANES), F32), jnp.zeros((tq, LANES), F32)]
    for i in range(4):
        qh = q_ref[:, (i // 2) * LANES:(i // 2 + 1) * LANES]
        qsel = jnp.where(half == (i % 2), qh, jnp.zeros_like(qh)).astype(F32)
        qpad = jnp.where(g == (i % 2), qsel, pltpu.roll(qsel, HEAD_DIM, 1)).astype(BF16)
        s = jnp.where(key_ok, _dot_nt(qpad, k), NEG_BIG)
        p, den = _softmax_rows(s)
        o = _dot(p.astype(BF16), v) / den
        osel = jnp.where(half == g, o, 0.0)
        cols[i // 2] = cols[i // 2] + jnp.where(g == (i % 2), osel, pltpu.roll(osel, HEAD_DIM, 1))
    o_ref[:, 0:LANES] = cols[0].astype(BF16)
    o_ref[:, LANES:2 * LANES] = cols[1].astype(BF16)


def _gqa(q, k, v, ctx_len, tile_off):
    nb, l, _ = q.shape
    tq = ROW_TILE
    nt = l // tq - tile_off
    return pl.pallas_call(
        functools.partial(_gqa_kernel, ctx_len=ctx_len, tile_off=tile_off),
        grid=(nb, 2, nt),
        in_specs=[pl.BlockSpec((None, tq, 2 * LANES), lambda b, g, t: (b, t + tile_off, g)),
                  pl.BlockSpec((None, l, LANES), lambda b, g, t: (b, 0, 0)),
                  pl.BlockSpec((None, l, LANES), lambda b, g, t: (b, 0, 0))],
        out_specs=pl.BlockSpec((None, tq, 2 * LANES), lambda b, g, t: (b, t + tile_off, g)),
        out_shape=jax.ShapeDtypeStruct((nb, l, 4 * LANES), BF16),
        compiler_params=_cparams(("parallel", "parallel", "parallel")),
        name="gqa_attention",
    )(q, k, v)


def _diff_kernel(q_ref, k_ref, v_ref, lam_ref, g_ref, o_ref, *, ctx_len, tile_off, lam_init):
    qi = pl.program_id(2) + tile_off
    k = k_ref[...]
    l = k.shape[0]
    tq = q_ref.shape[0]
    lf = lam_ref[...]
    lam = (jnp.exp(jnp.sum(lf[0:1, :] * lf[1:2, :], axis=-1, keepdims=True))
           - jnp.exp(jnp.sum(lf[2:3, :] * lf[3:4, :], axis=-1, keepdims=True)) + lam_init)
    klen = jnp.where(qi == 0, ctx_len, l)
    key_ok = lax.broadcasted_iota(I32, (1, l), 1) < klen
    half = lax.broadcasted_iota(I32, (tq, LANES), 1) // HEAD_DIM
    q = q_ref[...]
    zero = jnp.zeros_like(q)
    p1, d1 = _softmax_rows(jnp.where(key_ok, _dot_nt(jnp.where(half == 0, q, zero), k), NEG_BIG))
    p2, d2 = _softmax_rows(jnp.where(key_ok, _dot_nt(jnp.where(half == 1, q, zero), k), NEG_BIG))
    p = (p1 * (1.0 / d1) - (lam * (1.0 / d2)) * p2).astype(BF16)
    o = _dot(p, v_ref[...])
    o_ref[...] = (_rms(o) * g_ref[...] * (1.0 - lam_init)).astype(BF16)


def _diff_attention(q, k, v, lam_p, subln_g, ctx_len, tile_off, lam_init):
    nb, l, d = q.shape
    nh = d // LANES
    tq = ROW_TILE
    nt = l // tq - tile_off
    return pl.pallas_call(
        functools.partial(_diff_kernel, ctx_len=ctx_len, tile_off=tile_off, lam_init=lam_init),
        grid=(nb, nh, nt),
        in_specs=[pl.BlockSpec((None, tq, LANES), lambda b, h, t: (b, t + tile_off, h)),
                  pl.BlockSpec((None, l, LANES), lambda b, h, t: (b, 0, h)),
                  pl.BlockSpec((None, l, LANES), lambda b, h, t: (b, 0, h)),
                  pl.BlockSpec((4, HEAD_DIM), lambda b, h, t: (0, 0)),
                  pl.BlockSpec((1, LANES), lambda b, h, t: (0, 0))],
        out_specs=pl.BlockSpec((None, tq, LANES), lambda b, h, t: (b, t + tile_off, h)),
        out_shape=jax.ShapeDtypeStruct((nb, l, d), BF16),
        compiler_params=_cparams(("parallel", "parallel", "parallel")),
        name="diff_attention",
    )(q, k, v, lam_p, subln_g)


def _retention_kernel(rate_ref, qf_ref, kf_ref, vf_ref, qr_ref, kr_ref, vr_ref,
                      of_ref, or_ref, sf_ref, sr_ref):
    h = pl.program_id(1)
    n = pl.program_id(2)
    c = RET_CHUNK

    @pl.when(n == 0)
    def _():
        sf_ref[...] = jnp.zeros_like(sf_ref)
        sr_ref[...] = jnp.zeros_like(sr_ref)

    head_ok = (lax.broadcasted_iota(I32, (c, LANES), 1) // HEAD_DIM) == (h % 2)
    ri = lax.broadcasted_iota(I32, (c, c), 0)
    ci = lax.broadcasted_iota(I32, (c, c), 1)
    pos = lax.broadcasted_iota(I32, (c, 1), 0).astype(F32)

    def one(direction, q_ref, k_ref, v_ref, o_ref, s_ref):
        log_g = -jnp.exp(jnp.full((1, 1), rate_ref[direction, h], F32))
        rel = ((ri - ci) if direction == 0 else (ci - ri)).astype(F32)
        dmask = jnp.where(rel >= 0, jnp.exp(log_g * jnp.maximum(rel, 0.0)), 0.0)
        q = jnp.where(head_ok, q_ref[...], jnp.zeros((c, LANES), BF16))
        k = jnp.where(head_ok, k_ref[...], jnp.zeros((c, LANES), BF16))
        v = v_ref[...]
        sc = _dot_nt(q, k) * dmask
        intra = _dot(sc.astype(BF16), v)
        q_pow = (pos + 1.0) if direction == 0 else (c - pos)
        k_pow = (c - 1.0 - pos) if direction == 0 else pos
        state = s_ref[...]
        cross = _dot((q.astype(F32) * jnp.exp(log_g * q_pow)).astype(BF16), state.astype(BF16))
        o_ref[...] = intra + cross
        kd = (k.astype(F32) * jnp.exp(log_g * k_pow)).T.astype(BF16)
        s_ref[...] = jnp.exp(log_g * c) * state + _dot(kd, v)

    one(0, qf_ref, kf_ref, vf_ref, of_ref, sf_ref)
    one(1, qr_ref, kr_ref, vr_ref, or_ref, sr_ref)


def _retention(rates, q, k, v, ctx_len):
    nb, l, dv_all = v.shape
    c = RET_CHUNK
    nh = dv_all // LANES
    nctx = ctx_len // c
    ntot = l // c

    def rev(n):
        return jnp.where(n < nctx, nctx - 1 - n, ntot - 1 - (n - nctx))

    qk_f = pl.BlockSpec((None, c, LANES), lambda b, h, n: (b, n, h // 2))
    qk_r = pl.BlockSpec((None, c, LANES), lambda b, h, n: (b, rev(n), h // 2))
    v_f = pl.BlockSpec((None, c, LANES), lambda b, h, n: (b, n, h))
    v_r = pl.BlockSpec((None, c, LANES), lambda b, h, n: (b, rev(n), h))
    return pl.pallas_call(
        _retention_kernel,
        grid=(nb, nh, ntot),
        in_specs=[pl.BlockSpec(memory_space=pltpu.SMEM), qk_f, qk_f, v_f, qk_r, qk_r, v_r],
        out_specs=[v_f, v_r],
        out_shape=[jax.ShapeDtypeStruct((nb, l, dv_all), F32)] * 2,
        scratch_shapes=[pltpu.VMEM((LANES, LANES), F32), pltpu.VMEM((LANES, LANES), F32)],
        compiler_params=_cparams(("parallel", "parallel", "arbitrary")),
        name="retention",
    )(rates, q, k, v, q, k, v)


def _finish_outproj(y, x_ref, mod_ref, g2_ref, x1_ref, h2_ref):
    x1 = x_ref[...] + mod_ref[2:3, :] * y
    x1_ref[...] = x1
    h2_ref[...] = (_rms(x1) * g2_ref[...]) * (1.0 + mod_ref[4:5, :]) + mod_ref[3:4, :]


def _even_outproj_kernel(x_ref, ya_ref, of_ref, or_ref, gb_ref, w_ref, mod_ref, g2_ref, x1_ref, h2_ref):
    half = ya_ref.shape[1]
    y = _dot(ya_ref[...], w_ref[0:half, :])
    for c in range(half // LANES):
        sl = slice(c * LANES, (c + 1) * LANES)
        gate = gb_ref[:, sl]
        r = _rms(of_ref[:, sl] + or_ref[:, sl]) * (gate * _sigmoid(gate))
        y = y + _dot(r.astype(BF16), w_ref[half + c * LANES:half + (c + 1) * LANES, :])
    _finish_outproj(y, x_ref, mod_ref, g2_ref, x1_ref, h2_ref)


def _even_outproj(x, ya, o_f, o_r, gb, w, mods, g2):
    nb, l, d = x.shape
    tm = ROW_TILE
    row = lambda width: pl.BlockSpec((None, tm, width), lambda b, t: (b, t, 0))
    return pl.pallas_call(
        _even_outproj_kernel,
        grid=(nb, l // tm),
        in_specs=[row(d), row(512), row(512), row(512), row(512),
                  pl.BlockSpec((d, d), lambda b, t: (0, 0)),
                  pl.BlockSpec((None, 6, d), _mod_index(nb)),
                  pl.BlockSpec((1, d), lambda b, t: (0, 0))],
        out_specs=[row(d), row(d)],
        out_shape=[jax.ShapeDtypeStruct((nb, l, d), F32)] * 2,
        compiler_params=_cparams(("parallel", "parallel")),
        name="even_outproj",
    )(x, ya, o_f, o_r, gb, w, mods, g2)


def _odd_outproj_kernel(x_ref, mix_ref, w_ref, mod_ref, g2_ref, x1_ref, h2_ref):
    _finish_outproj(_dot(mix_ref[...], w_ref[...]), x_ref, mod_ref, g2_ref, x1_ref, h2_ref)


def _odd_outproj(x, mix, w, mods, g2, tile_off):
    nb, l, d = x.shape
    tm = ROW_TILE
    row = pl.BlockSpec((None, tm, d), lambda b, t: (b, t + tile_off, 0))
    return pl.pallas_call(
        _odd_outproj_kernel,
        grid=(nb, l // tm - tile_off),
        in_specs=[row, row,
                  pl.BlockSpec((d, d), lambda b, t: (0, 0)),
                  pl.BlockSpec((None, 6, d), lambda b, t: (b, 0, 0)),
                  pl.BlockSpec((1, d), lambda b, t: (0, 0))],
        out_specs=[row, row],
        out_shape=[jax.ShapeDtypeStruct((nb, l, d), F32)] * 2,
        compiler_params=_cparams(("parallel", "parallel")),
        name="odd_outproj",
    )(x, mix, w, mods, g2)


def _top_rows(s, k, payload=None):
    rows = lax.broadcasted_iota(I32, s.shape, 0).astype(F32)
    n = float(s.shape[0])
    vals, pays = [], []
    for _ in range(k):
        m = jnp.max(s, axis=0, keepdims=True)
        am = jnp.min(jnp.where(s == m, rows, n), axis=0, keepdims=True)
        hit = rows == am
        vals.append(m)
        pays.append(am if payload is None else jnp.max(jnp.where(hit, payload, -1.0), axis=0, keepdims=True))
        s = jnp.where(hit, -jnp.inf, s)
    return jnp.concatenate(vals, axis=0), jnp.concatenate(pays, axis=0)


def _pair_candidates(s1, i1, s2, i2):
    kk = PEER_TOPK
    sub = lax.broadcasted_iota(I32, (SUBLANES, s1.shape[1]), 0)
    cand = [s1[0:1, :] + s2]
    cid = [i1[0:1, :] * PEER_NKEYS + i2]
    for a in range(1, SUBLANES):
        ok = sub < kk // (a + 1)
        cand.append(jnp.where(ok, s1[a:a + 1, :] + s2[0:SUBLANES, :], -jnp.inf))
        cid.append(i1[a:a + 1, :] * PEER_NKEYS + i2[0:SUBLANES, :])
    cand.append(s1[SUBLANES:kk, :] + s2[0:1, :])
    cid.append(i1[SUBLANES:kk, :] * PEER_NKEYS + i2[0:1, :])
    return jnp.concatenate(cand, axis=0), jnp.concatenate(cid, axis=0)


def _peer_topk_kernel(h_ref, wq_ref, keys_ref, eid_ref, gate_ref, off_ref, *, half_rows):
    kk = PEER_TOPK
    q = _dot(h_ref[...].astype(BF16), wq_ref[...])
    eids, gates = [], []
    for hd in range(PEER_HEADS):
        tops = []
        for part in range(2):
            c = 2 * hd + part
            qs = q[:, c * LANES:(c + 1) * LANES].astype(BF16)
            tops.append(_top_rows(_dot_nt(keys_ref[c], qs), kk))
        (s1, i1), (s2, i2) = tops
        cand, cid = _pair_candidates(s1, i1, s2, i2)
        best, eid = _top_rows(cand, kk, cid)
        e = jnp.exp(best - best[0:1, :])
        eids.append(eid)
        gates.append(e / jnp.sum(e, axis=0, keepdims=True))
    eid = jnp.concatenate(eids, axis=0).T.astype(I32)
    eid_ref[...] = eid
    gate_ref[...] = jnp.concatenate(gates, axis=0).T
    off_ref[...] = (eid & (half_rows - 1)) * SUBLANES


def _flat_tile(nt_all, nt, tile_off):
    return lambda i: (i // nt) * nt_all + (i % nt) + tile_off


def _peer_topk(h2, wq, keys, half_rows, nb, tile_off):
    n, d = h2.shape
    tm = ROW_TILE
    nt_all = n // nb // tm
    nt = nt_all - tile_off
    ft = _flat_tile(nt_all, nt, tile_off)
    npick = PEER_HEADS * PEER_TOPK
    out = pl.BlockSpec((tm, npick), lambda i: (ft(i), 0))
    return pl.pallas_call(
        functools.partial(_peer_topk_kernel, half_rows=half_rows),
        grid=(nb * nt,),
        in_specs=[pl.BlockSpec((tm, d), lambda i: (ft(i), 0)),
                  pl.BlockSpec(wq.shape, lambda i: (0, 0)),
                  pl.BlockSpec(keys.shape, lambda i: (0, 0, 0))],
        out_specs=[out, out, out],
        out_shape=[jax.ShapeDtypeStruct((n, npick), I32), jax.ShapeDtypeStruct((n, npick), F32),
                   jax.ShapeDtypeStruct((n, npick), I32)],
        compiler_params=_cparams(("parallel",)),
        name="peer_topk",
    )(h2, wq, keys)


def _load_table_half(tab_hbm, tab_ref, sem, half_rows):
    hf = pl.program_id(0)

    @pl.when(pl.program_id(1) == 0)
    def _():
        cp = pltpu.make_async_copy(tab_hbm.at[pl.ds(hf * half_rows, half_rows)], tab_ref, sem)
        cp.start()
        cp.wait()


def _table_row(tab_ref, off):
    return tab_ref[pl.ds(pl.multiple_of(off, SUBLANES), SUBLANES), :]


def _sublane_sum_masks():
    sub = lax.broadcasted_iota(I32, (SUBLANES, LANES), 0)
    return ((sub + 2) % 8 < 4,
            (sub + 1) % 8 < 4,
            sub < 4,
            (sub + 7) % 8 < 4,
            sub % 4 < 2,
            (sub + 3) % 4 < 2,
            sub % 2 == 1)


def _sublane_sums(prods, masks):
    def merge(a, b, m, shift):
        return jnp.where(m, a, b) + pltpu.roll(jnp.where(m, b, a), shift, 0)

    s0 = merge(prods[0], prods[4], masks[0], 4)
    s1 = merge(prods[1], prods[5], masks[1], 4)
    s2 = merge(prods[2], prods[6], masks[2], 4)
    s3 = merge(prods[3], prods[7], masks[3], 4)
    u0 = merge(s0, s2, masks[4], 2)
    u1 = merge(s1, s3, masks[5], 2)
    return merge(u0, u1, masks[6], 1)


def _peer_dot_kernel(off_ref, x_ref, tab_hbm, d_ref, tab_ref, q_ref, sem):
    _load_table_half(tab_hbm, tab_ref, sem, tab_ref.shape[0])
    npick, tb = d_ref.shape
    lane = lax.broadcasted_iota(I32, (npick, tb), 1)
    masks = _sublane_sum_masks()

    def gather(t, carry):
        x = x_ref[t]
        offs = off_ref.at[pl.ds(pl.multiple_of(t * npick, npick), npick)]
        for g in range(npick // SUBLANES):
            prods = [_table_row(tab_ref, offs[g * SUBLANES + (k + 1) % SUBLANES]) * x
                     for k in range(SUBLANES)]
            q_ref[t, g * SUBLANES:(g + 1) * SUBLANES, :] = _sublane_sums(prods, masks)
        return carry

    def lane_reduce(t, acc):
        col = jnp.sum(q_ref[t], axis=1, keepdims=True)
        return jnp.where(lane == t, col, acc)

    lax.fori_loop(0, tb, gather, 0)
    d_ref[...] = lax.fori_loop(0, tb, lane_reduce, jnp.zeros((npick, tb), F32), unroll=8)


def _peer_sum_kernel(off_ref, w_ref, tab_hbm, f_ref, tab_ref, sem):
    _load_table_half(tab_hbm, tab_ref, sem, tab_ref.shape[0])
    tb = f_ref.shape[0]
    npick = off_ref.shape[0] // tb
    nacc = 4
    group = 32

    def token(t, carry):
        base = pl.multiple_of(t * npick, npick)

        def picks(g, accs):
            accs = list(accs)
            start = pl.multiple_of(base + g * group, group)
            offs = off_ref.at[pl.ds(start, group)]
            ws = w_ref.at[pl.ds(start, group)]
            for k in range(group):
                accs[k % nacc] = accs[k % nacc] + ws[k] * _table_row(tab_ref, offs[k])
            return tuple(accs)

        accs = lax.fori_loop(0, npick // group, picks,
                             tuple(jnp.zeros((SUBLANES, LANES), F32) for _ in range(nacc)))
        f_ref[t] = (accs[0] + accs[1]) + (accs[2] + accs[3])
        return carry

    lax.fori_loop(0, tb, token, 0)


def _peer_tiles(n, nb, tile_off_rows):
    tb = PEER_TOK
    nt_all = n // nb // tb
    off = tile_off_rows // tb
    nt = nt_all - off
    return tb, nb * nt, _flat_tile(nt_all, nt, off)


def _peer_dot(off_flat, h2v, tab2, npick, nb, tile_off_rows):
    n = h2v.shape[0]
    tb, steps, ft = _peer_tiles(n, nb, tile_off_rows)
    half = tab2.shape[0] // 2
    return pl.pallas_call(
        _peer_dot_kernel,
        grid=(2, steps),
        in_specs=[pl.BlockSpec((tb * npick,), lambda hf, i: (ft(i),), memory_space=pltpu.SMEM),
                  pl.BlockSpec((tb, SUBLANES, LANES), lambda hf, i: (ft(i), 0, 0)),
                  pl.BlockSpec(memory_space=pl.ANY)],
        out_specs=pl.BlockSpec((None, npick, tb), lambda hf, i: (hf, 0, ft(i))),
        out_shape=jax.ShapeDtypeStruct((2, npick, n), F32),
        scratch_shapes=[pltpu.VMEM((half, LANES), F32),
                        pltpu.VMEM((tb, npick, LANES), F32),
                        pltpu.SemaphoreType.DMA],
        compiler_params=_cparams(("arbitrary", "arbitrary"), VMEM_LIMIT_TABLE),
        name="peer_dot",
    )(off_flat, h2v, tab2)


def _peer_weight_kernel(eid_ref, gate_ref, d_ref, w_ref, *, half_rows):
    lo = eid_ref[...] < half_rows
    d = jnp.where(lo, d_ref[0].T, d_ref[1].T)
    act = 0.5 * d * (1.0 + lax.erf(d * (2.0 ** -0.5)))
    w = gate_ref[...] * act
    w_ref[0] = jnp.where(lo, w, 0.0)
    w_ref[1] = jnp.where(lo, 0.0, w)


def _peer_weight(eid, gate, dpart, half_rows, nb, tile_off_rows):
    n, npick = eid.shape
    tb, steps, ft = _peer_tiles(n, nb, tile_off_rows)
    blk = pl.BlockSpec((tb, npick), lambda i: (ft(i), 0))
    return pl.pallas_call(
        functools.partial(_peer_weight_kernel, half_rows=half_rows),
        grid=(steps,),
        in_specs=[blk, blk, pl.BlockSpec((2, npick, tb), lambda i: (0, 0, ft(i)))],
        out_specs=pl.BlockSpec((2, tb, npick), lambda i: (0, ft(i), 0)),
        out_shape=jax.ShapeDtypeStruct((2, n, npick), F32),
        compiler_params=_cparams(("parallel",)),
        name="peer_weight",
    )(eid, gate, dpart)


def _peer_sum(off_flat, w_flat, tab2, n, npick, nb, tile_off_rows):
    tb, steps, ft = _peer_tiles(n, nb, tile_off_rows)
    half = tab2.shape[0] // 2
    blocks_per_half = n // tb
    return pl.pallas_call(
        _peer_sum_kernel,
        grid=(2, steps),
        in_specs=[pl.BlockSpec((tb * npick,), lambda hf, i: (ft(i),), memory_space=pltpu.SMEM),
                  pl.BlockSpec((tb * npick,), lambda hf, i: (hf * blocks_per_half + ft(i),),
                               memory_space=pltpu.SMEM),
                  pl.BlockSpec(memory_space=pl.ANY)],
        out_specs=pl.BlockSpec((None, tb, SUBLANES, LANES), lambda hf, i: (hf, ft(i), 0, 0)),
        out_shape=jax.ShapeDtypeStruct((2, n, SUBLANES, LANES), F32),
        scratch_shapes=[pltpu.VMEM((half, LANES), F32),
                        pltpu.SemaphoreType.DMA],
        compiler_params=_cparams(("arbitrary", "arbitrary"), VMEM_LIMIT_TABLE),
        name="peer_sum",
    )(off_flat, w_flat, tab2)


def _residual_kernel(x_ref, f_ref, mod_ref, o_ref):
    o_ref[...] = x_ref[...] + mod_ref[5:6, :] * (f_ref[0] + f_ref[1])


def _final_kernel(x_ref, f_ref, mod_ref, g_ref, o_ref):
    o_ref[...] = _rms(x_ref[...] + mod_ref[5:6, :] * (f_ref[0] + f_ref[1])) * g_ref[...]


def _residual(x1, f2, mods):
    nb, l, d = x1.shape
    tm = ROW_TILE
    return pl.pallas_call(
        _residual_kernel,
        grid=(nb, l // tm),
        in_specs=[pl.BlockSpec((None, tm, d), lambda b, t: (b, t, 0)),
                  pl.BlockSpec((2, None, tm, d), lambda b, t: (0, b, t, 0)),
                  pl.BlockSpec((None, 6, d), _mod_index(nb))],
        out_specs=pl.BlockSpec((None, tm, d), lambda b, t: (b, t, 0)),
        out_shape=jax.ShapeDtypeStruct((nb, l, d), F32),
        compiler_params=_cparams(("parallel", "parallel")),
        name="peer_residual",
    )(x1, f2, mods)


def _final(x1, f2, mods, g, tile_off):
    nb, l, d = x1.shape
    tm = ROW_TILE
    nt = l // tm - tile_off
    return pl.pallas_call(
        _final_kernel,
        grid=(nb, nt),
        in_specs=[pl.BlockSpec((None, tm, d), lambda b, t: (b, t + tile_off, 0)),
                  pl.BlockSpec((2, None, tm, d), lambda b, t: (0, b, t + tile_off, 0)),
                  pl.BlockSpec((None, 6, d), lambda b, t: (b, 0, 0)),
                  pl.BlockSpec((1, d), lambda b, t: (0, 0))],
        out_specs=pl.BlockSpec((None, tm, d), lambda b, t: (b, t, 0)),
        out_shape=jax.ShapeDtypeStruct((nb, nt * tm, d), F32),
        compiler_params=_cparams(("parallel", "parallel")),
        name="final_norm",
    )(x1, f2, mods, g)


def _rope_tables(seq, ctx_len):
    t = jnp.arange(seq)
    row_id = (t // GRID_W).astype(F32)
    col_id = (t % GRID_W).astype(F32)
    axis_dim = HEAD_DIM // 2
    inv = ROPE_THETA ** (-jnp.arange(0, axis_dim, 2, dtype=F32) / axis_dim)
    ang_r = row_id[:, None] * inv[None, :]
    ang_c = col_id[:, None] * inv[None, :]
    cos = jnp.concatenate([jnp.cos(ang_r)] * 2 + [jnp.cos(ang_c)] * 2, axis=-1)
    sin = jnp.concatenate([-jnp.sin(ang_r), jnp.sin(ang_r), -jnp.sin(ang_c), jnp.sin(ang_c)], axis=-1)
    cos = jnp.concatenate([jnp.ones((ctx_len, HEAD_DIM), F32), cos], axis=0)
    sin = jnp.concatenate([jnp.zeros((ctx_len, HEAD_DIM), F32), sin], axis=0)
    return jnp.tile(cos, (1, 2)), jnp.tile(sin, (1, 2))


def _peer(h2, x1, mods, wq, keys, u_tab, v_tab, nb, tile_off_rows, final_g):
    b, l, d = x1.shape
    n = b * l
    nexp = u_tab.shape[0]
    npick = PEER_HEADS * PEER_TOPK
    eid, gate, off = _peer_topk(h2.reshape(n, d), wq.astype(BF16),
                                keys.reshape(PEER_HEADS * 2, PEER_NKEYS, -1).astype(BF16),
                                nexp // 2, nb, tile_off_rows // ROW_TILE)
    off_flat = off.reshape(n * npick)
    h2v = h2.reshape(n, SUBLANES, LANES)
    dpart = _peer_dot(off_flat, h2v, u_tab.reshape(nexp * SUBLANES, LANES), npick, nb, tile_off_rows)
    w2 = _peer_weight(eid, gate, dpart, nexp // 2, nb, tile_off_rows)
    f2 = _peer_sum(off_flat, w2.reshape(2 * n * npick), v_tab.reshape(nexp * SUBLANES, LANES),
                   n, npick, nb, tile_off_rows)
    f2 = f2.reshape(2, b, l, d)
    if final_g is None:
        return _residual(x1, f2, mods)
    return _final(x1, f2, mods, final_g.reshape(1, d), tile_off_rows // ROW_TILE)


def kernel(x, c, ctx, c_ctx, ada_w, ada_b, norm1_g, norm2_g, ev_w_in, ev_w_out, gqa_q_norm_g,
           gqa_k_norm_g, ret_log_rate, od_w_in, od_w_out, diff_lambda, diff_subln_g, peer_w_q,
           peer_keys, peer_u, peer_v, final_g):
    nb, seq, d = x.shape
    ctx_len = ctx.shape[1]
    depth = ada_w.shape[0]
    assert ctx_len == ROW_TILE and seq % ROW_TILE == 0 and d == SUBLANES * LANES
    xs = jnp.concatenate([ctx, x], axis=1)
    cos, sin = _rope_tables(seq, ctx_len)
    cc = jnp.concatenate([c, c_ctx[None, :]], axis=0)
    for layer in range(depth):
        last = layer == depth - 1
        mods = _ada(cc, ada_w[layer], ada_b[layer]).reshape(nb + 1, 6, d)
        g1 = norm1_g[layer].reshape(1, d)
        g2 = norm2_g[layer].reshape(1, d)
        lat_off = 1 if last else 0
        if layer % 2 == 0:
            e = layer // 2
            qa, ka, va, qb, kb, vb, gb = _even_inproj(
                xs, g1, mods, ev_w_in[e].astype(BF16), cos, sin,
                jnp.tile(gqa_q_norm_g[e], 2).reshape(1, LANES), jnp.tile(gqa_k_norm_g[e], 2).reshape(1, LANES))
            ya = _gqa(qa, ka, va, ctx_len, 0)
            o_f, o_r = _retention(ret_log_rate[e], qb, kb, vb, ctx_len)
            x1, h2 = _even_outproj(xs, ya, o_f, o_r, gb, ev_w_out[e].astype(BF16), mods, g2)
        else:
            o = layer // 2
            lam_init = 0.8 - 0.6 * math.exp(-0.3 * layer)
            q, k, v = _odd_inproj(xs, g1, mods, od_w_in[o].astype(BF16), cos, sin)
            mix = _diff_attention(q, k, v, diff_lambda[o], diff_subln_g[o].reshape(1, LANES),
                                  ctx_len, lat_off, lam_init)
            x1, h2 = _odd_outproj(xs, mix, od_w_out[o].astype(BF16), mods, g2, lat_off)
        xs = _peer(h2, x1, mods, peer_w_q[layer], peer_keys[layer], peer_u[layer], peer_v[layer],
                   nb, lat_off * ROW_TILE, final_g if last else None)
    return xs
```

```python
import functools
import math

import jax
import jax.numpy as jnp
from jax import lax
from jax.experimental import pallas as pl
from jax.experimental.pallas import tpu as pltpu

F32 = jnp.float32
BF16 = jnp.bfloat16
I32 = jnp.int32

LANES = 128
SUBLANES = 8
HEAD_DIM = 64
GRID_W = 64
ROPE_THETA = 10000.0
NORM_EPS = 1e-6
RET_CHUNK = 128
PEER_NKEYS = 128
PEER_TOPK = 16
PEER_HEADS = 8
ROW_TILE = 256
PEER_TOK = 128
PEER_STATIC_GROUPS = 9
NEG_BIG = -1e30
VMEM_LIMIT = 48 * 1024 * 1024
VMEM_LIMIT_TABLE = 56 * 1024 * 1024


def _cparams(sem, limit=VMEM_LIMIT):
    return pltpu.CompilerParams(dimension_semantics=sem, vmem_limit_bytes=limit)


def _rms(x):
    return x * lax.rsqrt(jnp.mean(x * x, axis=-1, keepdims=True) + NORM_EPS)


def _dot(a, b):
    return jnp.dot(a, b, preferred_element_type=F32)


def _dot_nt(a, b):
    return lax.dot_general(a, b, (((1,), (1,)), ((), ())), preferred_element_type=F32)


def _sigmoid(x):
    return 1.0 / (1.0 + jnp.exp(-x))


def _ada_kernel(c_ref, w_ref, b_ref, o_ref):
    c = c_ref[...]
    a = (c * _sigmoid(c)).astype(BF16)
    o_ref[...] = _dot(a, w_ref[...].astype(BF16)) + b_ref[...]


def _ada(cc, w, b):
    m, d = cc.shape
    n = w.shape[1]
    tn = 1024
    return pl.pallas_call(
        _ada_kernel,
        grid=(n // tn,),
        in_specs=[pl.BlockSpec((m, d), lambda j: (0, 0)),
                  pl.BlockSpec((d, tn), lambda j: (0, j)),
                  pl.BlockSpec((1, tn), lambda j: (0, j))],
        out_specs=pl.BlockSpec((m, tn), lambda j: (0, j)),
        out_shape=jax.ShapeDtypeStruct((m, n), F32),
        compiler_params=_cparams(("arbitrary",)),
        name="ada_mod",
    )(cc, w, b.reshape(1, n))


def _rope_cols(x, cos, sin_signed):
    lane = lax.broadcasted_iota(I32, x.shape, 1)
    partner = jnp.where((lane % 32) < 16, pltpu.roll(x, LANES - 16, 1), pltpu.roll(x, 16, 1))
    return x * cos + partner * sin_signed


def _head_mean_sq(x):
    r = lax.broadcasted_iota(I32, (LANES, LANES), 0) // HEAD_DIM
    c = lax.broadcasted_iota(I32, (LANES, LANES), 1) // HEAD_DIM
    ones_bd = jnp.where(r == c, 1.0, 0.0).astype(BF16)
    sq = x * x
    hi = sq.astype(BF16)
    lo = (sq - hi.astype(F32)).astype(BF16)
    return (_dot(hi, ones_bd) + _dot(lo, ones_bd)) * (1.0 / HEAD_DIM)


def _prologue(x_ref, g_ref, mod_ref, shift_row, scale_row):
    h = _rms(x_ref[...]) * g_ref[...]
    return h * (1.0 + mod_ref[scale_row:scale_row + 1, :]) + mod_ref[shift_row:shift_row + 1, :]


def _even_inproj_kernel(x_ref, g_ref, mod_ref, w_ref, cos_ref, sin_ref, qg_ref, kg_ref,
                        qa_ref, ka_ref, va_ref, qb_ref, kb_ref, vb_ref, gb_ref):
    h = _prologue(x_ref, g_ref, mod_ref, 0, 1).astype(BF16)
    y = _dot(h, w_ref[...])
    cos = cos_ref[...]
    sin = sin_ref[...]
    scale = HEAD_DIM ** -0.5

    def col(c):
        return y[:, c * LANES:(c + 1) * LANES]

    for c in range(4):
        x = col(c)
        x = x * lax.rsqrt(_head_mean_sq(x) + NORM_EPS) * qg_ref[...]
        qa_ref[:, c * LANES:(c + 1) * LANES] = (_rope_cols(x, cos, sin) * scale).astype(BF16)
    x = col(4)
    x = x * lax.rsqrt(_head_mean_sq(x) + NORM_EPS) * kg_ref[...]
    ka_ref[...] = _rope_cols(x, cos, sin).astype(BF16)
    va_ref[...] = col(5).astype(BF16)
    for c in range(2):
        qb_ref[:, c * LANES:(c + 1) * LANES] = _rope_cols(col(6 + c), cos, sin).astype(BF16)
        kb_ref[:, c * LANES:(c + 1) * LANES] = _rope_cols(col(8 + c) * scale, cos, sin).astype(BF16)
    vb_ref[...] = y[:, 10 * LANES:14 * LANES].astype(BF16)
    gb_ref[...] = y[:, 14 * LANES:18 * LANES]


def _mod_index(nb):
    return lambda b, t: (jnp.where(t == 0, nb, b), 0, 0)


def _even_inproj(x, g, mods, w, cos, sin, qg, kg):
    nb, l, d = x.shape
    tm = ROW_TILE
    n_in = w.shape[1]
    row = lambda width: pl.BlockSpec((None, tm, width), lambda b, t: (b, t, 0))
    widths = (512, 128, 128, 256, 256, 512, 512)
    dts = (BF16, BF16, BF16, BF16, BF16, BF16, F32)
    return pl.pallas_call(
        _even_inproj_kernel,
        grid=(nb, l // tm),
        in_specs=[row(d),
                  pl.BlockSpec((1, d), lambda b, t: (0, 0)),
                  pl.BlockSpec((None, 6, d), _mod_index(nb)),
                  pl.BlockSpec((d, n_in), lambda b, t: (0, 0)),
                  pl.BlockSpec((tm, LANES), lambda b, t: (t, 0)),
                  pl.BlockSpec((tm, LANES), lambda b, t: (t, 0)),
                  pl.BlockSpec((1, LANES), lambda b, t: (0, 0)),
                  pl.BlockSpec((1, LANES), lambda b, t: (0, 0))],
        out_specs=[row(wd) for wd in widths],
        out_shape=[jax.ShapeDtypeStruct((nb, l, wd), dt) for wd, dt in zip(widths, dts)],
        compiler_params=_cparams(("parallel", "parallel")),
        name="even_inproj",
    )(x, g, mods, w, cos, sin, qg, kg)


def _odd_inproj_kernel(x_ref, g_ref, mod_ref, w_ref, cos_ref, sin_ref, q_ref, k_ref, v_ref):
    h = _prologue(x_ref, g_ref, mod_ref, 0, 1).astype(BF16)
    y = _dot(h, w_ref[...])
    cos = cos_ref[...]
    sin = sin_ref[...]
    scale = HEAD_DIM ** -0.5
    for c in range(8):
        sl = slice(c * LANES, (c + 1) * LANES)
        q_ref[:, sl] = (_rope_cols(y[:, sl], cos, sin) * scale).astype(BF16)
        k_ref[:, sl] = _rope_cols(y[:, 8 * LANES + c * LANES:8 * LANES + (c + 1) * LANES], cos, sin).astype(BF16)
    v_ref[...] = y[:, 16 * LANES:24 * LANES].astype(BF16)


def _odd_inproj(x, g, mods, w, cos, sin):
    nb, l, d = x.shape
    tm = ROW_TILE
    n_in = w.shape[1]
    row = lambda width: pl.BlockSpec((None, tm, width), lambda b, t: (b, t, 0))
    return pl.pallas_call(
        _odd_inproj_kernel,
        grid=(nb, l // tm),
        in_specs=[row(d),
                  pl.BlockSpec((1, d), lambda b, t: (0, 0)),
                  pl.BlockSpec((None, 6, d), _mod_index(nb)),
                  pl.BlockSpec((d, n_in), lambda b, t: (0, 0)),
                  pl.BlockSpec((tm, LANES), lambda b, t: (t, 0)),
                  pl.BlockSpec((tm, LANES), lambda b, t: (t, 0))],
        out_specs=[row(d), row(d), row(d)],
        out_shape=[jax.ShapeDtypeStruct((nb, l, d), BF16)] * 3,
        compiler_params=_cparams(("parallel", "parallel")),
        name="odd_inproj",
    )(x, g, mods, w, cos, sin)


def _softmax_rows(s):
    m = jnp.max(s, axis=-1, keepdims=True)
    p = jnp.exp(s - m)
    return p, jnp.sum(p, axis=-1, keepdims=True)


def _gqa_kernel(q_ref, k_ref, v_ref, o_ref, *, ctx_len, tile_off):
    g = pl.program_id(1)
    qi = pl.program_id(2) + tile_off
    k = k_ref[...]
    v = v_ref[...]
    l = k.shape[0]
    tq = q_ref.shape[0]
    klen = jnp.where(qi == 0, ctx_len, l)
    key_ok = lax.broadcasted_iota(I32, (1, l), 1) < klen
    half = lax.broadcasted_iota(I32, (tq, LANES), 1) // HEAD_DIM
    cols = [jnp.zeros((tq, LANES), F32), jnp.zeros((tq, LANES), F32)]
    for i in range(4):
        qh = q_ref[:, (i // 2) * LANES:(i // 2 + 1) * LANES]
        qsel = jnp.where(half == (i % 2), qh, jnp.zeros_like(qh)).astype(F32)
        qpad = jnp.where(g == (i % 2), qsel, pltpu.roll(qsel, HEAD_DIM, 1)).astype(BF16)
        s = jnp.where(key_ok, _dot_nt(qpad, k), NEG_BIG)
        p, den = _softmax_rows(s)
        o = _dot(p.astype(BF16), v) / den
        osel = jnp.where(half == g, o, 0.0)
        cols[i // 2] = cols[i // 2] + jnp.where(g == (i % 2), osel, pltpu.roll(osel, HEAD_DIM, 1))
    o_ref[:, 0:LANES] = cols[0].astype(BF16)
    o_ref[:, LANES:2 * LANES] = cols[1].astype(BF16)


def _gqa(q, k, v, ctx_len, tile_off):
    nb, l, _ = q.shape
    tq = ROW_TILE
    nt = l // tq - tile_off
    return pl.pallas_call(
        functools.partial(_gqa_kernel, ctx_len=ctx_len, tile_off=tile_off),
        grid=(nb, 2, nt),
        in_specs=[pl.BlockSpec((None, tq, 2 * LANES), lambda b, g, t: (b, t + tile_off, g)),
                  pl.BlockSpec((None, l, LANES), lambda b, g, t: (b, 0, 0)),
                  pl.BlockSpec((None, l, LANES), lambda b, g, t: (b, 0, 0))],
        out_specs=pl.BlockSpec((None, tq, 2 * LANES), lambda b, g, t: (b, t + tile_off, g)),
        out_shape=jax.ShapeDtypeStruct((nb, l, 4 * LANES), BF16),
        compiler_params=_cparams(("parallel", "parallel", "parallel")),
        name="gqa_attention",
    )(q, k, v)


def _diff_kernel(q_ref, k_ref, v_ref, lam_ref, g_ref, o_ref, *, ctx_len, tile_off, lam_init):
    qi = pl.program_id(2) + tile_off
    k = k_ref[...]
    l = k.shape[0]
    tq = q_ref.shape[0]
    lf = lam_ref[...]
    lam = (jnp.exp(jnp.sum(lf[0:1, :] * lf[1:2, :], axis=-1, keepdims=True))
           - jnp.exp(jnp.sum(lf[2:3, :] * lf[3:4, :], axis=-1, keepdims=True)) + lam_init)
    klen = jnp.where(qi == 0, ctx_len, l)
    key_ok = lax.broadcasted_iota(I32, (1, l), 1) < klen
    half = lax.broadcasted_iota(I32, (tq, LANES), 1) // HEAD_DIM
    q = q_ref[...]
    zero = jnp.zeros_like(q)
    p1, d1 = _softmax_rows(jnp.where(key_ok, _dot_nt(jnp.where(half == 0, q, zero), k), NEG_BIG))
    p2, d2 = _softmax_rows(jnp.where(key_ok, _dot_nt(jnp.where(half == 1, q, zero), k), NEG_BIG))
    p = (p1 * (1.0 / d1) - (lam * (1.0 / d2)) * p2).astype(BF16)
    o = _dot(p, v_ref[...])
    o_ref[...] = (_rms(o) * g_ref[...] * (1.0 - lam_init)).astype(BF16)


def _diff_attention(q, k, v, lam_p, subln_g, ctx_len, tile_off, lam_init):
    nb, l, d = q.shape
    nh = d // LANES
    tq = ROW_TILE
    nt = l // tq - tile_off
    return pl.pallas_call(
        functools.partial(_diff_kernel, ctx_len=ctx_len, tile_off=tile_off, lam_init=lam_init),
        grid=(nb, nh, nt),
        in_specs=[pl.BlockSpec((None, tq, LANES), lambda b, h, t: (b, t + tile_off, h)),
                  pl.BlockSpec((None, l, LANES), lambda b, h, t: (b, 0, h)),
                  pl.BlockSpec((None, l, LANES), lambda b, h, t: (b, 0, h)),
                  pl.BlockSpec((4, HEAD_DIM), lambda b, h, t: (0, 0)),
                  pl.BlockSpec((1, LANES), lambda b, h, t: (0, 0))],
        out_specs=pl.BlockSpec((None, tq, LANES), lambda b, h, t: (b, t + tile_off, h)),
        out_shape=jax.ShapeDtypeStruct((nb, l, d), BF16),
        compiler_params=_cparams(("parallel", "parallel", "parallel")),
        name="diff_attention",
    )(q, k, v, lam_p, subln_g)


def _retention_kernel(rate_ref, qf_ref, kf_ref, vf_ref, qr_ref, kr_ref, vr_ref,
                      of_ref, or_ref, sf_ref, sr_ref):
    h = pl.program_id(1)
    n = pl.program_id(2)
    c = RET_CHUNK

    @pl.when(n == 0)
    def _():
        sf_ref[...] = jnp.zeros_like(sf_ref)
        sr_ref[...] = jnp.zeros_like(sr_ref)

    head_ok = (lax.broadcasted_iota(I32, (c, LANES), 1) // HEAD_DIM) == (h % 2)
    ri = lax.broadcasted_iota(I32, (c, c), 0)
    ci = lax.broadcasted_iota(I32, (c, c), 1)
    pos = lax.broadcasted_iota(I32, (c, 1), 0).astype(F32)

    def one(direction, q_ref, k_ref, v_ref, o_ref, s_ref):
        log_g = -jnp.exp(jnp.full((1, 1), rate_ref[direction, h], F32))
        rel = ((ri - ci) if direction == 0 else (ci - ri)).astype(F32)
        dmask = jnp.where(rel >= 0, jnp.exp(log_g * jnp.maximum(rel, 0.0)), 0.0)
        q = jnp.where(head_ok, q_ref[...], jnp.zeros((c, LANES), BF16))
        k = jnp.where(head_ok, k_ref[...], jnp.zeros((c, LANES), BF16))
        v = v_ref[...]
        sc = _dot_nt(q, k) * dmask
        intra = _dot(sc.astype(BF16), v)
        q_pow = (pos + 1.0) if direction == 0 else (c - pos)
        k_pow = (c - 1.0 - pos) if direction == 0 else pos
        state = s_ref[...]
        cross = _dot((q.astype(F32) * jnp.exp(log_g * q_pow)).astype(BF16), state.astype(BF16))
        o_ref[...] = intra + cross
        kd = (k.astype(F32) * jnp.exp(log_g * k_pow)).T.astype(BF16)
        s_ref[...] = jnp.exp(log_g * c) * state + _dot(kd, v)

    one(0, qf_ref, kf_ref, vf_ref, of_ref, sf_ref)
    one(1, qr_ref, kr_ref, vr_ref, or_ref, sr_ref)


def _retention(rates, q, k, v, ctx_len):
    nb, l, dv_all = v.shape
    c = RET_CHUNK
    nh = dv_all // LANES
    nctx = ctx_len // c
    ntot = l // c

    def rev(n):
        return jnp.where(n < nctx, nctx - 1 - n, ntot - 1 - (n - nctx))

    qk_f = pl.BlockSpec((None, c, LANES), lambda b, h, n: (b, n, h // 2))
    qk_r = pl.BlockSpec((None, c, LANES), lambda b, h, n: (b, rev(n), h // 2))
    v_f = pl.BlockSpec((None, c, LANES), lambda b, h, n: (b, n, h))
    v_r = pl.BlockSpec((None, c, LANES), lambda b, h, n: (b, rev(n), h))
    return pl.pallas_call(
        _retention_kernel,
        grid=(nb, nh, ntot),
        in_specs=[pl.BlockSpec(memory_space=pltpu.SMEM), qk_f, qk_f, v_f, qk_r, qk_r, v_r],
        out_specs=[v_f, v_r],
        out_shape=[jax.ShapeDtypeStruct((nb, l, dv_all), F32)] * 2,
        scratch_shapes=[pltpu.VMEM((LANES, LANES), F32), pltpu.VMEM((LANES, LANES), F32)],
        compiler_params=_cparams(("parallel", "parallel", "arbitrary")),
        name="retention",
    )(rates, q, k, v, q, k, v)


def _finish_outproj(y, x_ref, mod_ref, g2_ref, x1_ref, h2_ref):
    x1 = x_ref[...] + mod_ref[2:3, :] * y
    x1_ref[...] = x1
    h2_ref[...] = (_rms(x1) * g2_ref[...]) * (1.0 + mod_ref[4:5, :]) + mod_ref[3:4, :]


def _even_outproj_kernel(x_ref, ya_ref, of_ref, or_ref, gb_ref, w_ref, mod_ref, g2_ref, x1_ref, h2_ref):
    half = ya_ref.shape[1]
    y = _dot(ya_ref[...], w_ref[0:half, :])
    for c in range(half // LANES):
        sl = slice(c * LANES, (c + 1) * LANES)
        gate = gb_ref[:, sl]
        r = _rms(of_ref[:, sl] + or_ref[:, sl]) * (gate * _sigmoid(gate))
        y = y + _dot(r.astype(BF16), w_ref[half + c * LANES:half + (c + 1) * LANES, :])
    _finish_outproj(y, x_ref, mod_ref, g2_ref, x1_ref, h2_ref)


def _even_outproj(x, ya, o_f, o_r, gb, w, mods, g2):
    nb, l, d = x.shape
    tm = ROW_TILE
    row = lambda width: pl.BlockSpec((None, tm, width), lambda b, t: (b, t, 0))
    return pl.pallas_call(
        _even_outproj_kernel,
        grid=(nb, l // tm),
        in_specs=[row(d), row(512), row(512), row(512), row(512),
                  pl.BlockSpec((d, d), lambda b, t: (0, 0)),
                  pl.BlockSpec((None, 6, d), _mod_index(nb)),
                  pl.BlockSpec((1, d), lambda b, t: (0, 0))],
        out_specs=[row(d), row(d)],
        out_shape=[jax.ShapeDtypeStruct((nb, l, d), F32)] * 2,
        compiler_params=_cparams(("parallel", "parallel")),
        name="even_outproj",
    )(x, ya, o_f, o_r, gb, w, mods, g2)


def _odd_outproj_kernel(x_ref, mix_ref, w_ref, mod_ref, g2_ref, x1_ref, h2_ref):
    _finish_outproj(_dot(mix_ref[...], w_ref[...]), x_ref, mod_ref, g2_ref, x1_ref, h2_ref)


def _odd_outproj(x, mix, w, mods, g2, tile_off):
    nb, l, d = x.shape
    tm = ROW_TILE
    row = pl.BlockSpec((None, tm, d), lambda b, t: (b, t + tile_off, 0))
    return pl.pallas_call(
        _odd_outproj_kernel,
        grid=(nb, l // tm - tile_off),
        in_specs=[row, row,
                  pl.BlockSpec((d, d), lambda b, t: (0, 0)),
                  pl.BlockSpec((None, 6, d), lambda b, t: (b, 0, 0)),
                  pl.BlockSpec((1, d), lambda b, t: (0, 0))],
        out_specs=[row, row],
        out_shape=[jax.ShapeDtypeStruct((nb, l, d), F32)] * 2,
        compiler_params=_cparams(("parallel", "parallel")),
        name="odd_outproj",
    )(x, mix, w, mods, g2)


def _top_rows(s, k, payload=None):
    rows = lax.broadcasted_iota(I32, s.shape, 0).astype(F32)
    n = float(s.shape[0])
    vals, pays = [], []
    for _ in range(k):
        m = jnp.max(s, axis=0, keepdims=True)
        am = jnp.min(jnp.where(s == m, rows, n), axis=0, keepdims=True)
        hit = rows == am
        vals.append(m)
        pays.append(am if payload is None else jnp.max(jnp.where(hit, payload, -1.0), axis=0, keepdims=True))
        s = jnp.where(hit, -jnp.inf, s)
    return jnp.concatenate(vals, axis=0), jnp.concatenate(pays, axis=0)


def _pair_candidates(s1, i1, s2, i2):
    kk = PEER_TOPK
    sub = lax.broadcasted_iota(I32, (SUBLANES, s1.shape[1]), 0)
    cand = [s1[0:1, :] + s2]
    cid = [i1[0:1, :] * PEER_NKEYS + i2]
    for a in range(1, SUBLANES):
        ok = sub < kk // (a + 1)
        cand.append(jnp.where(ok, s1[a:a + 1, :] + s2[0:SUBLANES, :], -jnp.inf))
        cid.append(i1[a:a + 1, :] * PEER_NKEYS + i2[0:SUBLANES, :])
    cand.append(s1[SUBLANES:kk, :] + s2[0:1, :])
    cid.append(i1[SUBLANES:kk, :] * PEER_NKEYS + i2[0:1, :])
    return jnp.concatenate(cand, axis=0), jnp.concatenate(cid, axis=0)


def _compress_rows(sel, skip, arrays):
    n = sel.shape[0]
    cc = jnp.where(sel, skip, 0)
    vals = [jnp.where(sel, a, jnp.zeros_like(a)) for a in arrays]
    for k in range(n.bit_length() - 1):
        mv = ((cc >> k) & 1) == 1

        def step(a):
            zero = jnp.zeros_like(a)
            return jnp.where(mv, zero, a) + pltpu.roll(jnp.where(mv, a, zero), n - (1 << k), 0)

        vals = [step(a) for a in vals]
        cc = step(cc)
    return vals


def _peer_topk_kernel(h_ref, wq_ref, keys_ref, off_ref, gate_ref, ngrp_ref, *, half_rows):
    kk = PEER_TOPK
    q = _dot(h_ref[...].astype(BF16), wq_ref[...])
    eids, gates = [], []
    for hd in range(PEER_HEADS):
        tops = []
        for part in range(2):
            c = 2 * hd + part
            qs = q[:, c * LANES:(c + 1) * LANES].astype(BF16)
            tops.append(_top_rows(_dot_nt(keys_ref[c], qs), kk))
        (s1, i1), (s2, i2) = tops
        cand, cid = _pair_candidates(s1, i1, s2, i2)
        best, eid = _top_rows(cand, kk, cid)
        e = jnp.exp(best - best[0:1, :])
        eids.append(eid)
        gates.append(e / jnp.sum(e, axis=0, keepdims=True))
    eid = jnp.concatenate(eids, axis=0).astype(I32)
    gate = jnp.concatenate(gates, axis=0)
    npick, tm = eid.shape
    off = (eid & (half_rows - 1)) * SUBLANES
    upper = eid >= half_rows
    row = lax.broadcasted_iota(I32, (npick, tm), 0)
    before = (lax.broadcasted_iota(I32, (npick, npick), 1) < lax.broadcasted_iota(I32, (npick, npick), 0))
    upper_before = _dot(before.astype(BF16), upper.astype(BF16)).astype(I32)
    for hf, (sel, skip) in enumerate(((~upper, upper_before), (upper, row - upper_before))):
        off_c, gate_c = _compress_rows(sel, skip, [off, gate])
        off_ref[hf] = off_c.T
        gate_ref[hf] = gate_c.T
        count = jnp.sum(sel.astype(F32), axis=0, keepdims=True)
        ngrp = ((count + (SUBLANES - 1)) * (1.0 / SUBLANES)).astype(I32)
        ngrp_ref[hf] = jnp.broadcast_to(ngrp, (npick, tm)).T


def _flat_tile(nt_all, nt, tile_off):
    return lambda i: (i // nt) * nt_all + (i % nt) + tile_off


def _peer_topk(h2, wq, keys, half_rows, nb, tile_off):
    n, d = h2.shape
    tm = ROW_TILE
    nt_all = n // nb // tm
    nt = nt_all - tile_off
    ft = _flat_tile(nt_all, nt, tile_off)
    npick = PEER_HEADS * PEER_TOPK
    out = pl.BlockSpec((2, tm, npick), lambda i: (0, ft(i), 0))
    return pl.pallas_call(
        functools.partial(_peer_topk_kernel, half_rows=half_rows),
        grid=(nb * nt,),
        in_specs=[pl.BlockSpec((tm, d), lambda i: (ft(i), 0)),
                  pl.BlockSpec(wq.shape, lambda i: (0, 0)),
                  pl.BlockSpec(keys.shape, lambda i: (0, 0, 0))],
        out_specs=[out, out, out],
        out_shape=[jax.ShapeDtypeStruct((2, n, npick), I32), jax.ShapeDtypeStruct((2, n, npick), F32),
                   jax.ShapeDtypeStruct((2, n, npick), I32)],
        compiler_params=_cparams(("parallel",)),
        name="peer_topk",
    )(h2, wq, keys)


def _load_table_half(tab_hbm, tab_ref, sem, half_rows):
    hf = pl.program_id(0)

    @pl.when(pl.program_id(1) == 0)
    def _():
        cp = pltpu.make_async_copy(tab_hbm.at[pl.ds(hf * half_rows, half_rows)], tab_ref, sem)
        cp.start()
        cp.wait()


def _table_row(tab_ref, off):
    return tab_ref[pl.ds(pl.multiple_of(off, SUBLANES), SUBLANES), :]


def _sublane_sum_masks():
    sub = lax.broadcasted_iota(I32, (SUBLANES, LANES), 0)
    return ((sub + 2) % 8 < 4,
            (sub + 1) % 8 < 4,
            sub < 4,
            (sub + 7) % 8 < 4,
            sub % 4 < 2,
            (sub + 3) % 4 < 2,
            sub % 2 == 1)


def _sublane_sums(prods, masks):
    def merge(a, b, m, shift):
        return jnp.where(m, a, b) + pltpu.roll(jnp.where(m, b, a), shift, 0)

    s0 = merge(prods[0], prods[4], masks[0], 4)
    s1 = merge(prods[1], prods[5], masks[1], 4)
    s2 = merge(prods[2], prods[6], masks[2], 4)
    s3 = merge(prods[3], prods[7], masks[3], 4)
    u0 = merge(s0, s2, masks[4], 2)
    u1 = merge(s1, s3, masks[5], 2)
    return merge(u0, u1, masks[6], 1)


def _peer_dot_kernel(iota_ref, off_ref, ngrp_ref, x_ref, tab_hbm, d_ref, tab_ref, q_ref, sem):
    _load_table_half(tab_hbm, tab_ref, sem, tab_ref.shape[0])
    npick, tb = d_ref.shape
    lane = lax.broadcasted_iota(I32, (npick, tb), 1)
    masks = _sublane_sum_masks()
    ks = [iota_ref[k] for k in range(SUBLANES)]

    @pl.when(pl.program_id(1) == 0)
    def _():
        def clear(t, carry):
            q_ref[t] = jnp.zeros((npick, LANES), F32)
            return carry
        lax.fori_loop(0, tb, clear, 0)

    def gather(t, carry):
        x = x_ref[t]
        base = pl.multiple_of(t * npick, npick)

        def group(start):
            offs = off_ref.at[pl.ds(base + start, SUBLANES)]
            prods = [_table_row(tab_ref, offs[ks[(k + 1) % SUBLANES]]) * x for k in range(SUBLANES)]
            q_ref[t, pl.ds(start, SUBLANES), :] = _sublane_sums(prods, masks)

        for g in range(PEER_STATIC_GROUPS):
            group(g * SUBLANES)

        def extra(g, c):
            group(pl.multiple_of(g * SUBLANES, SUBLANES))
            return c

        lax.fori_loop(PEER_STATIC_GROUPS, ngrp_ref[base], extra, 0)
        return carry

    def lane_reduce(t, acc):
        col = jnp.sum(q_ref[t], axis=1, keepdims=True)
        return jnp.where(lane == t, col, acc)

    lax.fori_loop(0, tb, gather, 0)
    d_ref[...] = lax.fori_loop(0, tb, lane_reduce, jnp.zeros((npick, tb), F32), unroll=8)


def _peer_sum_kernel(iota_ref, off_ref, ngrp_ref, w_ref, tab_hbm, f_ref, tab_ref, wl_ref, sem):
    _load_table_half(tab_hbm, tab_ref, sem, tab_ref.shape[0])
    tb, npick = w_ref.shape
    nacc = 4
    ks = [iota_ref[k] for k in range(SUBLANES)]
    wt = w_ref[...].T
    lane = lax.broadcasted_iota(I32, (npick, tb), 1)

    def spread(t, carry):
        col = jnp.sum(jnp.where(lane == t, wt, 0.0), axis=1, keepdims=True)
        wl_ref[t] = jnp.broadcast_to(col, (npick, LANES))
        return carry

    def token(t, carry):
        base = pl.multiple_of(t * npick, npick)

        def group(start, accs):
            accs = list(accs)
            offs = off_ref.at[pl.ds(base + start, SUBLANES)]
            for k in range(SUBLANES):
                wv = jnp.broadcast_to(wl_ref[t, pl.ds(start + k, 1), :], (SUBLANES, LANES))
                accs[k % nacc] = accs[k % nacc] + wv * _table_row(tab_ref, offs[ks[k]])
            return tuple(accs)

        accs = tuple(jnp.zeros((SUBLANES, LANES), F32) for _ in range(nacc))
        for g in range(PEER_STATIC_GROUPS):
            accs = group(g * SUBLANES, accs)
        accs = lax.fori_loop(PEER_STATIC_GROUPS, ngrp_ref[base],
                             lambda g, a: group(pl.multiple_of(g * SUBLANES, SUBLANES), a), accs)
        f_ref[t] = (accs[0] + accs[1]) + (accs[2] + accs[3])
        return carry

    lax.fori_loop(0, tb, spread, 0, unroll=8)
    lax.fori_loop(0, tb, token, 0)


def _peer_tiles(n, nb, tile_off_rows):
    tb = PEER_TOK
    nt_all = n // nb // tb
    off = tile_off_rows // tb
    nt = nt_all - off
    return tb, nb * nt, _flat_tile(nt_all, nt, off)


def _smem_token_block(tb, npick, blocks_per_half, ft):
    return pl.BlockSpec((tb * npick,), lambda hf, i: (hf * blocks_per_half + ft(i),), memory_space=pltpu.SMEM)


def _peer_dot(off_flat, ngrp_flat, h2v, tab2, npick, nb, tile_off_rows):
    n = h2v.shape[0]
    tb, steps, ft = _peer_tiles(n, nb, tile_off_rows)
    half = tab2.shape[0] // 2
    smem_blk = _smem_token_block(tb, npick, n // tb, ft)
    return pl.pallas_call(
        _peer_dot_kernel,
        grid=(2, steps),
        in_specs=[pl.BlockSpec(memory_space=pltpu.SMEM), smem_blk, smem_blk,
                  pl.BlockSpec((tb, SUBLANES, LANES), lambda hf, i: (ft(i), 0, 0)),
                  pl.BlockSpec(memory_space=pl.ANY)],
        out_specs=pl.BlockSpec((None, npick, tb), lambda hf, i: (hf, 0, ft(i))),
        out_shape=jax.ShapeDtypeStruct((2, npick, n), F32),
        scratch_shapes=[pltpu.VMEM((half, LANES), F32),
                        pltpu.VMEM((tb, npick, LANES), F32),
                        pltpu.SemaphoreType.DMA],
        compiler_params=_cparams(("arbitrary", "arbitrary"), VMEM_LIMIT_TABLE),
        name="peer_dot",
    )(jnp.arange(SUBLANES, dtype=I32), off_flat, ngrp_flat, h2v, tab2)


def _peer_weight_kernel(gate_ref, d_ref, w_ref):
    for hf in range(2):
        gate = gate_ref[hf]
        d = d_ref[hf].T
        act = 0.5 * d * (1.0 + lax.erf(d * (2.0 ** -0.5)))
        w_ref[hf] = jnp.where(gate > 0.0, gate * act, 0.0)


def _peer_weight(gate2, dpart, nb, tile_off_rows):
    _, n, npick = gate2.shape
    tb, steps, ft = _peer_tiles(n, nb, tile_off_rows)
    blk = pl.BlockSpec((2, tb, npick), lambda i: (0, ft(i), 0))
    return pl.pallas_call(
        _peer_weight_kernel,
        grid=(steps,),
        in_specs=[blk, pl.BlockSpec((2, npick, tb), lambda i: (0, 0, ft(i)))],
        out_specs=blk,
        out_shape=jax.ShapeDtypeStruct((2, n, npick), F32),
        compiler_params=_cparams(("parallel",)),
        name="peer_weight",
    )(gate2, dpart)


def _peer_sum(off_flat, ngrp_flat, w2, tab2, nb, tile_off_rows):
    _, n, npick = w2.shape
    tb, steps, ft = _peer_tiles(n, nb, tile_off_rows)
    half = tab2.shape[0] // 2
    smem_blk = _smem_token_block(tb, npick, n // tb, ft)
    return pl.pallas_call(
        _peer_sum_kernel,
        grid=(2, steps),
        in_specs=[pl.BlockSpec(memory_space=pltpu.SMEM), smem_blk, smem_blk,
                  pl.BlockSpec((None, tb, npick), lambda hf, i: (hf, ft(i), 0)),
                  pl.BlockSpec(memory_space=pl.ANY)],
        out_specs=pl.BlockSpec((None, tb, SUBLANES, LANES), lambda hf, i: (hf, ft(i), 0, 0)),
        out_shape=jax.ShapeDtypeStruct((2, n, SUBLANES, LANES), F32),
        scratch_shapes=[pltpu.VMEM((half, LANES), F32),
                        pltpu.VMEM((tb, npick, LANES), F32),
                        pltpu.SemaphoreType.DMA],
        compiler_params=_cparams(("arbitrary", "arbitrary"), VMEM_LIMIT_TABLE),
        name="peer_sum",
    )(jnp.arange(SUBLANES, dtype=I32), off_flat, ngrp_flat, w2, tab2)


def _residual_kernel(x_ref, f_ref, mod_ref, o_ref):
    o_ref[...] = x_ref[...] + mod_ref[5:6, :] * (f_ref[0] + f_ref[1])


def _final_kernel(x_ref, f_ref, mod_ref, g_ref, o_ref):
    o_ref[...] = _rms(x_ref[...] + mod_ref[5:6, :] * (f_ref[0] + f_ref[1])) * g_ref[...]


def _residual(x1, f2, mods):
    nb, l, d = x1.shape
    tm = ROW_TILE
    return pl.pallas_call(
        _residual_kernel,
        grid=(nb, l // tm),
        in_specs=[pl.BlockSpec((None, tm, d), lambda b, t: (b, t, 0)),
                  pl.BlockSpec((2, None, tm, d), lambda b, t: (0, b, t, 0)),
                  pl.BlockSpec((None, 6, d), _mod_index(nb))],
        out_specs=pl.BlockSpec((None, tm, d), lambda b, t: (b, t, 0)),
        out_shape=jax.ShapeDtypeStruct((nb, l, d), F32),
        compiler_params=_cparams(("parallel", "parallel")),
        name="peer_residual",
    )(x1, f2, mods)


def _final(x1, f2, mods, g, tile_off):
    nb, l, d = x1.shape
    tm = ROW_TILE
    nt = l // tm - tile_off
    return pl.pallas_call(
        _final_kernel,
        grid=(nb, nt),
        in_specs=[pl.BlockSpec((None, tm, d), lambda b, t: (b, t + tile_off, 0)),
                  pl.BlockSpec((2, None, tm, d), lambda b, t: (0, b, t + tile_off, 0)),
                  pl.BlockSpec((None, 6, d), lambda b, t: (b, 0, 0)),
                  pl.BlockSpec((1, d), lambda b, t: (0, 0))],
        out_specs=pl.BlockSpec((None, tm, d), lambda b, t: (b, t, 0)),
        out_shape=jax.ShapeDtypeStruct((nb, nt * tm, d), F32),
        compiler_params=_cparams(("parallel", "parallel")),
        name="final_norm",
    )(x1, f2, mods, g)


def _rope_tables(seq, ctx_len):
    t = jnp.arange(seq)
    row_id = (t // GRID_W).astype(F32)
    col_id = (t % GRID_W).astype(F32)
    axis_dim = HEAD_DIM // 2
    inv = ROPE_THETA ** (-jnp.arange(0, axis_dim, 2, dtype=F32) / axis_dim)
    ang_r = row_id[:, None] * inv[None, :]
    ang_c = col_id[:, None] * inv[None, :]
    cos = jnp.concatenate([jnp.cos(ang_r)] * 2 + [jnp.cos(ang_c)] * 2, axis=-1)
    sin = jnp.concatenate([-jnp.sin(ang_r), jnp.sin(ang_r), -jnp.sin(ang_c), jnp.sin(ang_c)], axis=-1)
    cos = jnp.concatenate([jnp.ones((ctx_len, HEAD_DIM), F32), cos], axis=0)
    sin = jnp.concatenate([jnp.zeros((ctx_len, HEAD_DIM), F32), sin], axis=0)
    return jnp.tile(cos, (1, 2)), jnp.tile(sin, (1, 2))


def _peer(h2, x1, mods, wq, keys, u_tab, v_tab, nb, tile_off_rows, final_g):
    b, l, d = x1.shape
    n = b * l
    nexp = u_tab.shape[0]
    npick = PEER_HEADS * PEER_TOPK
    off2, gate2, ngrp2 = _peer_topk(h2.reshape(n, d), wq.astype(BF16),
                                    keys.reshape(PEER_HEADS * 2, PEER_NKEYS, -1).astype(BF16),
                                    nexp // 2, nb, tile_off_rows // ROW_TILE)
    off_flat = off2.reshape(2 * n * npick)
    ngrp_flat = ngrp2.reshape(2 * n * npick)
    h2v = h2.reshape(n, SUBLANES, LANES)
    dpart = _peer_dot(off_flat, ngrp_flat, h2v, u_tab.reshape(nexp * SUBLANES, LANES), npick, nb, tile_off_rows)
    w2 = _peer_weight(gate2, dpart, nb, tile_off_rows)
    f2 = _peer_sum(off_flat, ngrp_flat, w2, v_tab.reshape(nexp * SUBLANES, LANES), nb, tile_off_rows)
    f2 = f2.reshape(2, b, l, d)
    if final_g is None:
        return _residual(x1, f2, mods)
    return _final(x1, f2, mods, final_g.reshape(1, d), tile_off_rows // ROW_TILE)


def kernel(x, c, ctx, c_ctx, ada_w, ada_b, norm1_g, norm2_g, ev_w_in, ev_w_out, gqa_q_norm_g,
           gqa_k_norm_g, ret_log_rate, od_w_in, od_w_out, diff_lambda, diff_subln_g, peer_w_q,
           peer_keys, peer_u, peer_v, final_g):
    nb, seq, d = x.shape
    ctx_len = ctx.shape[1]
    depth = ada_w.shape[0]
    assert ctx_len == ROW_TILE and seq % ROW_TILE == 0 and d == SUBLANES * LANES
    xs = jnp.concatenate([ctx, x], axis=1)
    cos, sin = _rope_tables(seq, ctx_len)
    cc = jnp.concatenate([c, c_ctx[None, :]], axis=0)
    for layer in range(depth):
        last = layer == depth - 1
        mods = _ada(cc, ada_w[layer], ada_b[layer]).reshape(nb + 1, 6, d)
        g1 = norm1_g[layer].reshape(1, d)
        g2 = norm2_g[layer].reshape(1, d)
        lat_off = 1 if last else 0
        if layer % 2 == 0:
            e = layer // 2
            qa, ka, va, qb, kb, vb, gb = _even_inproj(
                xs, g1, mods, ev_w_in[e].astype(BF16), cos, sin,
                jnp.tile(gqa_q_norm_g[e], 2).reshape(1, LANES), jnp.tile(gqa_k_norm_g[e], 2).reshape(1, LANES))
            ya = _gqa(qa, ka, va, ctx_len, 0)
            o_f, o_r = _retention(ret_log_rate[e], qb, kb, vb, ctx_len)
            x1, h2 = _even_outproj(xs, ya, o_f, o_r, gb, ev_w_out[e].astype(BF16), mods, g2)
        else:
            o = layer // 2
            lam_init = 0.8 - 0.6 * math.exp(-0.3 * layer)
            q, k, v = _odd_inproj(xs, g1, mods, od_w_in[o].astype(BF16), cos, sin)
            mix = _diff_attention(q, k, v, diff_lambda[o], diff_subln_g[o].reshape(1, LANES),
                                  ctx_len, lat_off, lam_init)
            x1, h2 = _odd_outproj(xs, mix, od_w_out[o].astype(BF16), mods, g2, lat_off)
        xs = _peer(h2, x1, mods, peer_w_q[layer], peer_keys[layer], peer_u[layer], peer_v[layer],
                   nb, lat_off * ROW_TILE, final_g if last else None)
    return xs
```

```python
import functools
import math

import jax
import jax.numpy as jnp
from jax import lax
from jax.experimental import pallas as pl
from jax.experimental.pallas import tpu as pltpu

F32 = jnp.float32
BF16 = jnp.bfloat16
I32 = jnp.int32

LANES = 128
SUBLANES = 8
HEAD_DIM = 64
GRID_W = 64
ROPE_THETA = 10000.0
NORM_EPS = 1e-6
RET_CHUNK = 128
PEER_NKEYS = 128
PEER_TOPK = 16
PEER_HEADS = 8
ROW_TILE = 256
PEER_TOK = 128
PEER_STATIC_GROUPS = 9
PEER_XLU_GROUPS = 10
PEER_XLU_UNROLL = 32
KV_CHUNK = 256
VMEM_LIMIT = 48 * 1024 * 1024
VMEM_LIMIT_TABLE = 56 * 1024 * 1024


def _cparams(sem, limit=VMEM_LIMIT):
    return pltpu.CompilerParams(dimension_semantics=sem, vmem_limit_bytes=limit)


def _rms(x):
    return x * lax.rsqrt(jnp.mean(x * x, axis=-1, keepdims=True) + NORM_EPS)


def _dot(a, b):
    return jnp.dot(a, b, preferred_element_type=F32)


def _dot_nt(a, b):
    return lax.dot_general(a, b, (((1,), (1,)), ((), ())), preferred_element_type=F32)


def _sigmoid(x):
    return 1.0 / (1.0 + jnp.exp(-x))


def _ada_kernel(c_ref, w_ref, b_ref, o_ref):
    c = c_ref[...]
    a = (c * _sigmoid(c)).astype(BF16)
    o_ref[...] = _dot(a, w_ref[...].astype(BF16)) + b_ref[...]


def _ada(cc, w, b):
    m, d = cc.shape
    n = w.shape[1]
    tn = 1024
    return pl.pallas_call(
        _ada_kernel,
        grid=(n // tn,),
        in_specs=[pl.BlockSpec((m, d), lambda j: (0, 0)),
                  pl.BlockSpec((d, tn), lambda j: (0, j)),
                  pl.BlockSpec((1, tn), lambda j: (0, j))],
        out_specs=pl.BlockSpec((m, tn), lambda j: (0, j)),
        out_shape=jax.ShapeDtypeStruct((m, n), F32),
        compiler_params=_cparams(("arbitrary",)),
        name="ada_mod",
    )(cc, w, b.reshape(1, n))


def _rope_cols(x, cos, sin_signed):
    lane = lax.broadcasted_iota(I32, x.shape, 1)
    partner = jnp.where((lane % 32) < 16, pltpu.roll(x, LANES - 16, 1), pltpu.roll(x, 16, 1))
    return x * cos + partner * sin_signed


def _head_mean_sq(x):
    r = lax.broadcasted_iota(I32, (LANES, LANES), 0) // HEAD_DIM
    c = lax.broadcasted_iota(I32, (LANES, LANES), 1) // HEAD_DIM
    ones_bd = jnp.where(r == c, 1.0, 0.0).astype(BF16)
    sq = x * x
    hi = sq.astype(BF16)
    lo = (sq - hi.astype(F32)).astype(BF16)
    return (_dot(hi, ones_bd) + _dot(lo, ones_bd)) * (1.0 / HEAD_DIM)


def _prologue(x_ref, g_ref, mod_ref, shift_row, scale_row):
    h = _rms(x_ref[...]) * g_ref[...]
    return h * (1.0 + mod_ref[scale_row:scale_row + 1, :]) + mod_ref[shift_row:shift_row + 1, :]


def _even_inproj_kernel(x_ref, g_ref, mod_ref, w_ref, cos_ref, sin_ref, qg_ref, kg_ref,
                        qa_ref, ka_ref, va_ref, qb_ref, kb_ref, vb_ref, gb_ref):
    h = _prologue(x_ref, g_ref, mod_ref, 0, 1).astype(BF16)
    y = _dot(h, w_ref[...])
    cos = cos_ref[...]
    sin = sin_ref[...]
    scale = HEAD_DIM ** -0.5

    def col(c):
        return y[:, c * LANES:(c + 1) * LANES]

    for c in range(4):
        x = col(c)
        x = x * lax.rsqrt(_head_mean_sq(x) + NORM_EPS) * qg_ref[...]
        qa_ref[:, c * LANES:(c + 1) * LANES] = (_rope_cols(x, cos, sin) * scale).astype(BF16)
    x = col(4)
    x = x * lax.rsqrt(_head_mean_sq(x) + NORM_EPS) * kg_ref[...]
    ka_ref[...] = _rope_cols(x, cos, sin).astype(BF16)
    va_ref[...] = col(5).astype(BF16)
    for c in range(2):
        qb_ref[:, c * LANES:(c + 1) * LANES] = _rope_cols(col(6 + c), cos, sin).astype(BF16)
        kb_ref[:, c * LANES:(c + 1) * LANES] = _rope_cols(col(8 + c) * scale, cos, sin).astype(BF16)
    vb_ref[...] = y[:, 10 * LANES:14 * LANES].astype(BF16)
    gb_ref[...] = y[:, 14 * LANES:18 * LANES]


def _mod_index(nb):
    return lambda b, t: (jnp.where(t == 0, nb, b), 0, 0)


def _even_inproj(x, g, mods, w, cos, sin, qg, kg):
    nb, l, d = x.shape
    tm = ROW_TILE
    n_in = w.shape[1]
    row = lambda width: pl.BlockSpec((None, tm, width), lambda b, t: (b, t, 0))
    widths = (512, 128, 128, 256, 256, 512, 512)
    dts = (BF16, BF16, BF16, BF16, BF16, BF16, F32)
    return pl.pallas_call(
        _even_inproj_kernel,
        grid=(nb, l // tm),
        in_specs=[row(d),
                  pl.BlockSpec((1, d), lambda b, t: (0, 0)),
                  pl.BlockSpec((None, 6, d), _mod_index(nb)),
                  pl.BlockSpec((d, n_in), lambda b, t: (0, 0)),
                  pl.BlockSpec((tm, LANES), lambda b, t: (t, 0)),
                  pl.BlockSpec((tm, LANES), lambda b, t: (t, 0)),
                  pl.BlockSpec((1, LANES), lambda b, t: (0, 0)),
                  pl.BlockSpec((1, LANES), lambda b, t: (0, 0))],
        out_specs=[row(wd) for wd in widths],
        out_shape=[jax.ShapeDtypeStruct((nb, l, wd), dt) for wd, dt in zip(widths, dts)],
        compiler_params=_cparams(("parallel", "parallel")),
        name="even_inproj",
    )(x, g, mods, w, cos, sin, qg, kg)


def _odd_inproj_kernel(x_ref, g_ref, mod_ref, w_ref, cos_ref, sin_ref, q_ref, k_ref, v_ref):
    h = _prologue(x_ref, g_ref, mod_ref, 0, 1).astype(BF16)
    y = _dot(h, w_ref[...])
    cos = cos_ref[...]
    sin = sin_ref[...]
    scale = HEAD_DIM ** -0.5
    for c in range(8):
        sl = slice(c * LANES, (c + 1) * LANES)
        q_ref[:, sl] = (_rope_cols(y[:, sl], cos, sin) * scale).astype(BF16)
        k_ref[:, sl] = _rope_cols(y[:, 8 * LANES + c * LANES:8 * LANES + (c + 1) * LANES], cos, sin).astype(BF16)
    v_ref[...] = y[:, 16 * LANES:24 * LANES].astype(BF16)


def _odd_inproj(x, g, mods, w, cos, sin):
    nb, l, d = x.shape
    tm = ROW_TILE
    n_in = w.shape[1]
    row = lambda width: pl.BlockSpec((None, tm, width), lambda b, t: (b, t, 0))
    return pl.pallas_call(
        _odd_inproj_kernel,
        grid=(nb, l // tm),
        in_specs=[row(d),
                  pl.BlockSpec((1, d), lambda b, t: (0, 0)),
                  pl.BlockSpec((None, 6, d), _mod_index(nb)),
                  pl.BlockSpec((d, n_in), lambda b, t: (0, 0)),
                  pl.BlockSpec((tm, LANES), lambda b, t: (t, 0)),
                  pl.BlockSpec((tm, LANES), lambda b, t: (t, 0))],
        out_specs=[row(d), row(d), row(d)],
        out_shape=[jax.ShapeDtypeStruct((nb, l, d), BF16)] * 3,
        compiler_params=_cparams(("parallel", "parallel")),
        name="odd_inproj",
    )(x, g, mods, w, cos, sin)


def _scores(q, k_ref, nchunks):
    kc = KV_CHUNK
    scores, run = [], None
    for c in range(nchunks):
        s = _dot_nt(q, k_ref[c * kc:(c + 1) * kc, :])
        scores.append(s)
        for j in range(kc // LANES):
            col = s[:, j * LANES:(j + 1) * LANES]
            run = col if run is None else jnp.maximum(run, col)
    return scores, jnp.max(run, axis=-1, keepdims=True)


def _softmax_pv(scores_max, v_ref, next_q=None, k_ref=None):
    scores, m = scores_max
    kc = KV_CHUNK
    acc = jnp.zeros((scores[0].shape[0], v_ref.shape[1]), F32)
    den = jnp.zeros((scores[0].shape[0], LANES), F32)
    nxt, run = [], None
    for c, s in enumerate(scores):
        if next_q is not None:
            sn = _dot_nt(next_q, k_ref[c * kc:(c + 1) * kc, :])
            nxt.append(sn)
            for j in range(kc // LANES):
                col = sn[:, j * LANES:(j + 1) * LANES]
                run = col if run is None else jnp.maximum(run, col)
        e = jnp.exp(s - m)
        for j in range(kc // LANES):
            den = den + e[:, j * LANES:(j + 1) * LANES]
        acc = acc + _dot(e.astype(BF16), v_ref[c * kc:(c + 1) * kc, :])
    den = jnp.sum(den, axis=-1, keepdims=True)
    if next_q is None:
        return acc, den
    return acc, den, (nxt, jnp.max(run, axis=-1, keepdims=True))


def _key_chunk_branches(qi, ctx_len, total, body):
    @pl.when(qi == 0)
    def _():
        body(ctx_len // KV_CHUNK)

    @pl.when(qi != 0)
    def _():
        body(total // KV_CHUNK)


def _gqa_kernel(q_ref, k_ref, v_ref, o_ref, *, ctx_len, tile_off):
    g = pl.program_id(1)
    qi = pl.program_id(2) + tile_off
    tq = q_ref.shape[0]
    half = lax.broadcasted_iota(I32, (tq, LANES), 1) // HEAD_DIM

    def head_q(i):
        qh = q_ref[:, (i // 2) * LANES:(i // 2 + 1) * LANES]
        qsel = jnp.where(half == (i % 2), qh, jnp.zeros_like(qh)).astype(F32)
        return jnp.where(g == (i % 2), qsel, pltpu.roll(qsel, HEAD_DIM, 1)).astype(BF16)

    def body(nchunks):
        cols = [jnp.zeros((tq, LANES), F32), jnp.zeros((tq, LANES), F32)]
        cur = _scores(head_q(0), k_ref, nchunks)
        for i in range(4):
            if i + 1 < 4:
                acc, den, cur = _softmax_pv(cur, v_ref, head_q(i + 1), k_ref)
            else:
                acc, den = _softmax_pv(cur, v_ref)
            o = acc / den
            osel = jnp.where(half == g, o, 0.0)
            cols[i // 2] = cols[i // 2] + jnp.where(g == (i % 2), osel, pltpu.roll(osel, HEAD_DIM, 1))
        o_ref[:, 0:LANES] = cols[0].astype(BF16)
        o_ref[:, LANES:2 * LANES] = cols[1].astype(BF16)

    if tile_off == 0:
        _key_chunk_branches(qi, ctx_len, k_ref.shape[0], body)
    else:
        body(k_ref.shape[0] // KV_CHUNK)


def _gqa(q, k, v, ctx_len, tile_off):
    nb, l, _ = q.shape
    tq = ROW_TILE
    nt = l // tq - tile_off
    return pl.pallas_call(
        functools.partial(_gqa_kernel, ctx_len=ctx_len, tile_off=tile_off),
        grid=(nb, 2, nt),
        in_specs=[pl.BlockSpec((None, tq, 2 * LANES), lambda b, g, t: (b, t + tile_off, g)),
                  pl.BlockSpec((None, l, LANES), lambda b, g, t: (b, 0, 0)),
                  pl.BlockSpec((None, l, LANES), lambda b, g, t: (b, 0, 0))],
        out_specs=pl.BlockSpec((None, tq, 2 * LANES), lambda b, g, t: (b, t + tile_off, g)),
        out_shape=jax.ShapeDtypeStruct((nb, l, 4 * LANES), BF16),
        compiler_params=_cparams(("parallel", "parallel", "parallel")),
        name="gqa_attention",
    )(q, k, v)


def _diff_kernel(q_ref, k_ref, v_ref, lam_ref, g_ref, o_ref, *, ctx_len, tile_off, lam_init):
    qi = pl.program_id(2) + tile_off
    tq = q_ref.shape[0]
    lf = lam_ref[...]
    lam = (jnp.exp(jnp.sum(lf[0:1, :] * lf[1:2, :], axis=-1, keepdims=True))
           - jnp.exp(jnp.sum(lf[2:3, :] * lf[3:4, :], axis=-1, keepdims=True)) + lam_init)
    half = lax.broadcasted_iota(I32, (tq, LANES), 1) // HEAD_DIM

    def body(nchunks):
        q = q_ref[...]
        zero = jnp.zeros_like(q)
        s1 = _scores(jnp.where(half == 0, q, zero), k_ref, nchunks)
        a1, d1, s2 = _softmax_pv(s1, v_ref, jnp.where(half == 1, q, zero), k_ref)
        a2, d2 = _softmax_pv(s2, v_ref)
        o = a1 / d1 - lam * (a2 / d2)
        o_ref[...] = (_rms(o) * g_ref[...] * (1.0 - lam_init)).astype(BF16)

    if tile_off == 0:
        _key_chunk_branches(qi, ctx_len, k_ref.shape[0], body)
    else:
        body(k_ref.shape[0] // KV_CHUNK)


def _diff_attention(q, k, v, lam_p, subln_g, ctx_len, tile_off, lam_init):
    nb, l, d = q.shape
    nh = d // LANES
    tq = ROW_TILE
    nt = l // tq - tile_off
    return pl.pallas_call(
        functools.partial(_diff_kernel, ctx_len=ctx_len, tile_off=tile_off, lam_init=lam_init),
        grid=(nb, nh, nt),
        in_specs=[pl.BlockSpec((None, tq, LANES), lambda b, h, t: (b, t + tile_off, h)),
                  pl.BlockSpec((None, l, LANES), lambda b, h, t: (b, 0, h)),
                  pl.BlockSpec((None, l, LANES), lambda b, h, t: (b, 0, h)),
                  pl.BlockSpec((4, HEAD_DIM), lambda b, h, t: (0, 0)),
                  pl.BlockSpec((1, LANES), lambda b, h, t: (0, 0))],
        out_specs=pl.BlockSpec((None, tq, LANES), lambda b, h, t: (b, t + tile_off, h)),
        out_shape=jax.ShapeDtypeStruct((nb, l, d), BF16),
        compiler_params=_cparams(("parallel", "parallel", "parallel")),
        name="diff_attention",
    )(q, k, v, lam_p, subln_g)


def _retention_kernel(rate_ref, qf_ref, kf_ref, vf_ref, qr_ref, kr_ref, vr_ref,
                      of_ref, or_ref, sf_ref, sr_ref):
    h = pl.program_id(1)
    n = pl.program_id(2)
    c = RET_CHUNK

    @pl.when(n == 0)
    def _():
        sf_ref[...] = jnp.zeros_like(sf_ref)
        sr_ref[...] = jnp.zeros_like(sr_ref)

    head_ok = (lax.broadcasted_iota(I32, (c, LANES), 1) // HEAD_DIM) == (h % 2)
    ri = lax.broadcasted_iota(I32, (c, c), 0)
    ci = lax.broadcasted_iota(I32, (c, c), 1)
    pos = lax.broadcasted_iota(I32, (c, 1), 0).astype(F32)

    def one(direction, q_ref, k_ref, v_ref, o_ref, s_ref):
        log_g = -jnp.exp(jnp.full((1, 1), rate_ref[direction, h], F32))
        rel = ((ri - ci) if direction == 0 else (ci - ri)).astype(F32)
        dmask = jnp.where(rel >= 0, jnp.exp(log_g * jnp.maximum(rel, 0.0)), 0.0)
        q = jnp.where(head_ok, q_ref[...], jnp.zeros((c, LANES), BF16))
        k = jnp.where(head_ok, k_ref[...], jnp.zeros((c, LANES), BF16))
        v = v_ref[...]
        sc = _dot_nt(q, k) * dmask
        intra = _dot(sc.astype(BF16), v)
        q_pow = (pos + 1.0) if direction == 0 else (c - pos)
        k_pow = (c - 1.0 - pos) if direction == 0 else pos
        state = s_ref[...]
        cross = _dot((q.astype(F32) * jnp.exp(log_g * q_pow)).astype(BF16), state.astype(BF16))
        o_ref[...] = intra + cross
        kd = (k.astype(F32) * jnp.exp(log_g * k_pow)).T.astype(BF16)
        s_ref[...] = jnp.exp(log_g * c) * state + _dot(kd, v)

    one(0, qf_ref, kf_ref, vf_ref, of_ref, sf_ref)
    one(1, qr_ref, kr_ref, vr_ref, or_ref, sr_ref)


def _retention(rates, q, k, v, ctx_len):
    nb, l, dv_all = v.shape
    c = RET_CHUNK
    nh = dv_all // LANES
    nctx = ctx_len // c
    ntot = l // c

    def rev(n):
        return jnp.where(n < nctx, nctx - 1 - n, ntot - 1 - (n - nctx))

    qk_f = pl.BlockSpec((None, c, LANES), lambda b, h, n: (b, n, h // 2))
    qk_r = pl.BlockSpec((None, c, LANES), lambda b, h, n: (b, rev(n), h // 2))
    v_f = pl.BlockSpec((None, c, LANES), lambda b, h, n: (b, n, h))
    v_r = pl.BlockSpec((None, c, LANES), lambda b, h, n: (b, rev(n), h))
    return pl.pallas_call(
        _retention_kernel,
        grid=(nb, nh, ntot),
        in_specs=[pl.BlockSpec(memory_space=pltpu.SMEM), qk_f, qk_f, v_f, qk_r, qk_r, v_r],
        out_specs=[v_f, v_r],
        out_shape=[jax.ShapeDtypeStruct((nb, l, dv_all), F32)] * 2,
        scratch_shapes=[pltpu.VMEM((LANES, LANES), F32), pltpu.VMEM((LANES, LANES), F32)],
        compiler_params=_cparams(("parallel", "parallel", "arbitrary")),
        name="retention",
    )(rates, q, k, v, q, k, v)


def _finish_outproj(y, x_ref, mod_ref, g2_ref, x1_ref, h2_ref):
    x1 = x_ref[...] + mod_ref[2:3, :] * y
    x1_ref[...] = x1
    h2_ref[...] = (_rms(x1) * g2_ref[...]) * (1.0 + mod_ref[4:5, :]) + mod_ref[3:4, :]


def _even_outproj_kernel(x_ref, ya_ref, of_ref, or_ref, gb_ref, w_ref, mod_ref, g2_ref, x1_ref, h2_ref):
    half = ya_ref.shape[1]
    y = _dot(ya_ref[...], w_ref[0:half, :])
    for c in range(half // LANES):
        sl = slice(c * LANES, (c + 1) * LANES)
        gate = gb_ref[:, sl]
        r = _rms(of_ref[:, sl] + or_ref[:, sl]) * (gate * _sigmoid(gate))
        y = y + _dot(r.astype(BF16), w_ref[half + c * LANES:half + (c + 1) * LANES, :])
    _finish_outproj(y, x_ref, mod_ref, g2_ref, x1_ref, h2_ref)


def _even_outproj(x, ya, o_f, o_r, gb, w, mods, g2):
    nb, l, d = x.shape
    tm = ROW_TILE
    row = lambda width: pl.BlockSpec((None, tm, width), lambda b, t: (b, t, 0))
    return pl.pallas_call(
        _even_outproj_kernel,
        grid=(nb, l // tm),
        in_specs=[row(d), row(512), row(512), row(512), row(512),
                  pl.BlockSpec((d, d), lambda b, t: (0, 0)),
                  pl.BlockSpec((None, 6, d), _mod_index(nb)),
                  pl.BlockSpec((1, d), lambda b, t: (0, 0))],
        out_specs=[row(d), row(d)],
        out_shape=[jax.ShapeDtypeStruct((nb, l, d), F32)] * 2,
        compiler_params=_cparams(("parallel", "parallel")),
        name="even_outproj",
    )(x, ya, o_f, o_r, gb, w, mods, g2)


def _odd_outproj_kernel(x_ref, mix_ref, w_ref, mod_ref, g2_ref, x1_ref, h2_ref):
    _finish_outproj(_dot(mix_ref[...], w_ref[...]), x_ref, mod_ref, g2_ref, x1_ref, h2_ref)


def _odd_outproj(x, mix, w, mods, g2, tile_off):
    nb, l, d = x.shape
    tm = ROW_TILE
    row = pl.BlockSpec((None, tm, d), lambda b, t: (b, t + tile_off, 0))
    return pl.pallas_call(
        _odd_outproj_kernel,
        grid=(nb, l // tm - tile_off),
        in_specs=[row, row,
                  pl.BlockSpec((d, d), lambda b, t: (0, 0)),
                  pl.BlockSpec((None, 6, d), lambda b, t: (b, 0, 0)),
                  pl.BlockSpec((1, d), lambda b, t: (0, 0))],
        out_specs=[row, row],
        out_shape=[jax.ShapeDtypeStruct((nb, l, d), F32)] * 2,
        compiler_params=_cparams(("parallel", "parallel")),
        name="odd_outproj",
    )(x, mix, w, mods, g2)


def _top_rows(s, k, payload=None):
    rows = lax.broadcasted_iota(I32, s.shape, 0).astype(F32)
    n = float(s.shape[0])
    vals, pays = [], []
    for _ in range(k):
        m = jnp.max(s, axis=0, keepdims=True)
        am = jnp.min(jnp.where(s == m, rows, n), axis=0, keepdims=True)
        hit = rows == am
        vals.append(m)
        pays.append(am if payload is None else jnp.max(jnp.where(hit, payload, -1.0), axis=0, keepdims=True))
        s = jnp.where(hit, -jnp.inf, s)
    return jnp.concatenate(vals, axis=0), jnp.concatenate(pays, axis=0)


def _pair_candidates(s1, i1, s2, i2):
    kk = PEER_TOPK
    sub = lax.broadcasted_iota(I32, (SUBLANES, s1.shape[1]), 0)
    cand = [s1[0:1, :] + s2]
    cid = [i1[0:1, :] * PEER_NKEYS + i2]
    for a in range(1, SUBLANES):
        ok = sub < kk // (a + 1)
        cand.append(jnp.where(ok, s1[a:a + 1, :] + s2[0:SUBLANES, :], -jnp.inf))
        cid.append(i1[a:a + 1, :] * PEER_NKEYS + i2[0:SUBLANES, :])
    cand.append(s1[SUBLANES:kk, :] + s2[0:1, :])
    cid.append(i1[SUBLANES:kk, :] * PEER_NKEYS + i2[0:1, :])
    return jnp.concatenate(cand, axis=0), jnp.concatenate(cid, axis=0)


def _compress_rows(sel, skip, arrays):
    n = sel.shape[0]
    cc = jnp.where(sel, skip, 0)
    vals = [jnp.where(sel, a, jnp.zeros_like(a)) for a in arrays]
    for k in range(n.bit_length() - 1):
        mv = ((cc >> k) & 1) == 1

        def step(a):
            zero = jnp.zeros_like(a)
            return jnp.where(mv, zero, a) + pltpu.roll(jnp.where(mv, a, zero), n - (1 << k), 0)

        vals = [step(a) for a in vals]
        cc = step(cc)
    return vals


def _peer_topk_kernel(h_ref, wq_ref, keys_ref, off_ref, gate_ref, ngrp_ref, *, half_rows):
    kk = PEER_TOPK
    q = _dot(h_ref[...].astype(BF16), wq_ref[...])
    eids, gates = [], []
    for hd in range(PEER_HEADS):
        tops = []
        for part in range(2):
            c = 2 * hd + part
            qs = q[:, c * LANES:(c + 1) * LANES].astype(BF16)
            tops.append(_top_rows(_dot_nt(keys_ref[c], qs), kk))
        (s1, i1), (s2, i2) = tops
        cand, cid = _pair_candidates(s1, i1, s2, i2)
        best, eid = _top_rows(cand, kk, cid)
        e = jnp.exp(best - best[0:1, :])
        eids.append(eid)
        gates.append(e / jnp.sum(e, axis=0, keepdims=True))
    eid = jnp.concatenate(eids, axis=0).astype(I32)
    gate = jnp.concatenate(gates, axis=0)
    npick, tm = eid.shape
    off = (eid & (half_rows - 1)) * SUBLANES
    upper = eid >= half_rows
    row = lax.broadcasted_iota(I32, (npick, tm), 0)
    before = (lax.broadcasted_iota(I32, (npick, npick), 1) < lax.broadcasted_iota(I32, (npick, npick), 0))
    upper_before = _dot(before.astype(BF16), upper.astype(BF16)).astype(I32)
    for hf, (sel, skip) in enumerate(((~upper, upper_before), (upper, row - upper_before))):
        off_c, gate_c = _compress_rows(sel, skip, [off, gate])
        off_ref[hf] = off_c.T
        gate_ref[hf] = gate_c.T
        count = jnp.sum(sel.astype(F32), axis=0, keepdims=True)
        ngrp = ((count + (SUBLANES - 1)) * (1.0 / SUBLANES)).astype(I32)
        ngrp_ref[hf] = jnp.broadcast_to(ngrp, (npick, tm)).T


def _flat_tile(nt_all, nt, tile_off):
    return lambda i: (i // nt) * nt_all + (i % nt) + tile_off


def _peer_topk(h2, wq, keys, half_rows, nb, tile_off):
    n, d = h2.shape
    tm = ROW_TILE
    nt_all = n // nb // tm
    nt = nt_all - tile_off
    ft = _flat_tile(nt_all, nt, tile_off)
    npick = PEER_HEADS * PEER_TOPK
    out = pl.BlockSpec((2, tm, npick), lambda i: (0, ft(i), 0))
    return pl.pallas_call(
        functools.partial(_peer_topk_kernel, half_rows=half_rows),
        grid=(nb * nt,),
        in_specs=[pl.BlockSpec((tm, d), lambda i: (ft(i), 0)),
                  pl.BlockSpec(wq.shape, lambda i: (0, 0)),
                  pl.BlockSpec(keys.shape, lambda i: (0, 0, 0))],
        out_specs=[out, out, out],
        out_shape=[jax.ShapeDtypeStruct((2, n, npick), I32), jax.ShapeDtypeStruct((2, n, npick), F32),
                   jax.ShapeDtypeStruct((2, n, npick), I32)],
        compiler_params=_cparams(("parallel",)),
        name="peer_topk",
    )(h2, wq, keys)


def _load_table_half(tab_hbm, tab_ref, sem, half_rows):
    hf = pl.program_id(0)

    @pl.when(pl.program_id(1) == 0)
    def _():
        cp = pltpu.make_async_copy(tab_hbm.at[pl.ds(hf * half_rows, half_rows)], tab_ref, sem)
        cp.start()
        cp.wait()


def _table_row(tab_ref, off):
    return tab_ref[pl.ds(pl.multiple_of(off, SUBLANES), SUBLANES), :]


def _sublane_sum_masks():
    sub = lax.broadcasted_iota(I32, (SUBLANES, LANES), 0)
    return ((sub + 2) % 8 < 4,
            (sub + 1) % 8 < 4,
            sub < 4,
            (sub + 7) % 8 < 4,
            sub % 4 < 2,
            (sub + 3) % 4 < 2,
            sub % 2 == 1)


def _sublane_sums(prods, masks):
    def merge(a, b, m, shift):
        return jnp.where(m, a, b) + pltpu.roll(jnp.where(m, b, a), shift, 0)

    s0 = merge(prods[0], prods[4], masks[0], 4)
    s1 = merge(prods[1], prods[5], masks[1], 4)
    s2 = merge(prods[2], prods[6], masks[2], 4)
    s3 = merge(prods[3], prods[7], masks[3], 4)
    u0 = merge(s0, s2, masks[4], 2)
    u1 = merge(s1, s3, masks[5], 2)
    return merge(u0, u1, masks[6], 1)


def _peer_dot_kernel(iota_ref, off_ref, ngrp_ref, gate_ref, x_ref, tab_hbm, d_ref, tab_ref, q_ref, sem):
    _load_table_half(tab_hbm, tab_ref, sem, tab_ref.shape[0])
    npick, tb = d_ref.shape
    xr = PEER_XLU_GROUPS * SUBLANES
    lane_lo = lax.broadcasted_iota(I32, (xr, tb), 1)
    lane_hi = lax.broadcasted_iota(I32, (npick - xr, tb), 1)
    masks = _sublane_sum_masks()
    ks = [iota_ref[k] for k in range(SUBLANES)]

    @pl.when(pl.program_id(1) == 0)
    def _():
        def clear(t, carry):
            q_ref[t] = jnp.zeros((npick, LANES), F32)
            return carry
        lax.fori_loop(0, tb, clear, 0)

    def gather(t, carry):
        x = x_ref[t]
        base = pl.multiple_of(t * npick, npick)

        def group(start):
            offs = off_ref.at[pl.ds(base + start, SUBLANES)]
            prods = [_table_row(tab_ref, offs[ks[(k + 1) % SUBLANES]]) * x for k in range(SUBLANES)]
            q_ref[t, pl.ds(start, SUBLANES), :] = _sublane_sums(prods, masks)

        for g in range(PEER_STATIC_GROUPS):
            group(g * SUBLANES)

        def extra(g, c):
            group(pl.multiple_of(g * SUBLANES, SUBLANES))
            return c

        lax.fori_loop(PEER_STATIC_GROUPS, ngrp_ref[base], extra, 0)
        return carry

    def lane_reduce(t, acc):
        col = jnp.sum(q_ref[t, 0:xr, :], axis=1, keepdims=True)
        return jnp.where(lane_lo == t, col, acc)

    def lane_reduce_rest(t, carry):
        @pl.when(ngrp_ref[t * npick] > PEER_XLU_GROUPS)
        def _():
            col = jnp.sum(q_ref[t, xr:npick, :], axis=1, keepdims=True)
            d_ref[xr:npick, :] = jnp.where(lane_hi == t, col, d_ref[xr:npick, :])
        return carry

    lax.fori_loop(0, tb, gather, 0)
    d_ref[0:xr, :] = lax.fori_loop(0, tb, lane_reduce, jnp.zeros((xr, tb), F32), unroll=PEER_XLU_UNROLL)
    d_ref[xr:npick, :] = jnp.zeros((npick - xr, tb), F32)

    @pl.when(jnp.max(gate_ref[:, xr:npick]) > 0.0)
    def _():
        lax.fori_loop(0, tb, lane_reduce_rest, 0)


def _peer_sum_kernel(iota_ref, off_ref, ngrp_ref, gate_ref, d_ref, tab_hbm, f_ref, tab_ref, wl_ref, sem):
    _load_table_half(tab_hbm, tab_ref, sem, tab_ref.shape[0])
    tb, npick = gate_ref.shape
    xr = PEER_XLU_GROUPS * SUBLANES
    nacc = 4
    ks = [iota_ref[k] for k in range(SUBLANES)]
    gate = gate_ref[...].T
    d = d_ref[...]
    act = 0.5 * d * (1.0 + lax.erf(d * (2.0 ** -0.5)))
    wt = jnp.where(gate > 0.0, gate * act, 0.0)
    lane_lo = lax.broadcasted_iota(I32, (xr, tb), 1)
    lane_hi = lax.broadcasted_iota(I32, (npick - xr, tb), 1)

    def spread(t, carry):
        col = jnp.sum(jnp.where(lane_lo == t, wt[0:xr], 0.0), axis=1, keepdims=True)
        wl_ref[t, 0:xr, :] = jnp.broadcast_to(col, (xr, LANES))
        return carry

    def spread_rest(t, carry):
        @pl.when(ngrp_ref[t * npick] > PEER_XLU_GROUPS)
        def _():
            col = jnp.sum(jnp.where(lane_hi == t, wt[xr:npick], 0.0), axis=1, keepdims=True)
            wl_ref[t, xr:npick, :] = jnp.broadcast_to(col, (npick - xr, LANES))
        return carry

    def token(t, carry):
        base = pl.multiple_of(t * npick, npick)

        def group(start, accs):
            accs = list(accs)
            offs = off_ref.at[pl.ds(base + start, SUBLANES)]
            for k in range(SUBLANES):
                wv = jnp.broadcast_to(wl_ref[t, pl.ds(start + k, 1), :], (SUBLANES, LANES))
                accs[k % nacc] = accs[k % nacc] + wv * _table_row(tab_ref, offs[ks[k]])
            return tuple(accs)

        accs = tuple(jnp.zeros((SUBLANES, LANES), F32) for _ in range(nacc))
        for g in range(PEER_STATIC_GROUPS):
            accs = group(g * SUBLANES, accs)
        accs = lax.fori_loop(PEER_STATIC_GROUPS, ngrp_ref[base],
                             lambda g, a: group(pl.multiple_of(g * SUBLANES, SUBLANES), a), accs)
        f_ref[t] = (accs[0] + accs[1]) + (accs[2] + accs[3])
        return carry

    lax.fori_loop(0, tb, spread, 0, unroll=PEER_XLU_UNROLL)

    @pl.when(jnp.max(gate_ref[:, xr:npick]) > 0.0)
    def _():
        lax.fori_loop(0, tb, spread_rest, 0)

    lax.fori_loop(0, tb, token, 0)


def _peer_tiles(n, nb, tile_off_rows):
    tb = PEER_TOK
    nt_all = n // nb // tb
    off = tile_off_rows // tb
    nt = nt_all - off
    return tb, nb * nt, _flat_tile(nt_all, nt, off)


def _smem_token_block(tb, npick, blocks_per_half, ft):
    return pl.BlockSpec((tb * npick,), lambda hf, i: (hf * blocks_per_half + ft(i),), memory_space=pltpu.SMEM)


def _peer_dot(off_flat, ngrp_flat, gate2, h2v, tab2, nb, tile_off_rows):
    _, n, npick = gate2.shape
    tb, steps, ft = _peer_tiles(n, nb, tile_off_rows)
    half = tab2.shape[0] // 2
    smem_blk = _smem_token_block(tb, npick, n // tb, ft)
    return pl.pallas_call(
        _peer_dot_kernel,
        grid=(2, steps),
        in_specs=[pl.BlockSpec(memory_space=pltpu.SMEM), smem_blk, smem_blk,
                  pl.BlockSpec((None, tb, npick), lambda hf, i: (hf, ft(i), 0)),
                  pl.BlockSpec((tb, SUBLANES, LANES), lambda hf, i: (ft(i), 0, 0)),
                  pl.BlockSpec(memory_space=pl.ANY)],
        out_specs=pl.BlockSpec((None, npick, tb), lambda hf, i: (hf, 0, ft(i))),
        out_shape=jax.ShapeDtypeStruct((2, npick, n), F32),
        scratch_shapes=[pltpu.VMEM((half, LANES), F32),
                        pltpu.VMEM((tb, npick, LANES), F32),
                        pltpu.SemaphoreType.DMA],
        compiler_params=_cparams(("arbitrary", "arbitrary"), VMEM_LIMIT_TABLE),
        name="peer_dot",
    )(jnp.arange(SUBLANES, dtype=I32), off_flat, ngrp_flat, gate2, h2v, tab2)


def _peer_sum(off_flat, ngrp_flat, gate2, dpart, tab2, nb, tile_off_rows):
    _, n, npick = gate2.shape
    tb, steps, ft = _peer_tiles(n, nb, tile_off_rows)
    half = tab2.shape[0] // 2
    smem_blk = _smem_token_block(tb, npick, n // tb, ft)
    return pl.pallas_call(
        _peer_sum_kernel,
        grid=(2, steps),
        in_specs=[pl.BlockSpec(memory_space=pltpu.SMEM), smem_blk, smem_blk,
                  pl.BlockSpec((None, tb, npick), lambda hf, i: (hf, ft(i), 0)),
                  pl.BlockSpec((None, npick, tb), lambda hf, i: (hf, 0, ft(i))),
                  pl.BlockSpec(memory_space=pl.ANY)],
        out_specs=pl.BlockSpec((None, tb, SUBLANES, LANES), lambda hf, i: (hf, ft(i), 0, 0)),
        out_shape=jax.ShapeDtypeStruct((2, n, SUBLANES, LANES), F32),
        scratch_shapes=[pltpu.VMEM((half, LANES), F32),
                        pltpu.VMEM((tb, npick, LANES), F32),
                        pltpu.SemaphoreType.DMA],
        compiler_params=_cparams(("arbitrary", "arbitrary"), VMEM_LIMIT_TABLE),
        name="peer_sum",
    )(jnp.arange(SUBLANES, dtype=I32), off_flat, ngrp_flat, gate2, dpart, tab2)


def _residual_kernel(x_ref, f_ref, mod_ref, o_ref):
    o_ref[...] = x_ref[...] + mod_ref[5:6, :] * (f_ref[0] + f_ref[1])


def _final_kernel(x_ref, f_ref, mod_ref, g_ref, o_ref):
    o_ref[...] = _rms(x_ref[...] + mod_ref[5:6, :] * (f_ref[0] + f_ref[1])) * g_ref[...]


def _residual(x1, f2, mods):
    nb, l, d = x1.shape
    tm = ROW_TILE
    return pl.pallas_call(
        _residual_kernel,
        grid=(nb, l // tm),
        in_specs=[pl.BlockSpec((None, tm, d), lambda b, t: (b, t, 0)),
                  pl.BlockSpec((2, None, tm, d), lambda b, t: (0, b, t, 0)),
                  pl.BlockSpec((None, 6, d), _mod_index(nb))],
        out_specs=pl.BlockSpec((None, tm, d), lambda b, t: (b, t, 0)),
        out_shape=jax.ShapeDtypeStruct((nb, l, d), F32),
        compiler_params=_cparams(("parallel", "parallel")),
        name="peer_residual",
    )(x1, f2, mods)


def _final(x1, f2, mods, g, tile_off):
    nb, l, d = x1.shape
    tm = ROW_TILE
    nt = l // tm - tile_off
    return pl.pallas_call(
        _final_kernel,
        grid=(nb, nt),
        in_specs=[pl.BlockSpec((None, tm, d), lambda b, t: (b, t + tile_off, 0)),
                  pl.BlockSpec((2, None, tm, d), lambda b, t: (0, b, t + tile_off, 0)),
                  pl.BlockSpec((None, 6, d), lambda b, t: (b, 0, 0)),
                  pl.BlockSpec((1, d), lambda b, t: (0, 0))],
        out_specs=pl.BlockSpec((None, tm, d), lambda b, t: (b, t, 0)),
        out_shape=jax.ShapeDtypeStruct((nb, nt * tm, d), F32),
        compiler_params=_cparams(("parallel", "parallel")),
        name="final_norm",
    )(x1, f2, mods, g)


def _rope_tables(seq, ctx_len):
    t = jnp.arange(seq)
    row_id = (t // GRID_W).astype(F32)
    col_id = (t % GRID_W).astype(F32)
    axis_dim = HEAD_DIM // 2
    inv = ROPE_THETA ** (-jnp.arange(0, axis_dim, 2, dtype=F32) / axis_dim)
    ang_r = row_id[:, None] * inv[None, :]
    ang_c = col_id[:, None] * inv[None, :]
    cos = jnp.concatenate([jnp.cos(ang_r)] * 2 + [jnp.cos(ang_c)] * 2, axis=-1)
    sin = jnp.concatenate([-jnp.sin(ang_r), jnp.sin(ang_r), -jnp.sin(ang_c), jnp.sin(ang_c)], axis=-1)
    cos = jnp.concatenate([jnp.ones((ctx_len, HEAD_DIM), F32), cos], axis=0)
    sin = jnp.concatenate([jnp.zeros((ctx_len, HEAD_DIM), F32), sin], axis=0)
    return jnp.tile(cos, (1, 2)), jnp.tile(sin, (1, 2))


def _peer(h2, x1, mods, wq, keys, u_tab, v_tab, nb, tile_off_rows, final_g):
    b, l, d = x1.shape
    n = b * l
    nexp = u_tab.shape[0]
    npick = PEER_HEADS * PEER_TOPK
    off2, gate2, ngrp2 = _peer_topk(h2.reshape(n, d), wq.astype(BF16),
                                    keys.reshape(PEER_HEADS * 2, PEER_NKEYS, -1).astype(BF16),
                                    nexp // 2, nb, tile_off_rows // ROW_TILE)
    off_flat = off2.reshape(2 * n * npick)
    ngrp_flat = ngrp2.reshape(2 * n * npick)
    h2v = h2.reshape(n, SUBLANES, LANES)
    dpart = _peer_dot(off_flat, ngrp_flat, gate2, h2v, u_tab.reshape(nexp * SUBLANES, LANES), nb, tile_off_rows)
    f2 = _peer_sum(off_flat, ngrp_flat, gate2, dpart, v_tab.reshape(nexp * SUBLANES, LANES), nb, tile_off_rows)
    f2 = f2.reshape(2, b, l, d)
    if final_g is None:
        return _residual(x1, f2, mods)
    return _final(x1, f2, mods, final_g.reshape(1, d), tile_off_rows // ROW_TILE)


def kernel(x, c, ctx, c_ctx, ada_w, ada_b, norm1_g, norm2_g, ev_w_in, ev_w_out, gqa_q_norm_g,
           gqa_k_norm_g, ret_log_rate, od_w_in, od_w_out, diff_lambda, diff_subln_g, peer_w_q,
           peer_keys, peer_u, peer_v, final_g):
    nb, seq, d = x.shape
    ctx_len = ctx.shape[1]
    depth = ada_w.shape[0]
    assert ctx_len == ROW_TILE and seq % ROW_TILE == 0 and d == SUBLANES * LANES
    xs = jnp.concatenate([ctx, x], axis=1)
    cos, sin = _rope_tables(seq, ctx_len)
    cc = jnp.concatenate([c, c_ctx[None, :]], axis=0)
    for layer in range(depth):
        last = layer == depth - 1
        mods = _ada(cc, ada_w[layer], ada_b[layer]).reshape(nb + 1, 6, d)
        g1 = norm1_g[layer].reshape(1, d)
        g2 = norm2_g[layer].reshape(1, d)
        lat_off = 1 if last else 0
        if layer % 2 == 0:
            e = layer // 2
            qa, ka, va, qb, kb, vb, gb = _even_inproj(
                xs, g1, mods, ev_w_in[e].astype(BF16), cos, sin,
                jnp.tile(gqa_q_norm_g[e], 2).reshape(1, LANES), jnp.tile(gqa_k_norm_g[e], 2).reshape(1, LANES))
            ya = _gqa(qa, ka, va, ctx_len, 0)
            o_f, o_r = _retention(ret_log_rate[e], qb, kb, vb, ctx_len)
            x1, h2 = _even_outproj(xs, ya, o_f, o_r, gb, ev_w_out[e].astype(BF16), mods, g2)
        else:
            o = layer // 2
            lam_init = 0.8 - 0.6 * math.exp(-0.3 * layer)
            q, k, v = _odd_inproj(xs, g1, mods, od_w_in[o].astype(BF16), cos, sin)
            mix = _diff_attention(q, k, v, diff_lambda[o], diff_subln_g[o].reshape(1, LANES),
                                  ctx_len, lat_off, lam_init)
            x1, h2 = _odd_outproj(xs, mix, od_w_out[o].astype(BF16), mods, g2, lat_off)
        xs = _peer(h2, x1, mods, peer_w_q[layer], peer_keys[layer], peer_u[layer], peer_v[layer],
                   nb, lat_off * ROW_TILE, final_g if last else None)
    return xs
```

```python
import functools
import math

import jax
import jax.numpy as jnp
from jax import lax
from jax.experimental import pallas as pl
from jax.experimental.pallas import tpu as pltpu

F32 = jnp.float32
BF16 = jnp.bfloat16
I32 = jnp.int32

LANES = 128
SUBLANES = 8
HEAD_DIM = 64
GRID_W = 64
ROPE_THETA = 10000.0
NORM_EPS = 1e-6
RET_CHUNK = 128
PEER_NKEYS = 128
PEER_TOPK = 16
PEER_HEADS = 8
ROW_TILE = 256
PEER_TOK = 128
PEER_STATIC_GROUPS = 9
PEER_XLU_GROUPS = 11
PEER_XLU_UNROLL = 32
KV_CHUNK = 256
VMEM_LIMIT = 48 * 1024 * 1024
VMEM_LIMIT_TABLE = 56 * 1024 * 1024


def _cparams(sem, limit=VMEM_LIMIT):
    return pltpu.CompilerParams(dimension_semantics=sem, vmem_limit_bytes=limit)


def _rms(x):
    return x * lax.rsqrt(jnp.mean(x * x, axis=-1, keepdims=True) + NORM_EPS)


def _dot(a, b):
    return jnp.dot(a, b, preferred_element_type=F32)


def _dot_nt(a, b):
    return lax.dot_general(a, b, (((1,), (1,)), ((), ())), preferred_element_type=F32)


def _sigmoid(x):
    return 1.0 / (1.0 + jnp.exp(-x))


def _ada_kernel(c_ref, w_ref, b_ref, o_ref):
    c = c_ref[...]
    a = (c * _sigmoid(c)).astype(BF16)
    o_ref[...] = _dot(a, w_ref[...].astype(BF16)) + b_ref[...]


def _ada(cc, w, b):
    m, d = cc.shape
    n = w.shape[1]
    tn = 1024
    return pl.pallas_call(
        _ada_kernel,
        grid=(n // tn,),
        in_specs=[pl.BlockSpec((m, d), lambda j: (0, 0)),
                  pl.BlockSpec((d, tn), lambda j: (0, j)),
                  pl.BlockSpec((1, tn), lambda j: (0, j))],
        out_specs=pl.BlockSpec((m, tn), lambda j: (0, j)),
        out_shape=jax.ShapeDtypeStruct((m, n), F32),
        compiler_params=_cparams(("arbitrary",)),
        name="ada_mod",
    )(cc, w, b.reshape(1, n))


def _rope_cols(x, cos, sin_signed):
    lane = lax.broadcasted_iota(I32, x.shape, 1)
    partner = jnp.where((lane % 32) < 16, pltpu.roll(x, LANES - 16, 1), pltpu.roll(x, 16, 1))
    return x * cos + partner * sin_signed


def _head_mean_sq(x):
    r = lax.broadcasted_iota(I32, (LANES, LANES), 0) // HEAD_DIM
    c = lax.broadcasted_iota(I32, (LANES, LANES), 1) // HEAD_DIM
    ones_bd = jnp.where(r == c, 1.0, 0.0).astype(BF16)
    sq = x * x
    hi = sq.astype(BF16)
    lo = (sq - hi.astype(F32)).astype(BF16)
    return (_dot(hi, ones_bd) + _dot(lo, ones_bd)) * (1.0 / HEAD_DIM)


def _prologue(x_ref, g_ref, mod_ref, shift_row, scale_row):
    h = _rms(x_ref[...]) * g_ref[...]
    return h * (1.0 + mod_ref[scale_row:scale_row + 1, :]) + mod_ref[shift_row:shift_row + 1, :]


def _even_inproj_kernel(x_ref, g_ref, mod_ref, w_ref, cos_ref, sin_ref, qg_ref, kg_ref,
                        qa_ref, ka_ref, va_ref, qb_ref, kb_ref, vb_ref, gb_ref):
    h = _prologue(x_ref, g_ref, mod_ref, 0, 1).astype(BF16)
    y = _dot(h, w_ref[...])
    cos = cos_ref[...]
    sin = sin_ref[...]
    scale = HEAD_DIM ** -0.5

    def col(c):
        return y[:, c * LANES:(c + 1) * LANES]

    for c in range(4):
        x = col(c)
        x = x * lax.rsqrt(_head_mean_sq(x) + NORM_EPS) * qg_ref[...]
        qa_ref[:, c * LANES:(c + 1) * LANES] = (_rope_cols(x, cos, sin) * scale).astype(BF16)
    x = col(4)
    x = x * lax.rsqrt(_head_mean_sq(x) + NORM_EPS) * kg_ref[...]
    ka_ref[...] = _rope_cols(x, cos, sin).astype(BF16)
    va_ref[...] = col(5).astype(BF16)
    for c in range(2):
        qb_ref[:, c * LANES:(c + 1) * LANES] = _rope_cols(col(6 + c), cos, sin).astype(BF16)
        kb_ref[:, c * LANES:(c + 1) * LANES] = _rope_cols(col(8 + c) * scale, cos, sin).astype(BF16)
    vb_ref[...] = y[:, 10 * LANES:14 * LANES].astype(BF16)
    gb_ref[...] = y[:, 14 * LANES:18 * LANES]


def _mod_index(nb):
    return lambda b, t: (jnp.where(t == 0, nb, b), 0, 0)


def _even_inproj(x, g, mods, w, cos, sin, qg, kg):
    nb, l, d = x.shape
    tm = ROW_TILE
    n_in = w.shape[1]
    row = lambda width: pl.BlockSpec((None, tm, width), lambda b, t: (b, t, 0))
    widths = (512, 128, 128, 256, 256, 512, 512)
    dts = (BF16, BF16, BF16, BF16, BF16, BF16, F32)
    return pl.pallas_call(
        _even_inproj_kernel,
        grid=(nb, l // tm),
        in_specs=[row(d),
                  pl.BlockSpec((1, d), lambda b, t: (0, 0)),
                  pl.BlockSpec((None, 6, d), _mod_index(nb)),
                  pl.BlockSpec((d, n_in), lambda b, t: (0, 0)),
                  pl.BlockSpec((tm, LANES), lambda b, t: (t, 0)),
                  pl.BlockSpec((tm, LANES), lambda b, t: (t, 0)),
                  pl.BlockSpec((1, LANES), lambda b, t: (0, 0)),
                  pl.BlockSpec((1, LANES), lambda b, t: (0, 0))],
        out_specs=[row(wd) for wd in widths],
        out_shape=[jax.ShapeDtypeStruct((nb, l, wd), dt) for wd, dt in zip(widths, dts)],
        compiler_params=_cparams(("parallel", "parallel")),
        name="even_inproj",
    )(x, g, mods, w, cos, sin, qg, kg)


def _odd_inproj_kernel(x_ref, g_ref, mod_ref, w_ref, cos_ref, sin_ref, q_ref, k_ref, v_ref):
    h = _prologue(x_ref, g_ref, mod_ref, 0, 1).astype(BF16)
    y = _dot(h, w_ref[...])
    cos = cos_ref[...]
    sin = sin_ref[...]
    scale = HEAD_DIM ** -0.5
    for c in range(8):
        sl = slice(c * LANES, (c + 1) * LANES)
        q_ref[:, sl] = (_rope_cols(y[:, sl], cos, sin) * scale).astype(BF16)
        k_ref[:, sl] = _rope_cols(y[:, 8 * LANES + c * LANES:8 * LANES + (c + 1) * LANES], cos, sin).astype(BF16)
    v_ref[...] = y[:, 16 * LANES:24 * LANES].astype(BF16)


def _odd_inproj(x, g, mods, w, cos, sin):
    nb, l, d = x.shape
    tm = ROW_TILE
    n_in = w.shape[1]
    row = lambda width: pl.BlockSpec((None, tm, width), lambda b, t: (b, t, 0))
    return pl.pallas_call(
        _odd_inproj_kernel,
        grid=(nb, l // tm),
        in_specs=[row(d),
                  pl.BlockSpec((1, d), lambda b, t: (0, 0)),
                  pl.BlockSpec((None, 6, d), _mod_index(nb)),
                  pl.BlockSpec((d, n_in), lambda b, t: (0, 0)),
                  pl.BlockSpec((tm, LANES), lambda b, t: (t, 0)),
                  pl.BlockSpec((tm, LANES), lambda b, t: (t, 0))],
        out_specs=[row(d), row(d), row(d)],
        out_shape=[jax.ShapeDtypeStruct((nb, l, d), BF16)] * 3,
        compiler_params=_cparams(("parallel", "parallel")),
        name="odd_inproj",
    )(x, g, mods, w, cos, sin)


def _scores(q, k_ref, nchunks):
    kc = KV_CHUNK
    scores, run = [], None
    for c in range(nchunks):
        s = _dot_nt(q, k_ref[c * kc:(c + 1) * kc, :])
        scores.append(s)
        for j in range(kc // LANES):
            col = s[:, j * LANES:(j + 1) * LANES]
            run = col if run is None else jnp.maximum(run, col)
    return scores, jnp.max(run, axis=-1, keepdims=True)


def _softmax_pv(scores_max, v_ref, next_q=None, k_ref=None):
    scores, m = scores_max
    kc = KV_CHUNK
    acc = jnp.zeros((scores[0].shape[0], v_ref.shape[1]), F32)
    den = jnp.zeros((scores[0].shape[0], LANES), F32)
    nxt, run = [], None
    for c, s in enumerate(scores):
        if next_q is not None:
            sn = _dot_nt(next_q, k_ref[c * kc:(c + 1) * kc, :])
            nxt.append(sn)
            for j in range(kc // LANES):
                col = sn[:, j * LANES:(j + 1) * LANES]
                run = col if run is None else jnp.maximum(run, col)
        e = jnp.exp(s - m)
        for j in range(kc // LANES):
            den = den + e[:, j * LANES:(j + 1) * LANES]
        acc = acc + _dot(e.astype(BF16), v_ref[c * kc:(c + 1) * kc, :])
    den = jnp.sum(den, axis=-1, keepdims=True)
    if next_q is None:
        return acc, den
    return acc, den, (nxt, jnp.max(run, axis=-1, keepdims=True))


def _key_chunk_branches(qi, ctx_len, total, body):
    @pl.when(qi == 0)
    def _():
        body(ctx_len // KV_CHUNK)

    @pl.when(qi != 0)
    def _():
        body(total // KV_CHUNK)


def _gqa_kernel(q_ref, k_ref, v_ref, o_ref, *, ctx_len, tile_off):
    g = pl.program_id(1)
    qi = pl.program_id(2) + tile_off
    tq = q_ref.shape[0]
    half = lax.broadcasted_iota(I32, (tq, LANES), 1) // HEAD_DIM

    def head_q(i):
        qh = q_ref[:, (i // 2) * LANES:(i // 2 + 1) * LANES]
        qsel = jnp.where(half == (i % 2), qh, jnp.zeros_like(qh)).astype(F32)
        return jnp.where(g == (i % 2), qsel, pltpu.roll(qsel, HEAD_DIM, 1)).astype(BF16)

    def body(nchunks):
        cols = [jnp.zeros((tq, LANES), F32), jnp.zeros((tq, LANES), F32)]
        cur = _scores(head_q(0), k_ref, nchunks)
        for i in range(4):
            if i + 1 < 4:
                acc, den, cur = _softmax_pv(cur, v_ref, head_q(i + 1), k_ref)
            else:
                acc, den = _softmax_pv(cur, v_ref)
            o = acc / den
            osel = jnp.where(half == g, o, 0.0)
            cols[i // 2] = cols[i // 2] + jnp.where(g == (i % 2), osel, pltpu.roll(osel, HEAD_DIM, 1))
        o_ref[:, 0:LANES] = cols[0].astype(BF16)
        o_ref[:, LANES:2 * LANES] = cols[1].astype(BF16)

    if tile_off == 0:
        _key_chunk_branches(qi, ctx_len, k_ref.shape[0], body)
    else:
        body(k_ref.shape[0] // KV_CHUNK)


def _gqa(q, k, v, ctx_len, tile_off):
    nb, l, _ = q.shape
    tq = ROW_TILE
    nt = l // tq - tile_off
    return pl.pallas_call(
        functools.partial(_gqa_kernel, ctx_len=ctx_len, tile_off=tile_off),
        grid=(nb, 2, nt),
        in_specs=[pl.BlockSpec((None, tq, 2 * LANES), lambda b, g, t: (b, t + tile_off, g)),
                  pl.BlockSpec((None, l, LANES), lambda b, g, t: (b, 0, 0)),
                  pl.BlockSpec((None, l, LANES), lambda b, g, t: (b, 0, 0))],
        out_specs=pl.BlockSpec((None, tq, 2 * LANES), lambda b, g, t: (b, t + tile_off, g)),
        out_shape=jax.ShapeDtypeStruct((nb, l, 4 * LANES), BF16),
        compiler_params=_cparams(("parallel", "parallel", "parallel")),
        name="gqa_attention",
    )(q, k, v)


def _diff_kernel(q_ref, k_ref, v_ref, lam_ref, g_ref, o_ref, *, ctx_len, tile_off, lam_init):
    qi = pl.program_id(2) + tile_off
    tq = q_ref.shape[0]
    lf = lam_ref[...]
    lam = (jnp.exp(jnp.sum(lf[0:1, :] * lf[1:2, :], axis=-1, keepdims=True))
           - jnp.exp(jnp.sum(lf[2:3, :] * lf[3:4, :], axis=-1, keepdims=True)) + lam_init)
    half = lax.broadcasted_iota(I32, (tq, LANES), 1) // HEAD_DIM

    def body(nchunks):
        q = q_ref[...]
        zero = jnp.zeros_like(q)
        s1 = _scores(jnp.where(half == 0, q, zero), k_ref, nchunks)
        a1, d1, s2 = _softmax_pv(s1, v_ref, jnp.where(half == 1, q, zero), k_ref)
        a2, d2 = _softmax_pv(s2, v_ref)
        o = a1 / d1 - lam * (a2 / d2)
        o_ref[...] = (_rms(o) * g_ref[...] * (1.0 - lam_init)).astype(BF16)

    if tile_off == 0:
        _key_chunk_branches(qi, ctx_len, k_ref.shape[0], body)
    else:
        body(k_ref.shape[0] // KV_CHUNK)


def _diff_attention(q, k, v, lam_p, subln_g, ctx_len, tile_off, lam_init):
    nb, l, d = q.shape
    nh = d // LANES
    tq = ROW_TILE
    nt = l // tq - tile_off
    return pl.pallas_call(
        functools.partial(_diff_kernel, ctx_len=ctx_len, tile_off=tile_off, lam_init=lam_init),
        grid=(nb, nh, nt),
        in_specs=[pl.BlockSpec((None, tq, LANES), lambda b, h, t: (b, t + tile_off, h)),
                  pl.BlockSpec((None, l, LANES), lambda b, h, t: (b, 0, h)),
                  pl.BlockSpec((None, l, LANES), lambda b, h, t: (b, 0, h)),
                  pl.BlockSpec((4, HEAD_DIM), lambda b, h, t: (0, 0)),
                  pl.BlockSpec((1, LANES), lambda b, h, t: (0, 0))],
        out_specs=pl.BlockSpec((None, tq, LANES), lambda b, h, t: (b, t + tile_off, h)),
        out_shape=jax.ShapeDtypeStruct((nb, l, d), BF16),
        compiler_params=_cparams(("parallel", "parallel", "parallel")),
        name="diff_attention",
    )(q, k, v, lam_p, subln_g)


def _retention_kernel(rate_ref, qf_ref, kf_ref, vf_ref, qr_ref, kr_ref, vr_ref,
                      of_ref, or_ref, sf_ref, sr_ref):
    h = pl.program_id(1)
    n = pl.program_id(2)
    c = RET_CHUNK

    @pl.when(n == 0)
    def _():
        sf_ref[...] = jnp.zeros_like(sf_ref)
        sr_ref[...] = jnp.zeros_like(sr_ref)

    head_ok = (lax.broadcasted_iota(I32, (c, LANES), 1) // HEAD_DIM) == (h % 2)
    ri = lax.broadcasted_iota(I32, (c, c), 0)
    ci = lax.broadcasted_iota(I32, (c, c), 1)
    pos = lax.broadcasted_iota(I32, (c, 1), 0).astype(F32)

    def one(direction, q_ref, k_ref, v_ref, o_ref, s_ref):
        log_g = -jnp.exp(jnp.full((1, 1), rate_ref[direction, h], F32))
        rel = ((ri - ci) if direction == 0 else (ci - ri)).astype(F32)
        dmask = jnp.where(rel >= 0, jnp.exp(log_g * jnp.maximum(rel, 0.0)), 0.0)
        q = jnp.where(head_ok, q_ref[...], jnp.zeros((c, LANES), BF16))
        k = jnp.where(head_ok, k_ref[...], jnp.zeros((c, LANES), BF16))
        v = v_ref[...]
        sc = _dot_nt(q, k) * dmask
        intra = _dot(sc.astype(BF16), v)
        q_pow = (pos + 1.0) if direction == 0 else (c - pos)
        k_pow = (c - 1.0 - pos) if direction == 0 else pos
        state = s_ref[...]
        cross = _dot((q.astype(F32) * jnp.exp(log_g * q_pow)).astype(BF16), state.astype(BF16))
        o_ref[...] = intra + cross
        kd = (k.astype(F32) * jnp.exp(log_g * k_pow)).T.astype(BF16)
        s_ref[...] = jnp.exp(log_g * c) * state + _dot(kd, v)

    one(0, qf_ref, kf_ref, vf_ref, of_ref, sf_ref)
    one(1, qr_ref, kr_ref, vr_ref, or_ref, sr_ref)


def _retention(rates, q, k, v, ctx_len):
    nb, l, dv_all = v.shape
    c = RET_CHUNK
    nh = dv_all // LANES
    nctx = ctx_len // c
    ntot = l // c

    def rev(n):
        return jnp.where(n < nctx, nctx - 1 - n, ntot - 1 - (n - nctx))

    qk_f = pl.BlockSpec((None, c, LANES), lambda b, h, n: (b, n, h // 2))
    qk_r = pl.BlockSpec((None, c, LANES), lambda b, h, n: (b, rev(n), h // 2))
    v_f = pl.BlockSpec((None, c, LANES), lambda b, h, n: (b, n, h))
    v_r = pl.BlockSpec((None, c, LANES), lambda b, h, n: (b, rev(n), h))
    return pl.pallas_call(
        _retention_kernel,
        grid=(nb, nh, ntot),
        in_specs=[pl.BlockSpec(memory_space=pltpu.SMEM), qk_f, qk_f, v_f, qk_r, qk_r, v_r],
        out_specs=[v_f, v_r],
        out_shape=[jax.ShapeDtypeStruct((nb, l, dv_all), F32)] * 2,
        scratch_shapes=[pltpu.VMEM((LANES, LANES), F32), pltpu.VMEM((LANES, LANES), F32)],
        compiler_params=_cparams(("parallel", "parallel", "arbitrary")),
        name="retention",
    )(rates, q, k, v, q, k, v)


def _finish_outproj(y, x_ref, mod_ref, g2_ref, x1_ref, h2_ref):
    x1 = x_ref[...] + mod_ref[2:3, :] * y
    x1_ref[...] = x1
    h2_ref[...] = (_rms(x1) * g2_ref[...]) * (1.0 + mod_ref[4:5, :]) + mod_ref[3:4, :]


def _even_outproj_kernel(x_ref, ya_ref, of_ref, or_ref, gb_ref, w_ref, mod_ref, g2_ref, x1_ref, h2_ref):
    half = ya_ref.shape[1]
    y = _dot(ya_ref[...], w_ref[0:half, :])
    for c in range(half // LANES):
        sl = slice(c * LANES, (c + 1) * LANES)
        gate = gb_ref[:, sl]
        r = _rms(of_ref[:, sl] + or_ref[:, sl]) * (gate * _sigmoid(gate))
        y = y + _dot(r.astype(BF16), w_ref[half + c * LANES:half + (c + 1) * LANES, :])
    _finish_outproj(y, x_ref, mod_ref, g2_ref, x1_ref, h2_ref)


def _even_outproj(x, ya, o_f, o_r, gb, w, mods, g2):
    nb, l, d = x.shape
    tm = ROW_TILE
    row = lambda width: pl.BlockSpec((None, tm, width), lambda b, t: (b, t, 0))
    return pl.pallas_call(
        _even_outproj_kernel,
        grid=(nb, l // tm),
        in_specs=[row(d), row(512), row(512), row(512), row(512),
                  pl.BlockSpec((d, d), lambda b, t: (0, 0)),
                  pl.BlockSpec((None, 6, d), _mod_index(nb)),
                  pl.BlockSpec((1, d), lambda b, t: (0, 0))],
        out_specs=[row(d), row(d)],
        out_shape=[jax.ShapeDtypeStruct((nb, l, d), F32)] * 2,
        compiler_params=_cparams(("parallel", "parallel")),
        name="even_outproj",
    )(x, ya, o_f, o_r, gb, w, mods, g2)


def _odd_outproj_kernel(x_ref, mix_ref, w_ref, mod_ref, g2_ref, x1_ref, h2_ref):
    _finish_outproj(_dot(mix_ref[...], w_ref[...]), x_ref, mod_ref, g2_ref, x1_ref, h2_ref)


def _odd_outproj(x, mix, w, mods, g2, tile_off):
    nb, l, d = x.shape
    tm = ROW_TILE
    row = pl.BlockSpec((None, tm, d), lambda b, t: (b, t + tile_off, 0))
    return pl.pallas_call(
        _odd_outproj_kernel,
        grid=(nb, l // tm - tile_off),
        in_specs=[row, row,
                  pl.BlockSpec((d, d), lambda b, t: (0, 0)),
                  pl.BlockSpec((None, 6, d), lambda b, t: (b, 0, 0)),
                  pl.BlockSpec((1, d), lambda b, t: (0, 0))],
        out_specs=[row, row],
        out_shape=[jax.ShapeDtypeStruct((nb, l, d), F32)] * 2,
        compiler_params=_cparams(("parallel", "parallel")),
        name="odd_outproj",
    )(x, mix, w, mods, g2)


def _top_rows(s, k, payload=None):
    rows = lax.broadcasted_iota(I32, s.shape, 0).astype(F32)
    n = float(s.shape[0])
    vals, pays = [], []
    for _ in range(k):
        m = jnp.max(s, axis=0, keepdims=True)
        am = jnp.min(jnp.where(s == m, rows, n), axis=0, keepdims=True)
        hit = rows == am
        vals.append(m)
        pays.append(am if payload is None else jnp.max(jnp.where(hit, payload, -1.0), axis=0, keepdims=True))
        s = jnp.where(hit, -jnp.inf, s)
    return jnp.concatenate(vals, axis=0), jnp.concatenate(pays, axis=0)


def _pair_candidates(s1, i1, s2, i2):
    kk = PEER_TOPK
    sub = lax.broadcasted_iota(I32, (SUBLANES, s1.shape[1]), 0)
    cand = [s1[0:1, :] + s2]
    cid = [i1[0:1, :] * PEER_NKEYS + i2]
    for a in range(1, SUBLANES):
        ok = sub < kk // (a + 1)
        cand.append(jnp.where(ok, s1[a:a + 1, :] + s2[0:SUBLANES, :], -jnp.inf))
        cid.append(i1[a:a + 1, :] * PEER_NKEYS + i2[0:SUBLANES, :])
    cand.append(s1[SUBLANES:kk, :] + s2[0:1, :])
    cid.append(i1[SUBLANES:kk, :] * PEER_NKEYS + i2[0:1, :])
    return jnp.concatenate(cand, axis=0), jnp.concatenate(cid, axis=0)


def _compress_rows(sel, skip, arrays):
    n = sel.shape[0]
    cc = jnp.where(sel, skip, 0)
    vals = [jnp.where(sel, a, jnp.zeros_like(a)) for a in arrays]
    for k in range(n.bit_length() - 1):
        mv = ((cc >> k) & 1) == 1

        def step(a):
            zero = jnp.zeros_like(a)
            return jnp.where(mv, zero, a) + pltpu.roll(jnp.where(mv, a, zero), n - (1 << k), 0)

        vals = [step(a) for a in vals]
        cc = step(cc)
    return vals


def _peer_topk_kernel(h_ref, wq_ref, keys_ref, off_ref, gate_ref, ngrp_ref, *, half_rows):
    kk = PEER_TOPK
    q = _dot(h_ref[...].astype(BF16), wq_ref[...])
    eids, gates = [], []
    for hd in range(PEER_HEADS):
        tops = []
        for part in range(2):
            c = 2 * hd + part
            qs = q[:, c * LANES:(c + 1) * LANES].astype(BF16)
            tops.append(_top_rows(_dot_nt(keys_ref[c], qs), kk))
        (s1, i1), (s2, i2) = tops
        cand, cid = _pair_candidates(s1, i1, s2, i2)
        best, eid = _top_rows(cand, kk, cid)
        e = jnp.exp(best - best[0:1, :])
        eids.append(eid)
        gates.append(e / jnp.sum(e, axis=0, keepdims=True))
    eid = jnp.concatenate(eids, axis=0).astype(I32)
    gate = jnp.concatenate(gates, axis=0)
    npick, tm = eid.shape
    top = eid // half_rows
    upper = ((top ^ eid) & 1) == 1
    off = (top * (half_rows // 2) + (eid & (half_rows - 1)) // 2) * SUBLANES
    row = lax.broadcasted_iota(I32, (npick, tm), 0)
    before = (lax.broadcasted_iota(I32, (npick, npick), 1) < lax.broadcasted_iota(I32, (npick, npick), 0))
    upper_before = _dot(before.astype(BF16), upper.astype(BF16)).astype(I32)
    for hf, (sel, skip) in enumerate(((~upper, upper_before), (upper, row - upper_before))):
        off_c, gate_c = _compress_rows(sel, skip, [off, gate])
        off_ref[hf] = off_c.T
        gate_ref[hf] = gate_c.T
        count = jnp.sum(sel.astype(F32), axis=0, keepdims=True)
        ngrp = ((count + (SUBLANES - 1)) * (1.0 / SUBLANES)).astype(I32)
        ngrp_ref[hf] = jnp.broadcast_to(ngrp, (npick, tm)).T


def _flat_tile(nt_all, nt, tile_off):
    return lambda i: (i // nt) * nt_all + (i % nt) + tile_off


def _peer_topk(h2, wq, keys, half_rows, nb, tile_off):
    n, d = h2.shape
    tm = ROW_TILE
    nt_all = n // nb // tm
    nt = nt_all - tile_off
    ft = _flat_tile(nt_all, nt, tile_off)
    npick = PEER_HEADS * PEER_TOPK
    out = pl.BlockSpec((2, tm, npick), lambda i: (0, ft(i), 0))
    return pl.pallas_call(
        functools.partial(_peer_topk_kernel, half_rows=half_rows),
        grid=(nb * nt,),
        in_specs=[pl.BlockSpec((tm, d), lambda i: (ft(i), 0)),
                  pl.BlockSpec(wq.shape, lambda i: (0, 0)),
                  pl.BlockSpec(keys.shape, lambda i: (0, 0, 0))],
        out_specs=[out, out, out],
        out_shape=[jax.ShapeDtypeStruct((2, n, npick), I32), jax.ShapeDtypeStruct((2, n, npick), F32),
                   jax.ShapeDtypeStruct((2, n, npick), I32)],
        compiler_params=_cparams(("parallel",)),
        name="peer_topk",
    )(h2, wq, keys)


def _split_table(tab):
    nexp, d = tab.shape
    t = tab.reshape(2, nexp // 4, 2, d)
    halves = [jnp.stack([t[0, :, h], t[1, :, 1 - h]]) for h in range(2)]
    return jnp.stack(halves).reshape(nexp * d // LANES, LANES)


def _load_table_half(tab_hbm, tab_ref, sem):
    hf = pl.program_id(0)
    half_rows = tab_ref.shape[0]

    @pl.when(pl.program_id(1) == 0)
    def _():
        cp = pltpu.make_async_copy(tab_hbm.at[pl.ds(hf * half_rows, half_rows)], tab_ref, sem)
        cp.start()
        cp.wait()


def _table_row(tab_ref, off):
    return tab_ref[pl.ds(pl.multiple_of(off, SUBLANES), SUBLANES), :]


def _sublane_sum_masks():
    sub = lax.broadcasted_iota(I32, (SUBLANES, LANES), 0)
    return ((sub + 2) % 8 < 4,
            (sub + 1) % 8 < 4,
            sub < 4,
            (sub + 7) % 8 < 4,
            sub % 4 < 2,
            (sub + 3) % 4 < 2,
            sub % 2 == 1)


def _sublane_sums(prods, masks):
    def merge(a, b, m, shift):
        return jnp.where(m, a, b) + pltpu.roll(jnp.where(m, b, a), shift, 0)

    s0 = merge(prods[0], prods[4], masks[0], 4)
    s1 = merge(prods[1], prods[5], masks[1], 4)
    s2 = merge(prods[2], prods[6], masks[2], 4)
    s3 = merge(prods[3], prods[7], masks[3], 4)
    u0 = merge(s0, s2, masks[4], 2)
    u1 = merge(s1, s3, masks[5], 2)
    return merge(u0, u1, masks[6], 1)


def _peer_dot_kernel(iota_ref, off_ref, ngrp_ref, gate_ref, x_ref, tab_hbm, d_ref, tab_ref, q_ref, sem):
    _load_table_half(tab_hbm, tab_ref, sem)
    npick, tb = d_ref.shape
    xr = PEER_XLU_GROUPS * SUBLANES
    lane_lo = lax.broadcasted_iota(I32, (xr, tb), 1)
    lane_hi = lax.broadcasted_iota(I32, (npick - xr, tb), 1)
    masks = _sublane_sum_masks()
    ks = [iota_ref[k] for k in range(SUBLANES)]

    @pl.when(pl.program_id(1) == 0)
    def _():
        def clear(t, carry):
            q_ref[t] = jnp.zeros((npick, LANES), F32)
            return carry
        lax.fori_loop(0, tb, clear, 0)

    def gather(t, carry):
        x = x_ref[t]
        base = pl.multiple_of(t * npick, npick)

        def group(start):
            offs = off_ref.at[pl.ds(base + start, SUBLANES)]
            prods = [_table_row(tab_ref, offs[ks[(k + 1) % SUBLANES]]) * x for k in range(SUBLANES)]
            q_ref[t, pl.ds(start, SUBLANES), :] = _sublane_sums(prods, masks)

        for g in range(PEER_STATIC_GROUPS):
            group(g * SUBLANES)

        def extra(g, c):
            group(pl.multiple_of(g * SUBLANES, SUBLANES))
            return c

        lax.fori_loop(PEER_STATIC_GROUPS, ngrp_ref[base], extra, 0)
        return carry

    def lane_reduce(t, acc):
        col = jnp.sum(q_ref[t, 0:xr, :], axis=1, keepdims=True)
        return jnp.where(lane_lo == t, col, acc)

    def lane_reduce_rest(t, carry):
        @pl.when(ngrp_ref[t * npick] > PEER_XLU_GROUPS)
        def _():
            col = jnp.sum(q_ref[t, xr:npick, :], axis=1, keepdims=True)
            d_ref[xr:npick, :] = jnp.where(lane_hi == t, col, d_ref[xr:npick, :])
        return carry

    lax.fori_loop(0, tb, gather, 0)
    d_ref[0:xr, :] = lax.fori_loop(0, tb, lane_reduce, jnp.zeros((xr, tb), F32), unroll=PEER_XLU_UNROLL)
    d_ref[xr:npick, :] = jnp.zeros((npick - xr, tb), F32)

    @pl.when(jnp.max(gate_ref[:, xr:npick]) > 0.0)
    def _():
        lax.fori_loop(0, tb, lane_reduce_rest, 0)


def _peer_sum_kernel(iota_ref, off_ref, ngrp_ref, gate_ref, d_ref, tab_hbm, f_ref, tab_ref, wl_ref, sem):
    _load_table_half(tab_hbm, tab_ref, sem)
    tb, npick = gate_ref.shape
    xr = PEER_XLU_GROUPS * SUBLANES
    nacc = 4
    ks = [iota_ref[k] for k in range(SUBLANES)]
    gate = gate_ref[...].T
    d = d_ref[...]
    act = 0.5 * d * (1.0 + lax.erf(d * (2.0 ** -0.5)))
    wt = jnp.where(gate > 0.0, gate * act, 0.0)
    lane_lo = lax.broadcasted_iota(I32, (xr, tb), 1)
    lane_hi = lax.broadcasted_iota(I32, (npick - xr, tb), 1)

    def spread(t, carry):
        col = jnp.sum(jnp.where(lane_lo == t, wt[0:xr], 0.0), axis=1, keepdims=True)
        wl_ref[t, 0:xr, :] = jnp.broadcast_to(col, (xr, LANES))
        return carry

    def spread_rest(t, carry):
        @pl.when(ngrp_ref[t * npick] > PEER_XLU_GROUPS)
        def _():
            col = jnp.sum(jnp.where(lane_hi == t, wt[xr:npick], 0.0), axis=1, keepdims=True)
            wl_ref[t, xr:npick, :] = jnp.broadcast_to(col, (npick - xr, LANES))
        return carry

    def token(t, carry):
        base = pl.multiple_of(t * npick, npick)

        def group(start, accs):
            accs = list(accs)
            offs = off_ref.at[pl.ds(base + start, SUBLANES)]
            for k in range(SUBLANES):
                wv = jnp.broadcast_to(wl_ref[t, pl.ds(start + k, 1), :], (SUBLANES, LANES))
                accs[k % nacc] = accs[k % nacc] + wv * _table_row(tab_ref, offs[ks[k]])
            return tuple(accs)

        accs = tuple(jnp.zeros((SUBLANES, LANES), F32) for _ in range(nacc))
        for g in range(PEER_STATIC_GROUPS):
            accs = group(g * SUBLANES, accs)
        accs = lax.fori_loop(PEER_STATIC_GROUPS, ngrp_ref[base],
                             lambda g, a: group(pl.multiple_of(g * SUBLANES, SUBLANES), a), accs)
        f_ref[t] = (accs[0] + accs[1]) + (accs[2] + accs[3])
        return carry

    lax.fori_loop(0, tb, spread, 0, unroll=PEER_XLU_UNROLL)

    @pl.when(jnp.max(gate_ref[:, xr:npick]) > 0.0)
    def _():
        lax.fori_loop(0, tb, spread_rest, 0)

    lax.fori_loop(0, tb, token, 0)


def _peer_tiles(n, nb, tile_off_rows):
    tb = PEER_TOK
    nt_all = n // nb // tb
    off = tile_off_rows // tb
    nt = nt_all - off
    return tb, nb * nt, _flat_tile(nt_all, nt, off)


def _smem_token_block(tb, npick, blocks_per_half, ft):
    return pl.BlockSpec((tb * npick,), lambda hf, i: (hf * blocks_per_half + ft(i),), memory_space=pltpu.SMEM)


def _peer_dot(off_flat, ngrp_flat, gate2, h2v, tab2, nb, tile_off_rows):
    _, n, npick = gate2.shape
    tb, steps, ft = _peer_tiles(n, nb, tile_off_rows)
    half = tab2.shape[0] // 2
    smem_blk = _smem_token_block(tb, npick, n // tb, ft)
    return pl.pallas_call(
        _peer_dot_kernel,
        grid=(2, steps),
        in_specs=[pl.BlockSpec(memory_space=pltpu.SMEM), smem_blk, smem_blk,
                  pl.BlockSpec((None, tb, npick), lambda hf, i: (hf, ft(i), 0)),
                  pl.BlockSpec((tb, SUBLANES, LANES), lambda hf, i: (ft(i), 0, 0)),
                  pl.BlockSpec(memory_space=pl.ANY)],
        out_specs=pl.BlockSpec((None, npick, tb), lambda hf, i: (hf, 0, ft(i))),
        out_shape=jax.ShapeDtypeStruct((2, npick, n), F32),
        scratch_shapes=[pltpu.VMEM((half, LANES), F32),
                        pltpu.VMEM((tb, npick, LANES), F32),
                        pltpu.SemaphoreType.DMA],
        compiler_params=_cparams(("arbitrary", "arbitrary"), VMEM_LIMIT_TABLE),
        name="peer_dot",
    )(jnp.arange(SUBLANES, dtype=I32), off_flat, ngrp_flat, gate2, h2v, tab2)


def _peer_sum(off_flat, ngrp_flat, gate2, dpart, tab2, nb, tile_off_rows):
    _, n, npick = gate2.shape
    tb, steps, ft = _peer_tiles(n, nb, tile_off_rows)
    half = tab2.shape[0] // 2
    smem_blk = _smem_token_block(tb, npick, n // tb, ft)
    return pl.pallas_call(
        _peer_sum_kernel,
        grid=(2, steps),
        in_specs=[pl.BlockSpec(memory_space=pltpu.SMEM), smem_blk, smem_blk,
                  pl.BlockSpec((None, tb, npick), lambda hf, i: (hf, ft(i), 0)),
                  pl.BlockSpec((None, npick, tb), lambda hf, i: (hf, 0, ft(i))),
                  pl.BlockSpec(memory_space=pl.ANY)],
        out_specs=pl.BlockSpec((None, tb, SUBLANES, LANES), lambda hf, i: (hf, ft(i), 0, 0)),
        out_shape=jax.ShapeDtypeStruct((2, n, SUBLANES, LANES), F32),
        scratch_shapes=[pltpu.VMEM((half, LANES), F32),
                        pltpu.VMEM((tb, npick, LANES), F32),
                        pltpu.SemaphoreType.DMA],
        compiler_params=_cparams(("arbitrary", "arbitrary"), VMEM_LIMIT_TABLE),
        name="peer_sum",
    )(jnp.arange(SUBLANES, dtype=I32), off_flat, ngrp_flat, gate2, dpart, tab2)


def _residual_kernel(x_ref, f_ref, mod_ref, o_ref):
    o_ref[...] = x_ref[...] + mod_ref[5:6, :] * (f_ref[0] + f_ref[1])


def _final_kernel(x_ref, f_ref, mod_ref, g_ref, o_ref):
    o_ref[...] = _rms(x_ref[...] + mod_ref[5:6, :] * (f_ref[0] + f_ref[1])) * g_ref[...]


def _residual(x1, f2, mods):
    nb, l, d = x1.shape
    tm = ROW_TILE
    return pl.pallas_call(
        _residual_kernel,
        grid=(nb, l // tm),
        in_specs=[pl.BlockSpec((None, tm, d), lambda b, t: (b, t, 0)),
                  pl.BlockSpec((2, None, tm, d), lambda b, t: (0, b, t, 0)),
                  pl.BlockSpec((None, 6, d), _mod_index(nb))],
        out_specs=pl.BlockSpec((None, tm, d), lambda b, t: (b, t, 0)),
        out_shape=jax.ShapeDtypeStruct((nb, l, d), F32),
        compiler_params=_cparams(("parallel", "parallel")),
        name="peer_residual",
    )(x1, f2, mods)


def _final(x1, f2, mods, g, tile_off):
    nb, l, d = x1.shape
    tm = ROW_TILE
    nt = l // tm - tile_off
    return pl.pallas_call(
        _final_kernel,
        grid=(nb, nt),
        in_specs=[pl.BlockSpec((None, tm, d), lambda b, t: (b, t + tile_off, 0)),
                  pl.BlockSpec((2, None, tm, d), lambda b, t: (0, b, t + tile_off, 0)),
                  pl.BlockSpec((None, 6, d), lambda b, t: (b, 0, 0)),
                  pl.BlockSpec((1, d), lambda b, t: (0, 0))],
        out_specs=pl.BlockSpec((None, tm, d), lambda b, t: (b, t, 0)),
        out_shape=jax.ShapeDtypeStruct((nb, nt * tm, d), F32),
        compiler_params=_cparams(("parallel", "parallel")),
        name="final_norm",
    )(x1, f2, mods, g)


def _rope_tables(seq, ctx_len):
    t = jnp.arange(seq)
    row_id = (t // GRID_W).astype(F32)
    col_id = (t % GRID_W).astype(F32)
    axis_dim = HEAD_DIM // 2
    inv = ROPE_THETA ** (-jnp.arange(0, axis_dim, 2, dtype=F32) / axis_dim)
    ang_r = row_id[:, None] * inv[None, :]
    ang_c = col_id[:, None] * inv[None, :]
    cos = jnp.concatenate([jnp.cos(ang_r)] * 2 + [jnp.cos(ang_c)] * 2, axis=-1)
    sin = jnp.concatenate([-jnp.sin(ang_r), jnp.sin(ang_r), -jnp.sin(ang_c), jnp.sin(ang_c)], axis=-1)
    cos = jnp.concatenate([jnp.ones((ctx_len, HEAD_DIM), F32), cos], axis=0)
    sin = jnp.concatenate([jnp.zeros((ctx_len, HEAD_DIM), F32), sin], axis=0)
    return jnp.tile(cos, (1, 2)), jnp.tile(sin, (1, 2))


def _peer(h2, x1, mods, wq, keys, u_tab, v_tab, nb, tile_off_rows, final_g):
    b, l, d = x1.shape
    n = b * l
    nexp = u_tab.shape[0]
    npick = PEER_HEADS * PEER_TOPK
    off2, gate2, ngrp2 = _peer_topk(h2.reshape(n, d), wq.astype(BF16),
                                    keys.reshape(PEER_HEADS * 2, PEER_NKEYS, -1).astype(BF16),
                                    nexp // 2, nb, tile_off_rows // ROW_TILE)
    off_flat = off2.reshape(2 * n * npick)
    ngrp_flat = ngrp2.reshape(2 * n * npick)
    h2v = h2.reshape(n, SUBLANES, LANES)
    dpart = _peer_dot(off_flat, ngrp_flat, gate2, h2v, _split_table(u_tab), nb, tile_off_rows)
    f2 = _peer_sum(off_flat, ngrp_flat, gate2, dpart, _split_table(v_tab), nb, tile_off_rows)
    f2 = f2.reshape(2, b, l, d)
    if final_g is None:
        return _residual(x1, f2, mods)
    return _final(x1, f2, mods, final_g.reshape(1, d), tile_off_rows // ROW_TILE)


def kernel(x, c, ctx, c_ctx, ada_w, ada_b, norm1_g, norm2_g, ev_w_in, ev_w_out, gqa_q_norm_g,
           gqa_k_norm_g, ret_log_rate, od_w_in, od_w_out, diff_lambda, diff_subln_g, peer_w_q,
           peer_keys, peer_u, peer_v, final_g):
    nb, seq, d = x.shape
    ctx_len = ctx.shape[1]
    depth = ada_w.shape[0]
    assert ctx_len == ROW_TILE and seq % ROW_TILE == 0 and d == SUBLANES * LANES
    xs = jnp.concatenate([ctx, x], axis=1)
    cos, sin = _rope_tables(seq, ctx_len)
    cc = jnp.concatenate([c, c_ctx[None, :]], axis=0)
    for layer in range(depth):
        last = layer == depth - 1
        mods = _ada(cc, ada_w[layer], ada_b[layer]).reshape(nb + 1, 6, d)
        g1 = norm1_g[layer].reshape(1, d)
        g2 = norm2_g[layer].reshape(1, d)
        lat_off = 1 if last else 0
        if layer % 2 == 0:
            e = layer // 2
            qa, ka, va, qb, kb, vb, gb = _even_inproj(
                xs, g1, mods, ev_w_in[e].astype(BF16), cos, sin,
                jnp.tile(gqa_q_norm_g[e], 2).reshape(1, LANES), jnp.tile(gqa_k_norm_g[e], 2).reshape(1, LANES))
            ya = _gqa(qa, ka, va, ctx_len, 0)
            o_f, o_r = _retention(ret_log_rate[e], qb, kb, vb, ctx_len)
            x1, h2 = _even_outproj(xs, ya, o_f, o_r, gb, ev_w_out[e].astype(BF16), mods, g2)
        else:
            o = layer // 2
            lam_init = 0.8 - 0.6 * math.exp(-0.3 * layer)
            q, k, v = _odd_inproj(xs, g1, mods, od_w_in[o].astype(BF16), cos, sin)
            mix = _diff_attention(q, k, v, diff_lambda[o], diff_subln_g[o].reshape(1, LANES),
                                  ctx_len, lat_off, lam_init)
            x1, h2 = _odd_outproj(xs, mix, od_w_out[o].astype(BF16), mods, g2, lat_off)
        xs = _peer(h2, x1, mods, peer_w_q[layer], peer_keys[layer], peer_u[layer], peer_v[layer],
                   nb, lat_off * ROW_TILE, final_g if last else None)
    return xs
```

```python
import functools
import math

import jax
import jax.numpy as jnp
from jax import lax
from jax.experimental import pallas as pl
from jax.experimental.pallas import tpu as pltpu

F32 = jnp.float32
BF16 = jnp.bfloat16
I32 = jnp.int32

LANES = 128
SUBLANES = 8
HEAD_DIM = 64
GRID_W = 64
ROPE_THETA = 10000.0
NORM_EPS = 1e-6
RET_CHUNK = 128
PEER_NKEYS = 128
PEER_TOPK = 16
PEER_HEADS = 8
ROW_TILE = 256
PEER_TOK = 128
PEER_STATIC_GROUPS = 9
PEER_XLU_GROUPS = 11
PEER_XLU_UNROLL = 32
KV_CHUNK = 256
VMEM_LIMIT = 48 * 1024 * 1024
VMEM_LIMIT_TABLE = 56 * 1024 * 1024


def _cparams(sem, limit=VMEM_LIMIT):
    return pltpu.CompilerParams(dimension_semantics=sem, vmem_limit_bytes=limit)


def _rms(x):
    return x * lax.rsqrt(jnp.mean(x * x, axis=-1, keepdims=True) + NORM_EPS)


def _dot(a, b):
    return jnp.dot(a, b, preferred_element_type=F32)


def _dot_nt(a, b):
    return lax.dot_general(a, b, (((1,), (1,)), ((), ())), preferred_element_type=F32)


def _sigmoid(x):
    return 1.0 / (1.0 + jnp.exp(-x))


def _ada_kernel(c_ref, w_ref, b_ref, o_ref):
    c = c_ref[...]
    a = (c * _sigmoid(c)).astype(BF16)
    o_ref[...] = _dot(a, w_ref[...].astype(BF16)) + b_ref[...]


def _ada(cc, w, b):
    m, d = cc.shape
    n = w.shape[1]
    tn = 1024
    return pl.pallas_call(
        _ada_kernel,
        grid=(n // tn,),
        in_specs=[pl.BlockSpec((m, d), lambda j: (0, 0)),
                  pl.BlockSpec((d, tn), lambda j: (0, j)),
                  pl.BlockSpec((1, tn), lambda j: (0, j))],
        out_specs=pl.BlockSpec((m, tn), lambda j: (0, j)),
        out_shape=jax.ShapeDtypeStruct((m, n), F32),
        compiler_params=_cparams(("arbitrary",)),
        name="ada_mod",
    )(cc, w, b.reshape(1, n))


def _rope_cols(x, cos, sin_signed):
    lane = lax.broadcasted_iota(I32, x.shape, 1)
    partner = jnp.where((lane % 32) < 16, pltpu.roll(x, LANES - 16, 1), pltpu.roll(x, 16, 1))
    return x * cos + partner * sin_signed


def _head_mean_sq(x):
    r = lax.broadcasted_iota(I32, (LANES, LANES), 0) // HEAD_DIM
    c = lax.broadcasted_iota(I32, (LANES, LANES), 1) // HEAD_DIM
    ones_bd = jnp.where(r == c, 1.0, 0.0).astype(BF16)
    sq = x * x
    hi = sq.astype(BF16)
    lo = (sq - hi.astype(F32)).astype(BF16)
    return (_dot(hi, ones_bd) + _dot(lo, ones_bd)) * (1.0 / HEAD_DIM)


def _prologue(x_ref, g_ref, mod_ref, shift_row, scale_row):
    h = _rms(x_ref[...]) * g_ref[...]
    return h * (1.0 + mod_ref[scale_row:scale_row + 1, :]) + mod_ref[shift_row:shift_row + 1, :]


def _even_inproj_kernel(x_ref, g_ref, mod_ref, w_ref, cos_ref, sin_ref, qg_ref, kg_ref,
                        qa_ref, ka_ref, va_ref, qb_ref, kb_ref, vb_ref, gb_ref):
    h = _prologue(x_ref, g_ref, mod_ref, 0, 1).astype(BF16)
    y = _dot(h, w_ref[...])
    cos = cos_ref[...]
    sin = sin_ref[...]
    scale = HEAD_DIM ** -0.5

    def col(c):
        return y[:, c * LANES:(c + 1) * LANES]

    for c in range(4):
        x = col(c)
        x = x * lax.rsqrt(_head_mean_sq(x) + NORM_EPS) * qg_ref[...]
        qa_ref[:, c * LANES:(c + 1) * LANES] = (_rope_cols(x, cos, sin) * scale).astype(BF16)
    x = col(4)
    x = x * lax.rsqrt(_head_mean_sq(x) + NORM_EPS) * kg_ref[...]
    ka_ref[...] = _rope_cols(x, cos, sin).astype(BF16)
    va_ref[...] = col(5).astype(BF16)
    for c in range(2):
        qb_ref[:, c * LANES:(c + 1) * LANES] = _rope_cols(col(6 + c), cos, sin).astype(BF16)
        kb_ref[:, c * LANES:(c + 1) * LANES] = _rope_cols(col(8 + c) * scale, cos, sin).astype(BF16)
    vb_ref[...] = y[:, 10 * LANES:14 * LANES].astype(BF16)
    gb_ref[...] = y[:, 14 * LANES:18 * LANES]


def _mod_index(nb):
    return lambda b, t: (jnp.where(t == 0, nb, b), 0, 0)


def _even_inproj(x, g, mods, w, cos, sin, qg, kg):
    nb, l, d = x.shape
    tm = ROW_TILE
    n_in = w.shape[1]
    row = lambda width: pl.BlockSpec((None, tm, width), lambda b, t: (b, t, 0))
    widths = (512, 128, 128, 256, 256, 512, 512)
    dts = (BF16, BF16, BF16, BF16, BF16, BF16, F32)
    return pl.pallas_call(
        _even_inproj_kernel,
        grid=(nb, l // tm),
        in_specs=[row(d),
                  pl.BlockSpec((1, d), lambda b, t: (0, 0)),
                  pl.BlockSpec((None, 6, d), _mod_index(nb)),
                  pl.BlockSpec((d, n_in), lambda b, t: (0, 0)),
                  pl.BlockSpec((tm, LANES), lambda b, t: (t, 0)),
                  pl.BlockSpec((tm, LANES), lambda b, t: (t, 0)),
                  pl.BlockSpec((1, LANES), lambda b, t: (0, 0)),
                  pl.BlockSpec((1, LANES), lambda b, t: (0, 0))],
        out_specs=[row(wd) for wd in widths],
        out_shape=[jax.ShapeDtypeStruct((nb, l, wd), dt) for wd, dt in zip(widths, dts)],
        compiler_params=_cparams(("parallel", "parallel")),
        name="even_inproj",
    )(x, g, mods, w, cos, sin, qg, kg)


def _odd_inproj_kernel(x_ref, g_ref, mod_ref, w_ref, cos_ref, sin_ref, q_ref, k_ref, v_ref):
    h = _prologue(x_ref, g_ref, mod_ref, 0, 1).astype(BF16)
    y = _dot(h, w_ref[...])
    cos = cos_ref[...]
    sin = sin_ref[...]
    scale = HEAD_DIM ** -0.5
    for c in range(8):
        sl = slice(c * LANES, (c + 1) * LANES)
        q_ref[:, sl] = (_rope_cols(y[:, sl], cos, sin) * scale).astype(BF16)
        k_ref[:, sl] = _rope_cols(y[:, 8 * LANES + c * LANES:8 * LANES + (c + 1) * LANES], cos, sin).astype(BF16)
    v_ref[...] = y[:, 16 * LANES:24 * LANES].astype(BF16)


def _odd_inproj(x, g, mods, w, cos, sin):
    nb, l, d = x.shape
    tm = ROW_TILE
    n_in = w.shape[1]
    row = lambda width: pl.BlockSpec((None, tm, width), lambda b, t: (b, t, 0))
    return pl.pallas_call(
        _odd_inproj_kernel,
        grid=(nb, l // tm),
        in_specs=[row(d),
                  pl.BlockSpec((1, d), lambda b, t: (0, 0)),
                  pl.BlockSpec((None, 6, d), _mod_index(nb)),
                  pl.BlockSpec((d, n_in), lambda b, t: (0, 0)),
                  pl.BlockSpec((tm, LANES), lambda b, t: (t, 0)),
                  pl.BlockSpec((tm, LANES), lambda b, t: (t, 0))],
        out_specs=[row(d), row(d), row(d)],
        out_shape=[jax.ShapeDtypeStruct((nb, l, d), BF16)] * 3,
        compiler_params=_cparams(("parallel", "parallel")),
        name="odd_inproj",
    )(x, g, mods, w, cos, sin)


def _scores(q, k_ref, nchunks):
    kc = KV_CHUNK
    scores, run = [], None
    for c in range(nchunks):
        s = _dot_nt(q, k_ref[c * kc:(c + 1) * kc, :])
        scores.append(s)
        for j in range(kc // LANES):
            col = s[:, j * LANES:(j + 1) * LANES]
            run = col if run is None else jnp.maximum(run, col)
    return scores, jnp.max(run, axis=-1, keepdims=True)


def _softmax_pv(scores_max, v_ref, next_q=None, k_ref=None):
    scores, m = scores_max
    kc = KV_CHUNK
    acc = jnp.zeros((scores[0].shape[0], v_ref.shape[1]), F32)
    den = jnp.zeros((scores[0].shape[0], LANES), F32)
    nxt, run = [], None
    for c, s in enumerate(scores):
        if next_q is not None:
            sn = _dot_nt(next_q, k_ref[c * kc:(c + 1) * kc, :])
            nxt.append(sn)
            for j in range(kc // LANES):
                col = sn[:, j * LANES:(j + 1) * LANES]
                run = col if run is None else jnp.maximum(run, col)
        e = jnp.exp(s - m)
        for j in range(kc // LANES):
            den = den + e[:, j * LANES:(j + 1) * LANES]
        acc = acc + _dot(e.astype(BF16), v_ref[c * kc:(c + 1) * kc, :])
    den = jnp.sum(den, axis=-1, keepdims=True)
    if next_q is None:
        return acc, den
    return acc, den, (nxt, jnp.max(run, axis=-1, keepdims=True))


def _key_chunk_branches(qi, ctx_len, total, body):
    @pl.when(qi == 0)
    def _():
        body(ctx_len // KV_CHUNK)

    @pl.when(qi != 0)
    def _():
        body(total // KV_CHUNK)


def _gqa_kernel(q_ref, k_ref, v_ref, o_ref, *, ctx_len, tile_off):
    g = pl.program_id(1)
    qi = pl.program_id(2) + tile_off
    tq = q_ref.shape[0]
    half = lax.broadcasted_iota(I32, (tq, LANES), 1) // HEAD_DIM

    def head_q(i):
        qh = q_ref[:, (i // 2) * LANES:(i // 2 + 1) * LANES]
        qsel = jnp.where(half == (i % 2), qh, jnp.zeros_like(qh)).astype(F32)
        return jnp.where(g == (i % 2), qsel, pltpu.roll(qsel, HEAD_DIM, 1)).astype(BF16)

    def body(nchunks):
        cols = [jnp.zeros((tq, LANES), F32), jnp.zeros((tq, LANES), F32)]
        cur = _scores(head_q(0), k_ref, nchunks)
        for i in range(4):
            if i + 1 < 4:
                acc, den, cur = _softmax_pv(cur, v_ref, head_q(i + 1), k_ref)
            else:
                acc, den = _softmax_pv(cur, v_ref)
            o = acc / den
            osel = jnp.where(half == g, o, 0.0)
            cols[i // 2] = cols[i // 2] + jnp.where(g == (i % 2), osel, pltpu.roll(osel, HEAD_DIM, 1))
        o_ref[:, 0:LANES] = cols[0].astype(BF16)
        o_ref[:, LANES:2 * LANES] = cols[1].astype(BF16)

    if tile_off == 0:
        _key_chunk_branches(qi, ctx_len, k_ref.shape[0], body)
    else:
        body(k_ref.shape[0] // KV_CHUNK)


def _gqa(q, k, v, ctx_len, tile_off):
    nb, l, _ = q.shape
    tq = ROW_TILE
    nt = l // tq - tile_off
    return pl.pallas_call(
        functools.partial(_gqa_kernel, ctx_len=ctx_len, tile_off=tile_off),
        grid=(nb, 2, nt),
        in_specs=[pl.BlockSpec((None, tq, 2 * LANES), lambda b, g, t: (b, t + tile_off, g)),
                  pl.BlockSpec((None, l, LANES), lambda b, g, t: (b, 0, 0)),
                  pl.BlockSpec((None, l, LANES), lambda b, g, t: (b, 0, 0))],
        out_specs=pl.BlockSpec((None, tq, 2 * LANES), lambda b, g, t: (b, t + tile_off, g)),
        out_shape=jax.ShapeDtypeStruct((nb, l, 4 * LANES), BF16),
        compiler_params=_cparams(("parallel", "parallel", "parallel")),
        name="gqa_attention",
    )(q, k, v)


def _diff_kernel(q_ref, k_ref, v_ref, lam_ref, g_ref, o_ref, *, ctx_len, tile_off, lam_init):
    qi = pl.program_id(2) + tile_off
    tq = q_ref.shape[0]
    lf = lam_ref[...]
    lam = (jnp.exp(jnp.sum(lf[0:1, :] * lf[1:2, :], axis=-1, keepdims=True))
           - jnp.exp(jnp.sum(lf[2:3, :] * lf[3:4, :], axis=-1, keepdims=True)) + lam_init)
    half = lax.broadcasted_iota(I32, (tq, LANES), 1) // HEAD_DIM

    def body(nchunks):
        q = q_ref[...]
        zero = jnp.zeros_like(q)
        s1 = _scores(jnp.where(half == 0, q, zero), k_ref, nchunks)
        a1, d1, s2 = _softmax_pv(s1, v_ref, jnp.where(half == 1, q, zero), k_ref)
        a2, d2 = _softmax_pv(s2, v_ref)
        o = a1 / d1 - lam * (a2 / d2)
        o_ref[...] = (_rms(o) * g_ref[...] * (1.0 - lam_init)).astype(BF16)

    if tile_off == 0:
        _key_chunk_branches(qi, ctx_len, k_ref.shape[0], body)
    else:
        body(k_ref.shape[0] // KV_CHUNK)


def _diff_attention(q, k, v, lam_p, subln_g, ctx_len, tile_off, lam_init):
    nb, l, d = q.shape
    nh = d // LANES
    tq = ROW_TILE
    nt = l // tq - tile_off
    return pl.pallas_call(
        functools.partial(_diff_kernel, ctx_len=ctx_len, tile_off=tile_off, lam_init=lam_init),
        grid=(nb, nh, nt),
        in_specs=[pl.BlockSpec((None, tq, LANES), lambda b, h, t: (b, t + tile_off, h)),
                  pl.BlockSpec((None, l, LANES), lambda b, h, t: (b, 0, h)),
                  pl.BlockSpec((None, l, LANES), lambda b, h, t: (b, 0, h)),
                  pl.BlockSpec((4, HEAD_DIM), lambda b, h, t: (0, 0)),
                  pl.BlockSpec((1, LANES), lambda b, h, t: (0, 0))],
        out_specs=pl.BlockSpec((None, tq, LANES), lambda b, h, t: (b, t + tile_off, h)),
        out_shape=jax.ShapeDtypeStruct((nb, l, d), BF16),
        compiler_params=_cparams(("parallel", "parallel", "parallel")),
        name="diff_attention",
    )(q, k, v, lam_p, subln_g)


def _retention_kernel(rate_ref, qf_ref, kf_ref, vf_ref, qr_ref, kr_ref, vr_ref,
                      of_ref, or_ref, sf_ref, sr_ref):
    h = pl.program_id(1)
    n = pl.program_id(2)
    c = RET_CHUNK

    @pl.when(n == 0)
    def _():
        sf_ref[...] = jnp.zeros_like(sf_ref)
        sr_ref[...] = jnp.zeros_like(sr_ref)

    head_ok = (lax.broadcasted_iota(I32, (c, LANES), 1) // HEAD_DIM) == (h % 2)
    ri = lax.broadcasted_iota(I32, (c, c), 0)
    ci = lax.broadcasted_iota(I32, (c, c), 1)
    pos = lax.broadcasted_iota(I32, (c, 1), 0).astype(F32)

    def one(direction, q_ref, k_ref, v_ref, o_ref, s_ref):
        log_g = -jnp.exp(jnp.full((1, 1), rate_ref[direction, h], F32))
        rel = ((ri - ci) if direction == 0 else (ci - ri)).astype(F32)
        dmask = jnp.where(rel >= 0, jnp.exp(log_g * jnp.maximum(rel, 0.0)), 0.0)
        q = jnp.where(head_ok, q_ref[...], jnp.zeros((c, LANES), BF16))
        k = jnp.where(head_ok, k_ref[...], jnp.zeros((c, LANES), BF16))
        v = v_ref[...]
        sc = _dot_nt(q, k) * dmask
        intra = _dot(sc.astype(BF16), v)
        q_pow = (pos + 1.0) if direction == 0 else (c - pos)
        k_pow = (c - 1.0 - pos) if direction == 0 else pos
        state = s_ref[...]
        cross = _dot((q.astype(F32) * jnp.exp(log_g * q_pow)).astype(BF16), state.astype(BF16))
        o_ref[...] = intra + cross
        kd = (k.astype(F32) * jnp.exp(log_g * k_pow)).T.astype(BF16)
        s_ref[...] = jnp.exp(log_g * c) * state + _dot(kd, v)

    one(0, qf_ref, kf_ref, vf_ref, of_ref, sf_ref)
    one(1, qr_ref, kr_ref, vr_ref, or_ref, sr_ref)


def _retention(rates, q, k, v, ctx_len):
    nb, l, dv_all = v.shape
    c = RET_CHUNK
    nh = dv_all // LANES
    nctx = ctx_len // c
    ntot = l // c

    def rev(n):
        return jnp.where(n < nctx, nctx - 1 - n, ntot - 1 - (n - nctx))

    qk_f = pl.BlockSpec((None, c, LANES), lambda b, h, n: (b, n, h // 2))
    qk_r = pl.BlockSpec((None, c, LANES), lambda b, h, n: (b, rev(n), h // 2))
    v_f = pl.BlockSpec((None, c, LANES), lambda b, h, n: (b, n, h))
    v_r = pl.BlockSpec((None, c, LANES), lambda b, h, n: (b, rev(n), h))
    return pl.pallas_call(
        _retention_kernel,
        grid=(nb, nh, ntot),
        in_specs=[pl.BlockSpec(memory_space=pltpu.SMEM), qk_f, qk_f, v_f, qk_r, qk_r, v_r],
        out_specs=[v_f, v_r],
        out_shape=[jax.ShapeDtypeStruct((nb, l, dv_all), F32)] * 2,
        scratch_shapes=[pltpu.VMEM((LANES, LANES), F32), pltpu.VMEM((LANES, LANES), F32)],
        compiler_params=_cparams(("parallel", "parallel", "arbitrary")),
        name="retention",
    )(rates, q, k, v, q, k, v)


def _finish_outproj(y, x_ref, mod_ref, g2_ref, x1_ref, h2_ref, h2t_ref):
    x1 = x_ref[...] + mod_ref[2:3, :] * y
    x1_ref[...] = x1
    h2 = (_rms(x1) * g2_ref[...]) * (1.0 + mod_ref[4:5, :]) + mod_ref[3:4, :]
    h2_ref[...] = h2
    h2t_ref[...] = h2.reshape(h2.shape[0], SUBLANES, LANES)


def _even_outproj_kernel(x_ref, ya_ref, of_ref, or_ref, gb_ref, w_ref, mod_ref, g2_ref, x1_ref, h2_ref, h2t_ref):
    half = ya_ref.shape[1]
    y = _dot(ya_ref[...], w_ref[0:half, :])
    for c in range(half // LANES):
        sl = slice(c * LANES, (c + 1) * LANES)
        gate = gb_ref[:, sl]
        r = _rms(of_ref[:, sl] + or_ref[:, sl]) * (gate * _sigmoid(gate))
        y = y + _dot(r.astype(BF16), w_ref[half + c * LANES:half + (c + 1) * LANES, :])
    _finish_outproj(y, x_ref, mod_ref, g2_ref, x1_ref, h2_ref, h2t_ref)


def _outproj_outputs(nb, l, d, tm, tile_off):
    row = pl.BlockSpec((None, tm, d), lambda b, t: (b, t + tile_off, 0))
    tiles = pl.BlockSpec((None, tm, SUBLANES, LANES), lambda b, t: (b, t + tile_off, 0, 0))
    shapes = [jax.ShapeDtypeStruct((nb, l, d), F32)] * 2 + [jax.ShapeDtypeStruct((nb, l, SUBLANES, LANES), F32)]
    return [row, row, tiles], shapes


def _even_outproj(x, ya, o_f, o_r, gb, w, mods, g2):
    nb, l, d = x.shape
    tm = ROW_TILE
    out_specs, out_shape = _outproj_outputs(nb, l, d, tm, 0)
    row = lambda width: pl.BlockSpec((None, tm, width), lambda b, t: (b, t, 0))
    return pl.pallas_call(
        _even_outproj_kernel,
        grid=(nb, l // tm),
        in_specs=[row(d), row(512), row(512), row(512), row(512),
                  pl.BlockSpec((d, d), lambda b, t: (0, 0)),
                  pl.BlockSpec((None, 6, d), _mod_index(nb)),
                  pl.BlockSpec((1, d), lambda b, t: (0, 0))],
        out_specs=out_specs,
        out_shape=out_shape,
        compiler_params=_cparams(("parallel", "parallel")),
        name="even_outproj",
    )(x, ya, o_f, o_r, gb, w, mods, g2)


def _odd_outproj_kernel(x_ref, mix_ref, w_ref, mod_ref, g2_ref, x1_ref, h2_ref, h2t_ref):
    _finish_outproj(_dot(mix_ref[...], w_ref[...]), x_ref, mod_ref, g2_ref, x1_ref, h2_ref, h2t_ref)


def _odd_outproj(x, mix, w, mods, g2, tile_off):
    nb, l, d = x.shape
    tm = ROW_TILE
    out_specs, out_shape = _outproj_outputs(nb, l, d, tm, tile_off)
    row = pl.BlockSpec((None, tm, d), lambda b, t: (b, t + tile_off, 0))
    return pl.pallas_call(
        _odd_outproj_kernel,
        grid=(nb, l // tm - tile_off),
        in_specs=[row, row,
                  pl.BlockSpec((d, d), lambda b, t: (0, 0)),
                  pl.BlockSpec((None, 6, d), lambda b, t: (b, 0, 0)),
                  pl.BlockSpec((1, d), lambda b, t: (0, 0))],
        out_specs=out_specs,
        out_shape=out_shape,
        compiler_params=_cparams(("parallel", "parallel")),
        name="odd_outproj",
    )(x, mix, w, mods, g2)


def _top_rows(s, k, payload=None):
    rows = lax.broadcasted_iota(I32, s.shape, 0).astype(F32)
    n = float(s.shape[0])
    vals, pays = [], []
    for _ in range(k):
        m = jnp.max(s, axis=0, keepdims=True)
        am = jnp.min(jnp.where(s == m, rows, n), axis=0, keepdims=True)
        hit = rows == am
        vals.append(m)
        pays.append(am if payload is None else jnp.max(jnp.where(hit, payload, -1.0), axis=0, keepdims=True))
        s = jnp.where(hit, -jnp.inf, s)
    return jnp.concatenate(vals, axis=0), jnp.concatenate(pays, axis=0)


def _pair_candidates(s1, i1, s2, i2):
    kk = PEER_TOPK
    sub = lax.broadcasted_iota(I32, (SUBLANES, s1.shape[1]), 0)
    cand = [s1[0:1, :] + s2]
    cid = [i1[0:1, :] * PEER_NKEYS + i2]
    for a in range(1, SUBLANES):
        ok = sub < kk // (a + 1)
        cand.append(jnp.where(ok, s1[a:a + 1, :] + s2[0:SUBLANES, :], -jnp.inf))
        cid.append(i1[a:a + 1, :] * PEER_NKEYS + i2[0:SUBLANES, :])
    cand.append(s1[SUBLANES:kk, :] + s2[0:1, :])
    cid.append(i1[SUBLANES:kk, :] * PEER_NKEYS + i2[0:1, :])
    return jnp.concatenate(cand, axis=0), jnp.concatenate(cid, axis=0)


def _compress_rows(sel, skip, arrays):
    n = sel.shape[0]
    cc = jnp.where(sel, skip, 0)
    vals = [jnp.where(sel, a, jnp.zeros_like(a)) for a in arrays]
    for k in range(n.bit_length() - 1):
        mv = ((cc >> k) & 1) == 1

        def step(a):
            zero = jnp.zeros_like(a)
            return jnp.where(mv, zero, a) + pltpu.roll(jnp.where(mv, a, zero), n - (1 << k), 0)

        vals = [step(a) for a in vals]
        cc = step(cc)
    return vals


def _peer_topk_kernel(h_ref, wq_ref, keys_ref, off_ref, gate_ref, ngrp_ref, *, half_rows):
    kk = PEER_TOPK
    q = _dot(h_ref[...].astype(BF16), wq_ref[...])
    eids, gates = [], []
    for hd in range(PEER_HEADS):
        tops = []
        for part in range(2):
            c = 2 * hd + part
            qs = q[:, c * LANES:(c + 1) * LANES].astype(BF16)
            tops.append(_top_rows(_dot_nt(keys_ref[c], qs), kk))
        (s1, i1), (s2, i2) = tops
        cand, cid = _pair_candidates(s1, i1, s2, i2)
        best, eid = _top_rows(cand, kk, cid)
        e = jnp.exp(best - best[0:1, :])
        eids.append(eid)
        gates.append(e / jnp.sum(e, axis=0, keepdims=True))
    eid = jnp.concatenate(eids, axis=0).astype(I32)
    gate = jnp.concatenate(gates, axis=0)
    npick, tm = eid.shape
    top = eid // half_rows
    upper = ((top ^ eid) & 1) == 1
    off = (top * (half_rows // 2) + (eid & (half_rows - 1)) // 2) * SUBLANES
    row = lax.broadcasted_iota(I32, (npick, tm), 0)
    before = (lax.broadcasted_iota(I32, (npick, npick), 1) < lax.broadcasted_iota(I32, (npick, npick), 0))
    upper_before = _dot(before.astype(BF16), upper.astype(BF16)).astype(I32)
    for hf, (sel, skip) in enumerate(((~upper, upper_before), (upper, row - upper_before))):
        off_c, gate_c = _compress_rows(sel, skip, [off, gate])
        off_ref[hf] = off_c.T
        gate_ref[hf] = gate_c.T
        count = jnp.sum(sel.astype(F32), axis=0, keepdims=True)
        ngrp = ((count + (SUBLANES - 1)) * (1.0 / SUBLANES)).astype(I32)
        ngrp_ref[hf] = jnp.broadcast_to(ngrp, (npick, tm)).T


def _flat_tile(nt_all, nt, tile_off):
    return lambda i: (i // nt) * nt_all + (i % nt) + tile_off


def _peer_topk(h2, wq, keys, half_rows, nb, tile_off):
    n, d = h2.shape
    tm = ROW_TILE
    nt_all = n // nb // tm
    nt = nt_all - tile_off
    ft = _flat_tile(nt_all, nt, tile_off)
    npick = PEER_HEADS * PEER_TOPK
    out = pl.BlockSpec((2, tm, npick), lambda i: (0, ft(i), 0))
    return pl.pallas_call(
        functools.partial(_peer_topk_kernel, half_rows=half_rows),
        grid=(nb * nt,),
        in_specs=[pl.BlockSpec((tm, d), lambda i: (ft(i), 0)),
                  pl.BlockSpec(wq.shape, lambda i: (0, 0)),
                  pl.BlockSpec(keys.shape, lambda i: (0, 0, 0))],
        out_specs=[out, out, out],
        out_shape=[jax.ShapeDtypeStruct((2, n, npick), I32), jax.ShapeDtypeStruct((2, n, npick), F32),
                   jax.ShapeDtypeStruct((2, n, npick), I32)],
        compiler_params=_cparams(("parallel",)),
        name="peer_topk",
    )(h2, wq, keys)


def _split_table(tab):
    nexp, d = tab.shape
    t = tab.reshape(2, nexp // 4, 2, d)
    halves = [jnp.stack([t[0, :, h], t[1, :, 1 - h]]) for h in range(2)]
    return jnp.stack(halves).reshape(nexp * d // LANES, LANES)


def _load_table_half(tab_hbm, tab_ref, sem):
    hf = pl.program_id(0)
    half_rows = tab_ref.shape[0]

    @pl.when(pl.program_id(1) == 0)
    def _():
        cp = pltpu.make_async_copy(tab_hbm.at[pl.ds(hf * half_rows, half_rows)], tab_ref, sem)
        cp.start()
        cp.wait()


def _table_row(tab_ref, off):
    return tab_ref[pl.ds(pl.multiple_of(off, SUBLANES), SUBLANES), :]


def _sublane_sum_masks():
    sub = lax.broadcasted_iota(I32, (SUBLANES, LANES), 0)
    return ((sub + 2) % 8 < 4,
            (sub + 1) % 8 < 4,
            sub < 4,
            (sub + 7) % 8 < 4,
            sub % 4 < 2,
            (sub + 3) % 4 < 2,
            sub % 2 == 1)


def _sublane_sums(prods, masks):
    def merge(a, b, m, shift):
        return jnp.where(m, a, b) + pltpu.roll(jnp.where(m, b, a), shift, 0)

    s0 = merge(prods[0], prods[4], masks[0], 4)
    s1 = merge(prods[1], prods[5], masks[1], 4)
    s2 = merge(prods[2], prods[6], masks[2], 4)
    s3 = merge(prods[3], prods[7], masks[3], 4)
    u0 = merge(s0, s2, masks[4], 2)
    u1 = merge(s1, s3, masks[5], 2)
    return merge(u0, u1, masks[6], 1)


def _peer_dot_kernel(iota_ref, off_ref, ngrp_ref, gate_ref, x_ref, tab_hbm, d_ref, tab_ref, q_ref, sem):
    _load_table_half(tab_hbm, tab_ref, sem)
    npick, tb = d_ref.shape
    xr = PEER_XLU_GROUPS * SUBLANES
    lane_lo = lax.broadcasted_iota(I32, (xr, tb), 1)
    lane_hi = lax.broadcasted_iota(I32, (npick - xr, tb), 1)
    masks = _sublane_sum_masks()
    ks = [iota_ref[k] for k in range(SUBLANES)]

    @pl.when(pl.program_id(1) == 0)
    def _():
        def clear(t, carry):
            q_ref[t] = jnp.zeros((npick, LANES), F32)
            return carry
        lax.fori_loop(0, tb, clear, 0)

    def gather(t, carry):
        x = x_ref[t]
        base = pl.multiple_of(t * npick, npick)

        def group(start):
            offs = off_ref.at[pl.ds(base + start, SUBLANES)]
            prods = [_table_row(tab_ref, offs[ks[(k + 1) % SUBLANES]]) * x for k in range(SUBLANES)]
            q_ref[t, pl.ds(start, SUBLANES), :] = _sublane_sums(prods, masks)

        for g in range(PEER_STATIC_GROUPS):
            group(g * SUBLANES)

        def extra(g, c):
            group(pl.multiple_of(g * SUBLANES, SUBLANES))
            return c

        lax.fori_loop(PEER_STATIC_GROUPS, ngrp_ref[base], extra, 0)
        return carry

    def lane_reduce(t, acc):
        col = jnp.sum(q_ref[t, 0:xr, :], axis=1, keepdims=True)
        return jnp.where(lane_lo == t, col, acc)

    def lane_reduce_rest(t, carry):
        @pl.when(ngrp_ref[t * npick] > PEER_XLU_GROUPS)
        def _():
            col = jnp.sum(q_ref[t, xr:npick, :], axis=1, keepdims=True)
            d_ref[xr:npick, :] = jnp.where(lane_hi == t, col, d_ref[xr:npick, :])
        return carry

    lax.fori_loop(0, tb, gather, 0)
    d_ref[0:xr, :] = lax.fori_loop(0, tb, lane_reduce, jnp.zeros((xr, tb), F32), unroll=PEER_XLU_UNROLL)
    d_ref[xr:npick, :] = jnp.zeros((npick - xr, tb), F32)

    @pl.when(jnp.max(gate_ref[:, xr:npick]) > 0.0)
    def _():
        lax.fori_loop(0, tb, lane_reduce_rest, 0)


def _peer_sum_kernel(iota_ref, off_ref, ngrp_ref, gate_ref, d_ref, tab_hbm, f_ref, tab_ref, wl_ref, sem):
    _load_table_half(tab_hbm, tab_ref, sem)
    tb, npick = gate_ref.shape
    xr = PEER_XLU_GROUPS * SUBLANES
    nacc = 4
    ks = [iota_ref[k] for k in range(SUBLANES)]
    gate = gate_ref[...].T
    d = d_ref[...]
    act = 0.5 * d * (1.0 + lax.erf(d * (2.0 ** -0.5)))
    wt = jnp.where(gate > 0.0, gate * act, 0.0)
    lane_lo = lax.broadcasted_iota(I32, (xr, tb), 1)
    lane_hi = lax.broadcasted_iota(I32, (npick - xr, tb), 1)

    def spread(t, carry):
        col = jnp.sum(jnp.where(lane_lo == t, wt[0:xr], 0.0), axis=1, keepdims=True)
        wl_ref[t, 0:xr, :] = jnp.broadcast_to(col, (xr, LANES))
        return carry

    def spread_rest(t, carry):
        @pl.when(ngrp_ref[t * npick] > PEER_XLU_GROUPS)
        def _():
            col = jnp.sum(jnp.where(lane_hi == t, wt[xr:npick], 0.0), axis=1, keepdims=True)
            wl_ref[t, xr:npick, :] = jnp.broadcast_to(col, (npick - xr, LANES))
        return carry

    def token(t, carry):
        base = pl.multiple_of(t * npick, npick)

        def group(start, accs):
            accs = list(accs)
            offs = off_ref.at[pl.ds(base + start, SUBLANES)]
            for k in range(SUBLANES):
                wv = jnp.broadcast_to(wl_ref[t, pl.ds(start + k, 1), :], (SUBLANES, LANES))
                accs[k % nacc] = accs[k % nacc] + wv * _table_row(tab_ref, offs[ks[k]])
            return tuple(accs)

        accs = tuple(jnp.zeros((SUBLANES, LANES), F32) for _ in range(nacc))
        for g in range(PEER_STATIC_GROUPS):
            accs = group(g * SUBLANES, accs)
        accs = lax.fori_loop(PEER_STATIC_GROUPS, ngrp_ref[base],
                             lambda g, a: group(pl.multiple_of(g * SUBLANES, SUBLANES), a), accs)
        f_ref[t] = (accs[0] + accs[1]) + (accs[2] + accs[3])
        return carry

    lax.fori_loop(0, tb, spread, 0, unroll=PEER_XLU_UNROLL)

    @pl.when(jnp.max(gate_ref[:, xr:npick]) > 0.0)
    def _():
        lax.fori_loop(0, tb, spread_rest, 0)

    lax.fori_loop(0, tb, token, 0)


def _peer_tiles(n, nb, tile_off_rows):
    tb = PEER_TOK
    nt_all = n // nb // tb
    off = tile_off_rows // tb
    nt = nt_all - off
    return tb, nb * nt, _flat_tile(nt_all, nt, off)


def _smem_token_block(tb, npick, blocks_per_half, ft):
    return pl.BlockSpec((tb * npick,), lambda hf, i: (hf * blocks_per_half + ft(i),), memory_space=pltpu.SMEM)


def _peer_dot(off_flat, ngrp_flat, gate2, h2v, tab2, nb, tile_off_rows):
    _, n, npick = gate2.shape
    tb, steps, ft = _peer_tiles(n, nb, tile_off_rows)
    half = tab2.shape[0] // 2
    smem_blk = _smem_token_block(tb, npick, n // tb, ft)
    return pl.pallas_call(
        _peer_dot_kernel,
        grid=(2, steps),
        in_specs=[pl.BlockSpec(memory_space=pltpu.SMEM), smem_blk, smem_blk,
                  pl.BlockSpec((None, tb, npick), lambda hf, i: (hf, ft(i), 0)),
                  pl.BlockSpec((tb, SUBLANES, LANES), lambda hf, i: (ft(i), 0, 0)),
                  pl.BlockSpec(memory_space=pl.ANY)],
        out_specs=pl.BlockSpec((None, npick, tb), lambda hf, i: (hf, 0, ft(i))),
        out_shape=jax.ShapeDtypeStruct((2, npick, n), F32),
        scratch_shapes=[pltpu.VMEM((half, LANES), F32),
                        pltpu.VMEM((tb, npick, LANES), F32),
                        pltpu.SemaphoreType.DMA],
        compiler_params=_cparams(("arbitrary", "arbitrary"), VMEM_LIMIT_TABLE),
        name="peer_dot",
    )(jnp.arange(SUBLANES, dtype=I32), off_flat, ngrp_flat, gate2, h2v, tab2)


def _peer_sum(off_flat, ngrp_flat, gate2, dpart, tab2, nb, tile_off_rows):
    _, n, npick = gate2.shape
    tb, steps, ft = _peer_tiles(n, nb, tile_off_rows)
    half = tab2.shape[0] // 2
    smem_blk = _smem_token_block(tb, npick, n // tb, ft)
    return pl.pallas_call(
        _peer_sum_kernel,
        grid=(2, steps),
        in_specs=[pl.BlockSpec(memory_space=pltpu.SMEM), smem_blk, smem_blk,
                  pl.BlockSpec((None, tb, npick), lambda hf, i: (hf, ft(i), 0)),
                  pl.BlockSpec((None, npick, tb), lambda hf, i: (hf, 0, ft(i))),
                  pl.BlockSpec(memory_space=pl.ANY)],
        out_specs=pl.BlockSpec((None, tb, SUBLANES, LANES), lambda hf, i: (hf, ft(i), 0, 0)),
        out_shape=jax.ShapeDtypeStruct((2, n, SUBLANES, LANES), F32),
        scratch_shapes=[pltpu.VMEM((half, LANES), F32),
                        pltpu.VMEM((tb, npick, LANES), F32),
                        pltpu.SemaphoreType.DMA],
        compiler_params=_cparams(("arbitrary", "arbitrary"), VMEM_LIMIT_TABLE),
        name="peer_sum",
    )(jnp.arange(SUBLANES, dtype=I32), off_flat, ngrp_flat, gate2, dpart, tab2)


def _mixer_out(f_ref, shape):
    return (f_ref[0] + f_ref[1]).reshape(shape)


def _residual_kernel(x_ref, f_ref, mod_ref, o_ref):
    o_ref[...] = x_ref[...] + mod_ref[5:6, :] * _mixer_out(f_ref, o_ref.shape)


def _final_kernel(x_ref, f_ref, mod_ref, g_ref, o_ref):
    o_ref[...] = _rms(x_ref[...] + mod_ref[5:6, :] * _mixer_out(f_ref, o_ref.shape)) * g_ref[...]


def _residual(x1, f2, mods):
    nb, l, d = x1.shape
    tm = ROW_TILE
    return pl.pallas_call(
        _residual_kernel,
        grid=(nb, l // tm),
        in_specs=[pl.BlockSpec((None, tm, d), lambda b, t: (b, t, 0)),
                  pl.BlockSpec((2, None, tm, SUBLANES, LANES), lambda b, t: (0, b, t, 0, 0)),
                  pl.BlockSpec((None, 6, d), _mod_index(nb))],
        out_specs=pl.BlockSpec((None, tm, d), lambda b, t: (b, t, 0)),
        out_shape=jax.ShapeDtypeStruct((nb, l, d), F32),
        compiler_params=_cparams(("parallel", "parallel")),
        name="peer_residual",
    )(x1, f2, mods)


def _final(x1, f2, mods, g, tile_off):
    nb, l, d = x1.shape
    tm = ROW_TILE
    nt = l // tm - tile_off
    return pl.pallas_call(
        _final_kernel,
        grid=(nb, nt),
        in_specs=[pl.BlockSpec((None, tm, d), lambda b, t: (b, t + tile_off, 0)),
                  pl.BlockSpec((2, None, tm, SUBLANES, LANES), lambda b, t: (0, b, t + tile_off, 0, 0)),
                  pl.BlockSpec((None, 6, d), lambda b, t: (b, 0, 0)),
                  pl.BlockSpec((1, d), lambda b, t: (0, 0))],
        out_specs=pl.BlockSpec((None, tm, d), lambda b, t: (b, t, 0)),
        out_shape=jax.ShapeDtypeStruct((nb, nt * tm, d), F32),
        compiler_params=_cparams(("parallel", "parallel")),
        name="final_norm",
    )(x1, f2, mods, g)


def _rope_tables(seq, ctx_len):
    t = jnp.arange(seq)
    row_id = (t // GRID_W).astype(F32)
    col_id = (t % GRID_W).astype(F32)
    axis_dim = HEAD_DIM // 2
    inv = ROPE_THETA ** (-jnp.arange(0, axis_dim, 2, dtype=F32) / axis_dim)
    ang_r = row_id[:, None] * inv[None, :]
    ang_c = col_id[:, None] * inv[None, :]
    cos = jnp.concatenate([jnp.cos(ang_r)] * 2 + [jnp.cos(ang_c)] * 2, axis=-1)
    sin = jnp.concatenate([-jnp.sin(ang_r), jnp.sin(ang_r), -jnp.sin(ang_c), jnp.sin(ang_c)], axis=-1)
    cos = jnp.concatenate([jnp.ones((ctx_len, HEAD_DIM), F32), cos], axis=0)
    sin = jnp.concatenate([jnp.zeros((ctx_len, HEAD_DIM), F32), sin], axis=0)
    return jnp.tile(cos, (1, 2)), jnp.tile(sin, (1, 2))


def _peer(h2, h2t, x1, mods, wq, keys, u_tab, v_tab, nb, tile_off_rows, final_g):
    b, l, d = x1.shape
    n = b * l
    nexp = u_tab.shape[0]
    npick = PEER_HEADS * PEER_TOPK
    off2, gate2, ngrp2 = _peer_topk(h2.reshape(n, d), wq.astype(BF16),
                                    keys.reshape(PEER_HEADS * 2, PEER_NKEYS, -1).astype(BF16),
                                    nexp // 2, nb, tile_off_rows // ROW_TILE)
    off_flat = off2.reshape(2 * n * npick)
    ngrp_flat = ngrp2.reshape(2 * n * npick)
    h2v = h2t.reshape(n, SUBLANES, LANES)
    dpart = _peer_dot(off_flat, ngrp_flat, gate2, h2v, _split_table(u_tab), nb, tile_off_rows)
    f2 = _peer_sum(off_flat, ngrp_flat, gate2, dpart, _split_table(v_tab), nb, tile_off_rows)
    f2 = f2.reshape(2, b, l, SUBLANES, LANES)
    if final_g is None:
        return _residual(x1, f2, mods)
    return _final(x1, f2, mods, final_g.reshape(1, d), tile_off_rows // ROW_TILE)


def kernel(x, c, ctx, c_ctx, ada_w, ada_b, norm1_g, norm2_g, ev_w_in, ev_w_out, gqa_q_norm_g,
           gqa_k_norm_g, ret_log_rate, od_w_in, od_w_out, diff_lambda, diff_subln_g, peer_w_q,
           peer_keys, peer_u, peer_v, final_g):
    nb, seq, d = x.shape
    ctx_len = ctx.shape[1]
    depth = ada_w.shape[0]
    assert ctx_len == ROW_TILE and seq % ROW_TILE == 0 and d == SUBLANES * LANES
    xs = jnp.concatenate([ctx, x], axis=1)
    cos, sin = _rope_tables(seq, ctx_len)
    cc = jnp.concatenate([c, c_ctx[None, :]], axis=0)
    for layer in range(depth):
        last = layer == depth - 1
        mods = _ada(cc, ada_w[layer], ada_b[layer]).reshape(nb + 1, 6, d)
        g1 = norm1_g[layer].reshape(1, d)
        g2 = norm2_g[layer].reshape(1, d)
        lat_off = 1 if last else 0
        if layer % 2 == 0:
            e = layer // 2
            qa, ka, va, qb, kb, vb, gb = _even_inproj(
                xs, g1, mods, ev_w_in[e].astype(BF16), cos, sin,
                jnp.tile(gqa_q_norm_g[e], 2).reshape(1, LANES), jnp.tile(gqa_k_norm_g[e], 2).reshape(1, LANES))
            ya = _gqa(qa, ka, va, ctx_len, 0)
            o_f, o_r = _retention(ret_log_rate[e], qb, kb, vb, ctx_len)
            x1, h2, h2t = _even_outproj(xs, ya, o_f, o_r, gb, ev_w_out[e].astype(BF16), mods, g2)
        else:
            o = layer // 2
            lam_init = 0.8 - 0.6 * math.exp(-0.3 * layer)
            q, k, v = _odd_inproj(xs, g1, mods, od_w_in[o].astype(BF16), cos, sin)
            mix = _diff_attention(q, k, v, diff_lambda[o], diff_subln_g[o].reshape(1, LANES),
                                  ctx_len, lat_off, lam_init)
            x1, h2, h2t = _odd_outproj(xs, mix, od_w_out[o].astype(BF16), mods, g2, lat_off)
        xs = _peer(h2, h2t, x1, mods, peer_w_q[layer], peer_keys[layer], peer_u[layer], peer_v[layer],
                   nb, lat_off * ROW_TILE, final_g if last else None)
    return xs
```

```python
import functools
import math

import jax
import jax.numpy as jnp
from jax import lax
from jax.experimental import pallas as pl
from jax.experimental.pallas import tpu as pltpu

F32 = jnp.float32
BF16 = jnp.bfloat16
I32 = jnp.int32

LANES = 128
SUBLANES = 8
HEAD_DIM = 64
GRID_W = 64
ROPE_THETA = 10000.0
NORM_EPS = 1e-6
RET_CHUNK = 128
PEER_NKEYS = 128
PEER_TOPK = 16
PEER_HEADS = 8
ROW_TILE = 256
PEER_TOK = 256
PEER_STATIC_GROUPS = 9
PEER_XLU_GROUPS = 11
PEER_XLU_UNROLL = 32
KV_CHUNK = 256
VMEM_LIMIT = 48 * 1024 * 1024
VMEM_LIMIT_TABLE = 56 * 1024 * 1024


def _cparams(sem, limit=VMEM_LIMIT):
    return pltpu.CompilerParams(dimension_semantics=sem, vmem_limit_bytes=limit)


def _rms(x):
    return x * lax.rsqrt(jnp.mean(x * x, axis=-1, keepdims=True) + NORM_EPS)


def _dot(a, b):
    return jnp.dot(a, b, preferred_element_type=F32)


def _dot_nt(a, b):
    return lax.dot_general(a, b, (((1,), (1,)), ((), ())), preferred_element_type=F32)


def _sigmoid(x):
    return 1.0 / (1.0 + jnp.exp(-x))


def _ada_kernel(c_ref, w_ref, b_ref, o_ref):
    c = c_ref[...]
    a = (c * _sigmoid(c)).astype(BF16)
    o_ref[...] = _dot(a, w_ref[...].astype(BF16)) + b_ref[...]


def _ada(cc, w, b):
    m, d = cc.shape
    n = w.shape[1]
    tn = 1024
    return pl.pallas_call(
        _ada_kernel,
        grid=(n // tn,),
        in_specs=[pl.BlockSpec((m, d), lambda j: (0, 0)),
                  pl.BlockSpec((d, tn), lambda j: (0, j)),
                  pl.BlockSpec((1, tn), lambda j: (0, j))],
        out_specs=pl.BlockSpec((m, tn), lambda j: (0, j)),
        out_shape=jax.ShapeDtypeStruct((m, n), F32),
        compiler_params=_cparams(("arbitrary",)),
        name="ada_mod",
    )(cc, w, b.reshape(1, n))


def _rope_cols(x, cos, sin_signed):
    lane = lax.broadcasted_iota(I32, x.shape, 1)
    partner = jnp.where((lane % 32) < 16, pltpu.roll(x, LANES - 16, 1), pltpu.roll(x, 16, 1))
    return x * cos + partner * sin_signed


def _head_mean_sq(x):
    r = lax.broadcasted_iota(I32, (LANES, LANES), 0) // HEAD_DIM
    c = lax.broadcasted_iota(I32, (LANES, LANES), 1) // HEAD_DIM
    ones_bd = jnp.where(r == c, 1.0, 0.0).astype(BF16)
    sq = x * x
    hi = sq.astype(BF16)
    lo = (sq - hi.astype(F32)).astype(BF16)
    return (_dot(hi, ones_bd) + _dot(lo, ones_bd)) * (1.0 / HEAD_DIM)


def _prologue(x_ref, g_ref, mod_ref, shift_row, scale_row):
    h = _rms(x_ref[...]) * g_ref[...]
    return h * (1.0 + mod_ref[scale_row:scale_row + 1, :]) + mod_ref[shift_row:shift_row + 1, :]


def _even_inproj_kernel(x_ref, g_ref, mod_ref, w_ref, cos_ref, sin_ref, qg_ref, kg_ref,
                        qa_ref, ka_ref, va_ref, qb_ref, kb_ref, vb_ref, gb_ref):
    h = _prologue(x_ref, g_ref, mod_ref, 0, 1).astype(BF16)
    y = _dot(h, w_ref[...])
    cos = cos_ref[...]
    sin = sin_ref[...]
    scale = HEAD_DIM ** -0.5

    def col(c):
        return y[:, c * LANES:(c + 1) * LANES]

    for c in range(4):
        x = col(c)
        x = x * lax.rsqrt(_head_mean_sq(x) + NORM_EPS) * qg_ref[...]
        qa_ref[:, c * LANES:(c + 1) * LANES] = (_rope_cols(x, cos, sin) * scale).astype(BF16)
    x = col(4)
    x = x * lax.rsqrt(_head_mean_sq(x) + NORM_EPS) * kg_ref[...]
    ka_ref[...] = _rope_cols(x, cos, sin).astype(BF16)
    va_ref[...] = col(5).astype(BF16)
    for c in range(2):
        qb_ref[:, c * LANES:(c + 1) * LANES] = _rope_cols(col(6 + c), cos, sin).astype(BF16)
        kb_ref[:, c * LANES:(c + 1) * LANES] = _rope_cols(col(8 + c) * scale, cos, sin).astype(BF16)
    vb_ref[...] = y[:, 10 * LANES:14 * LANES].astype(BF16)
    gb_ref[...] = y[:, 14 * LANES:18 * LANES]


def _mod_index(nb):
    return lambda b, t: (jnp.where(t == 0, nb, b), 0, 0)


def _even_inproj(x, g, mods, w, cos, sin, qg, kg):
    nb, l, d = x.shape
    tm = ROW_TILE
    n_in = w.shape[1]
    row = lambda width: pl.BlockSpec((None, tm, width), lambda b, t: (b, t, 0))
    widths = (512, 128, 128, 256, 256, 512, 512)
    dts = (BF16, BF16, BF16, BF16, BF16, BF16, F32)
    return pl.pallas_call(
        _even_inproj_kernel,
        grid=(nb, l // tm),
        in_specs=[row(d),
                  pl.BlockSpec((1, d), lambda b, t: (0, 0)),
                  pl.BlockSpec((None, 6, d), _mod_index(nb)),
                  pl.BlockSpec((d, n_in), lambda b, t: (0, 0)),
                  pl.BlockSpec((tm, LANES), lambda b, t: (t, 0)),
                  pl.BlockSpec((tm, LANES), lambda b, t: (t, 0)),
                  pl.BlockSpec((1, LANES), lambda b, t: (0, 0)),
                  pl.BlockSpec((1, LANES), lambda b, t: (0, 0))],
        out_specs=[row(wd) for wd in widths],
        out_shape=[jax.ShapeDtypeStruct((nb, l, wd), dt) for wd, dt in zip(widths, dts)],
        compiler_params=_cparams(("parallel", "parallel")),
        name="even_inproj",
    )(x, g, mods, w, cos, sin, qg, kg)


def _odd_inproj_kernel(x_ref, g_ref, mod_ref, w_ref, cos_ref, sin_ref, q_ref, k_ref, v_ref):
    h = _prologue(x_ref, g_ref, mod_ref, 0, 1).astype(BF16)
    y = _dot(h, w_ref[...])
    cos = cos_ref[...]
    sin = sin_ref[...]
    scale = HEAD_DIM ** -0.5
    for c in range(8):
        sl = slice(c * LANES, (c + 1) * LANES)
        q_ref[:, sl] = (_rope_cols(y[:, sl], cos, sin) * scale).astype(BF16)
        k_ref[:, sl] = _rope_cols(y[:, 8 * LANES + c * LANES:8 * LANES + (c + 1) * LANES], cos, sin).astype(BF16)
    v_ref[...] = y[:, 16 * LANES:24 * LANES].astype(BF16)


def _odd_inproj(x, g, mods, w, cos, sin):
    nb, l, d = x.shape
    tm = ROW_TILE
    n_in = w.shape[1]
    row = lambda width: pl.BlockSpec((None, tm, width), lambda b, t: (b, t, 0))
    return pl.pallas_call(
        _odd_inproj_kernel,
        grid=(nb, l // tm),
        in_specs=[row(d),
                  pl.BlockSpec((1, d), lambda b, t: (0, 0)),
                  pl.BlockSpec((None, 6, d), _mod_index(nb)),
                  pl.BlockSpec((d, n_in), lambda b, t: (0, 0)),
                  pl.BlockSpec((tm, LANES), lambda b, t: (t, 0)),
                  pl.BlockSpec((tm, LANES), lambda b, t: (t, 0))],
        out_specs=[row(d), row(d), row(d)],
        out_shape=[jax.ShapeDtypeStruct((nb, l, d), BF16)] * 3,
        compiler_params=_cparams(("parallel", "parallel")),
        name="odd_inproj",
    )(x, g, mods, w, cos, sin)


def _scores(q, k_ref, nchunks):
    kc = KV_CHUNK
    scores, run = [], None
    for c in range(nchunks):
        s = _dot_nt(q, k_ref[c * kc:(c + 1) * kc, :])
        scores.append(s)
        for j in range(kc // LANES):
            col = s[:, j * LANES:(j + 1) * LANES]
            run = col if run is None else jnp.maximum(run, col)
    return scores, jnp.max(run, axis=-1, keepdims=True)


def _softmax_pv(scores_max, v_ref, next_q=None, k_ref=None):
    scores, m = scores_max
    kc = KV_CHUNK
    acc = jnp.zeros((scores[0].shape[0], v_ref.shape[1]), F32)
    den = jnp.zeros((scores[0].shape[0], LANES), F32)
    nxt, run = [], None
    for c, s in enumerate(scores):
        if next_q is not None:
            sn = _dot_nt(next_q, k_ref[c * kc:(c + 1) * kc, :])
            nxt.append(sn)
            for j in range(kc // LANES):
                col = sn[:, j * LANES:(j + 1) * LANES]
                run = col if run is None else jnp.maximum(run, col)
        e = jnp.exp(s - m)
        for j in range(kc // LANES):
            den = den + e[:, j * LANES:(j + 1) * LANES]
        acc = acc + _dot(e.astype(BF16), v_ref[c * kc:(c + 1) * kc, :])
    den = jnp.sum(den, axis=-1, keepdims=True)
    if next_q is None:
        return acc, den
    return acc, den, (nxt, jnp.max(run, axis=-1, keepdims=True))


def _key_chunk_branches(qi, ctx_len, total, body):
    @pl.when(qi == 0)
    def _():
        body(ctx_len // KV_CHUNK)

    @pl.when(qi != 0)
    def _():
        body(total // KV_CHUNK)


def _gqa_kernel(q_ref, k_ref, v_ref, o_ref, *, ctx_len, tile_off):
    g = pl.program_id(1)
    qi = pl.program_id(2) + tile_off
    tq = q_ref.shape[0]
    half = lax.broadcasted_iota(I32, (tq, LANES), 1) // HEAD_DIM

    def head_q(i):
        qh = q_ref[:, (i // 2) * LANES:(i // 2 + 1) * LANES]
        qsel = jnp.where(half == (i % 2), qh, jnp.zeros_like(qh)).astype(F32)
        return jnp.where(g == (i % 2), qsel, pltpu.roll(qsel, HEAD_DIM, 1)).astype(BF16)

    def body(nchunks):
        cols = [jnp.zeros((tq, LANES), F32), jnp.zeros((tq, LANES), F32)]
        cur = _scores(head_q(0), k_ref, nchunks)
        for i in range(4):
            if i + 1 < 4:
                acc, den, cur = _softmax_pv(cur, v_ref, head_q(i + 1), k_ref)
            else:
                acc, den = _softmax_pv(cur, v_ref)
            o = acc / den
            osel = jnp.where(half == g, o, 0.0)
            cols[i // 2] = cols[i // 2] + jnp.where(g == (i % 2), osel, pltpu.roll(osel, HEAD_DIM, 1))
        o_ref[:, 0:LANES] = cols[0].astype(BF16)
        o_ref[:, LANES:2 * LANES] = cols[1].astype(BF16)

    if tile_off == 0:
        _key_chunk_branches(qi, ctx_len, k_ref.shape[0], body)
    else:
        body(k_ref.shape[0] // KV_CHUNK)


def _gqa(q, k, v, ctx_len, tile_off):
    nb, l, _ = q.shape
    tq = ROW_TILE
    nt = l // tq - tile_off
    return pl.pallas_call(
        functools.partial(_gqa_kernel, ctx_len=ctx_len, tile_off=tile_off),
        grid=(nb, 2, nt),
        in_specs=[pl.BlockSpec((None, tq, 2 * LANES), lambda b, g, t: (b, t + tile_off, g)),
                  pl.BlockSpec((None, l, LANES), lambda b, g, t: (b, 0, 0)),
                  pl.BlockSpec((None, l, LANES), lambda b, g, t: (b, 0, 0))],
        out_specs=pl.BlockSpec((None, tq, 2 * LANES), lambda b, g, t: (b, t + tile_off, g)),
        out_shape=jax.ShapeDtypeStruct((nb, l, 4 * LANES), BF16),
        compiler_params=_cparams(("parallel", "parallel", "parallel")),
        name="gqa_attention",
    )(q, k, v)


def _diff_kernel(q_ref, k_ref, v_ref, lam_ref, g_ref, o_ref, *, ctx_len, tile_off, lam_init):
    qi = pl.program_id(2) + tile_off
    tq = q_ref.shape[0]
    lf = lam_ref[...]
    lam = (jnp.exp(jnp.sum(lf[0:1, :] * lf[1:2, :], axis=-1, keepdims=True))
           - jnp.exp(jnp.sum(lf[2:3, :] * lf[3:4, :], axis=-1, keepdims=True)) + lam_init)
    half = lax.broadcasted_iota(I32, (tq, LANES), 1) // HEAD_DIM

    def body(nchunks):
        q = q_ref[...]
        zero = jnp.zeros_like(q)
        s1 = _scores(jnp.where(half == 0, q, zero), k_ref, nchunks)
        a1, d1, s2 = _softmax_pv(s1, v_ref, jnp.where(half == 1, q, zero), k_ref)
        a2, d2 = _softmax_pv(s2, v_ref)
        o = a1 / d1 - lam * (a2 / d2)
        o_ref[...] = (_rms(o) * g_ref[...] * (1.0 - lam_init)).astype(BF16)

    if tile_off == 0:
        _key_chunk_branches(qi, ctx_len, k_ref.shape[0], body)
    else:
        body(k_ref.shape[0] // KV_CHUNK)


def _diff_attention(q, k, v, lam_p, subln_g, ctx_len, tile_off, lam_init):
    nb, l, d = q.shape
    nh = d // LANES
    tq = ROW_TILE
    nt = l // tq - tile_off
    return pl.pallas_call(
        functools.partial(_diff_kernel, ctx_len=ctx_len, tile_off=tile_off, lam_init=lam_init),
        grid=(nb, nh, nt),
        in_specs=[pl.BlockSpec((None, tq, LANES), lambda b, h, t: (b, t + tile_off, h)),
                  pl.BlockSpec((None, l, LANES), lambda b, h, t: (b, 0, h)),
                  pl.BlockSpec((None, l, LANES), lambda b, h, t: (b, 0, h)),
                  pl.BlockSpec((4, HEAD_DIM), lambda b, h, t: (0, 0)),
                  pl.BlockSpec((1, LANES), lambda b, h, t: (0, 0))],
        out_specs=pl.BlockSpec((None, tq, LANES), lambda b, h, t: (b, t + tile_off, h)),
        out_shape=jax.ShapeDtypeStruct((nb, l, d), BF16),
        compiler_params=_cparams(("parallel", "parallel", "parallel")),
        name="diff_attention",
    )(q, k, v, lam_p, subln_g)


def _retention_kernel(rate_ref, qf_ref, kf_ref, vf_ref, qr_ref, kr_ref, vr_ref,
                      of_ref, or_ref, sf_ref, sr_ref):
    n = pl.program_id(1)
    c = RET_CHUNK
    nh = vf_ref.shape[1] // LANES

    @pl.when(n == 0)
    def _():
        sf_ref[...] = jnp.zeros_like(sf_ref)
        sr_ref[...] = jnp.zeros_like(sr_ref)

    half = lax.broadcasted_iota(I32, (c, LANES), 1) // HEAD_DIM
    ri = lax.broadcasted_iota(I32, (c, c), 0)
    ci = lax.broadcasted_iota(I32, (c, c), 1)
    pos = lax.broadcasted_iota(I32, (c, 1), 0).astype(F32)

    def one(direction, h, q_ref, k_ref, v_ref, o_ref, s_ref):
        qk = slice((h // 2) * LANES, (h // 2 + 1) * LANES)
        vo = slice(h * LANES, (h + 1) * LANES)
        head_ok = half == (h % 2)
        log_g = -jnp.exp(jnp.full((1, 1), rate_ref[direction, h], F32))
        rel = ((ri - ci) if direction == 0 else (ci - ri)).astype(F32)
        dmask = jnp.where(rel >= 0, jnp.exp(log_g * jnp.maximum(rel, 0.0)), 0.0)
        q = jnp.where(head_ok, q_ref[:, qk], jnp.zeros((c, LANES), BF16))
        k = jnp.where(head_ok, k_ref[:, qk], jnp.zeros((c, LANES), BF16))
        v = v_ref[:, vo]
        sc = _dot_nt(q, k) * dmask
        intra = _dot(sc.astype(BF16), v)
        q_pow = (pos + 1.0) if direction == 0 else (c - pos)
        k_pow = (c - 1.0 - pos) if direction == 0 else pos
        state = s_ref[h]
        cross = _dot((q.astype(F32) * jnp.exp(log_g * q_pow)).astype(BF16), state.astype(BF16))
        o_ref[:, vo] = intra + cross
        kd = (k.astype(F32) * jnp.exp(log_g * k_pow)).T.astype(BF16)
        s_ref[h] = jnp.exp(log_g * c) * state + _dot(kd, v)

    for h in range(nh):
        one(0, h, qf_ref, kf_ref, vf_ref, of_ref, sf_ref)
        one(1, h, qr_ref, kr_ref, vr_ref, or_ref, sr_ref)


def _retention(rates, q, k, v, ctx_len):
    nb, l, dv_all = v.shape
    c = RET_CHUNK
    nh = dv_all // LANES
    nctx = ctx_len // c
    ntot = l // c

    def rev(n):
        return jnp.where(n < nctx, nctx - 1 - n, ntot - 1 - (n - nctx))

    dqk = q.shape[2]
    qk_f = pl.BlockSpec((None, c, dqk), lambda b, n: (b, n, 0))
    qk_r = pl.BlockSpec((None, c, dqk), lambda b, n: (b, rev(n), 0))
    v_f = pl.BlockSpec((None, c, dv_all), lambda b, n: (b, n, 0))
    v_r = pl.BlockSpec((None, c, dv_all), lambda b, n: (b, rev(n), 0))
    return pl.pallas_call(
        _retention_kernel,
        grid=(nb, ntot),
        in_specs=[pl.BlockSpec(memory_space=pltpu.SMEM), qk_f, qk_f, v_f, qk_r, qk_r, v_r],
        out_specs=[v_f, v_r],
        out_shape=[jax.ShapeDtypeStruct((nb, l, dv_all), F32)] * 2,
        scratch_shapes=[pltpu.VMEM((nh, LANES, LANES), F32), pltpu.VMEM((nh, LANES, LANES), F32)],
        compiler_params=_cparams(("parallel", "arbitrary")),
        name="retention",
    )(rates, q, k, v, q, k, v)


def _finish_outproj(y, x_ref, mod_ref, g2_ref, x1_ref, h2_ref, h2t_ref):
    x1 = x_ref[...] + mod_ref[2:3, :] * y
    x1_ref[...] = x1
    h2 = (_rms(x1) * g2_ref[...]) * (1.0 + mod_ref[4:5, :]) + mod_ref[3:4, :]
    h2_ref[...] = h2
    h2t_ref[...] = h2.reshape(h2.shape[0], SUBLANES, LANES)


def _even_outproj_kernel(x_ref, ya_ref, of_ref, or_ref, gb_ref, w_ref, mod_ref, g2_ref, x1_ref, h2_ref, h2t_ref):
    half = ya_ref.shape[1]
    y = _dot(ya_ref[...], w_ref[0:half, :])
    for c in range(half // LANES):
        sl = slice(c * LANES, (c + 1) * LANES)
        gate = gb_ref[:, sl]
        r = _rms(of_ref[:, sl] + or_ref[:, sl]) * (gate * _sigmoid(gate))
        y = y + _dot(r.astype(BF16), w_ref[half + c * LANES:half + (c + 1) * LANES, :])
    _finish_outproj(y, x_ref, mod_ref, g2_ref, x1_ref, h2_ref, h2t_ref)


def _outproj_outputs(nb, l, d, tm, tile_off):
    row = pl.BlockSpec((None, tm, d), lambda b, t: (b, t + tile_off, 0))
    tiles = pl.BlockSpec((None, tm, SUBLANES, LANES), lambda b, t: (b, t + tile_off, 0, 0))
    shapes = [jax.ShapeDtypeStruct((nb, l, d), F32)] * 2 + [jax.ShapeDtypeStruct((nb, l, SUBLANES, LANES), F32)]
    return [row, row, tiles], shapes


def _even_outproj(x, ya, o_f, o_r, gb, w, mods, g2):
    nb, l, d = x.shape
    tm = ROW_TILE
    out_specs, out_shape = _outproj_outputs(nb, l, d, tm, 0)
    row = lambda width: pl.BlockSpec((None, tm, width), lambda b, t: (b, t, 0))
    return pl.pallas_call(
        _even_outproj_kernel,
        grid=(nb, l // tm),
        in_specs=[row(d), row(512), row(512), row(512), row(512),
                  pl.BlockSpec((d, d), lambda b, t: (0, 0)),
                  pl.BlockSpec((None, 6, d), _mod_index(nb)),
                  pl.BlockSpec((1, d), lambda b, t: (0, 0))],
        out_specs=out_specs,
        out_shape=out_shape,
        compiler_params=_cparams(("parallel", "parallel")),
        name="even_outproj",
    )(x, ya, o_f, o_r, gb, w, mods, g2)


def _odd_outproj_kernel(x_ref, mix_ref, w_ref, mod_ref, g2_ref, x1_ref, h2_ref, h2t_ref):
    _finish_outproj(_dot(mix_ref[...], w_ref[...]), x_ref, mod_ref, g2_ref, x1_ref, h2_ref, h2t_ref)


def _odd_outproj(x, mix, w, mods, g2, tile_off):
    nb, l, d = x.shape
    tm = ROW_TILE
    out_specs, out_shape = _outproj_outputs(nb, l, d, tm, tile_off)
    row = pl.BlockSpec((None, tm, d), lambda b, t: (b, t + tile_off, 0))
    return pl.pallas_call(
        _odd_outproj_kernel,
        grid=(nb, l // tm - tile_off),
        in_specs=[row, row,
                  pl.BlockSpec((d, d), lambda b, t: (0, 0)),
                  pl.BlockSpec((None, 6, d), lambda b, t: (b, 0, 0)),
                  pl.BlockSpec((1, d), lambda b, t: (0, 0))],
        out_specs=out_specs,
        out_shape=out_shape,
        compiler_params=_cparams(("parallel", "parallel")),
        name="odd_outproj",
    )(x, mix, w, mods, g2)


def _top_rows(s, k, payload=None):
    rows = lax.broadcasted_iota(I32, s.shape, 0).astype(F32)
    n = float(s.shape[0])
    vals, pays = [], []
    for _ in range(k):
        m = jnp.max(s, axis=0, keepdims=True)
        am = jnp.min(jnp.where(s == m, rows, n), axis=0, keepdims=True)
        hit = rows == am
        vals.append(m)
        pays.append(am if payload is None else jnp.max(jnp.where(hit, payload, -1.0), axis=0, keepdims=True))
        s = jnp.where(hit, -jnp.inf, s)
    return jnp.concatenate(vals, axis=0), jnp.concatenate(pays, axis=0)


def _pair_candidates(s1, i1, s2, i2):
    kk = PEER_TOPK
    sub = lax.broadcasted_iota(I32, (SUBLANES, s1.shape[1]), 0)
    cand = [s1[0:1, :] + s2]
    cid = [i1[0:1, :] * PEER_NKEYS + i2]
    for a in range(1, SUBLANES):
        ok = sub < kk // (a + 1)
        cand.append(jnp.where(ok, s1[a:a + 1, :] + s2[0:SUBLANES, :], -jnp.inf))
        cid.append(i1[a:a + 1, :] * PEER_NKEYS + i2[0:SUBLANES, :])
    cand.append(s1[SUBLANES:kk, :] + s2[0:1, :])
    cid.append(i1[SUBLANES:kk, :] * PEER_NKEYS + i2[0:1, :])
    return jnp.concatenate(cand, axis=0), jnp.concatenate(cid, axis=0)


def _compress_rows(sel, skip, arrays):
    n = sel.shape[0]
    cc = jnp.where(sel, skip, 0)
    vals = [jnp.where(sel, a, jnp.zeros_like(a)) for a in arrays]
    for k in range(n.bit_length() - 1):
        mv = ((cc >> k) & 1) == 1

        def step(a):
            zero = jnp.zeros_like(a)
            return jnp.where(mv, zero, a) + pltpu.roll(jnp.where(mv, a, zero), n - (1 << k), 0)

        vals = [step(a) for a in vals]
        cc = step(cc)
    return vals


def _peer_topk_kernel(h_ref, wq_ref, keys_ref, off_ref, gate_ref, ngrp_ref, *, half_rows):
    kk = PEER_TOPK
    q = _dot(h_ref[...].astype(BF16), wq_ref[...])
    eids, gates = [], []
    for hd in range(PEER_HEADS):
        tops = []
        for part in range(2):
            c = 2 * hd + part
            qs = q[:, c * LANES:(c + 1) * LANES].astype(BF16)
            tops.append(_top_rows(_dot_nt(keys_ref[c], qs), kk))
        (s1, i1), (s2, i2) = tops
        cand, cid = _pair_candidates(s1, i1, s2, i2)
        best, eid = _top_rows(cand, kk, cid)
        e = jnp.exp(best - best[0:1, :])
        eids.append(eid)
        gates.append(e / jnp.sum(e, axis=0, keepdims=True))
    eid = jnp.concatenate(eids, axis=0).astype(I32)
    gate = jnp.concatenate(gates, axis=0)
    npick, tm = eid.shape
    top = eid // half_rows
    upper = ((top ^ eid) & 1) == 1
    off = (top * (half_rows // 2) + (eid & (half_rows - 1)) // 2) * SUBLANES
    row = lax.broadcasted_iota(I32, (npick, tm), 0)
    before = (lax.broadcasted_iota(I32, (npick, npick), 1) < lax.broadcasted_iota(I32, (npick, npick), 0))
    upper_before = _dot(before.astype(BF16), upper.astype(BF16)).astype(I32)
    for hf, (sel, skip) in enumerate(((~upper, upper_before), (upper, row - upper_before))):
        off_c, gate_c = _compress_rows(sel, skip, [off, gate])
        off_ref[hf] = off_c.T
        gate_ref[hf] = gate_c.T
        count = jnp.sum(sel.astype(F32), axis=0, keepdims=True)
        ngrp = ((count + (SUBLANES - 1)) * (1.0 / SUBLANES)).astype(I32)
        ngrp_ref[hf] = jnp.broadcast_to(ngrp, (npick, tm)).T


def _flat_tile(nt_all, nt, tile_off):
    return lambda i: (i // nt) * nt_all + (i % nt) + tile_off


def _peer_topk(h2, wq, keys, half_rows, nb, tile_off):
    n, d = h2.shape
    tm = ROW_TILE
    nt_all = n // nb // tm
    nt = nt_all - tile_off
    ft = _flat_tile(nt_all, nt, tile_off)
    npick = PEER_HEADS * PEER_TOPK
    out = pl.BlockSpec((2, tm, npick), lambda i: (0, ft(i), 0))
    return pl.pallas_call(
        functools.partial(_peer_topk_kernel, half_rows=half_rows),
        grid=(nb * nt,),
        in_specs=[pl.BlockSpec((tm, d), lambda i: (ft(i), 0)),
                  pl.BlockSpec(wq.shape, lambda i: (0, 0)),
                  pl.BlockSpec(keys.shape, lambda i: (0, 0, 0))],
        out_specs=[out, out, out],
        out_shape=[jax.ShapeDtypeStruct((2, n, npick), I32), jax.ShapeDtypeStruct((2, n, npick), F32),
                   jax.ShapeDtypeStruct((2, n, npick), I32)],
        compiler_params=_cparams(("parallel",)),
        name="peer_topk",
    )(h2, wq, keys)


def _split_table(tab):
    nexp, d = tab.shape
    t = tab.reshape(2, nexp // 4, 2, d)
    halves = [jnp.stack([t[0, :, h], t[1, :, 1 - h]]) for h in range(2)]
    return jnp.stack(halves).reshape(nexp * d // LANES, LANES)


def _load_table_half(tab_hbm, tab_ref, sem):
    hf = pl.program_id(0)
    half_rows = tab_ref.shape[0]

    @pl.when(pl.program_id(1) == 0)
    def _():
        cp = pltpu.make_async_copy(tab_hbm.at[pl.ds(hf * half_rows, half_rows)], tab_ref, sem)
        cp.start()
        cp.wait()


def _table_row(tab_ref, off):
    return tab_ref[pl.ds(pl.multiple_of(off, SUBLANES), SUBLANES), :]


def _sublane_sum_masks():
    sub = lax.broadcasted_iota(I32, (SUBLANES, LANES), 0)
    return ((sub + 2) % 8 < 4,
            (sub + 1) % 8 < 4,
            sub < 4,
            (sub + 7) % 8 < 4,
            sub % 4 < 2,
            (sub + 3) % 4 < 2,
            sub % 2 == 1)


def _sublane_sums(prods, masks):
    def merge(a, b, m, shift):
        return jnp.where(m, a, b) + pltpu.roll(jnp.where(m, b, a), shift, 0)

    s0 = merge(prods[0], prods[4], masks[0], 4)
    s1 = merge(prods[1], prods[5], masks[1], 4)
    s2 = merge(prods[2], prods[6], masks[2], 4)
    s3 = merge(prods[3], prods[7], masks[3], 4)
    u0 = merge(s0, s2, masks[4], 2)
    u1 = merge(s1, s3, masks[5], 2)
    return merge(u0, u1, masks[6], 1)


def _peer_dot_kernel(iota_ref, off_ref, ngrp_ref, gate_ref, x_ref, tab_hbm, d_ref, tab_ref, q_ref, sem):
    _load_table_half(tab_hbm, tab_ref, sem)
    npick, tb = d_ref.shape
    xr = PEER_XLU_GROUPS * SUBLANES
    lane_lo = lax.broadcasted_iota(I32, (xr, tb), 1)
    lane_hi = lax.broadcasted_iota(I32, (npick - xr, tb), 1)
    masks = _sublane_sum_masks()
    ks = [iota_ref[k] for k in range(SUBLANES)]

    @pl.when(pl.program_id(1) == 0)
    def _():
        def clear(t, carry):
            q_ref[t] = jnp.zeros((npick, LANES), F32)
            return carry
        lax.fori_loop(0, tb, clear, 0)

    def gather(t, carry):
        x = x_ref[t]
        base = pl.multiple_of(t * npick, npick)

        def group(start):
            offs = off_ref.at[pl.ds(base + start, SUBLANES)]
            prods = [_table_row(tab_ref, offs[ks[(k + 1) % SUBLANES]]) * x for k in range(SUBLANES)]
            q_ref[t, pl.ds(start, SUBLANES), :] = _sublane_sums(prods, masks)

        for g in range(PEER_STATIC_GROUPS):
            group(g * SUBLANES)

        def extra(g, c):
            group(pl.multiple_of(g * SUBLANES, SUBLANES))
            return c

        lax.fori_loop(PEER_STATIC_GROUPS, ngrp_ref[base], extra, 0)
        return carry

    def lane_reduce(t, acc):
        col = jnp.sum(q_ref[t, 0:xr, :], axis=1, keepdims=True)
        return jnp.where(lane_lo == t, col, acc)

    def lane_reduce_rest(t, carry):
        @pl.when(ngrp_ref[t * npick] > PEER_XLU_GROUPS)
        def _():
            col = jnp.sum(q_ref[t, xr:npick, :], axis=1, keepdims=True)
            d_ref[xr:npick, :] = jnp.where(lane_hi == t, col, d_ref[xr:npick, :])
        return carry

    lax.fori_loop(0, tb, gather, 0)
    d_ref[0:xr, :] = lax.fori_loop(0, tb, lane_reduce, jnp.zeros((xr, tb), F32), unroll=PEER_XLU_UNROLL)
    d_ref[xr:npick, :] = jnp.zeros((npick - xr, tb), F32)

    @pl.when(jnp.max(gate_ref[:, xr:npick]) > 0.0)
    def _():
        lax.fori_loop(0, tb, lane_reduce_rest, 0)


def _peer_sum_kernel(iota_ref, off_ref, ngrp_ref, gate_ref, d_ref, tab_hbm, f_ref, tab_ref, wl_ref, sem):
    _load_table_half(tab_hbm, tab_ref, sem)
    tb, npick = gate_ref.shape
    xr = PEER_XLU_GROUPS * SUBLANES
    nacc = 4
    ks = [iota_ref[k] for k in range(SUBLANES)]
    gate = gate_ref[...].T
    d = d_ref[...]
    act = 0.5 * d * (1.0 + lax.erf(d * (2.0 ** -0.5)))
    wt = jnp.where(gate > 0.0, gate * act, 0.0)
    lane_lo = lax.broadcasted_iota(I32, (xr, tb), 1)
    lane_hi = lax.broadcasted_iota(I32, (npick - xr, tb), 1)

    def spread(t, carry):
        col = jnp.sum(jnp.where(lane_lo == t, wt[0:xr], 0.0), axis=1, keepdims=True)
        wl_ref[t, 0:xr, :] = jnp.broadcast_to(col, (xr, LANES))
        return carry

    def spread_rest(t, carry):
        @pl.when(ngrp_ref[t * npick] > PEER_XLU_GROUPS)
        def _():
            col = jnp.sum(jnp.where(lane_hi == t, wt[xr:npick], 0.0), axis=1, keepdims=True)
            wl_ref[t, xr:npick, :] = jnp.broadcast_to(col, (npick - xr, LANES))
        return carry

    def token(t, carry):
        base = pl.multiple_of(t * npick, npick)

        def group(start, accs):
            accs = list(accs)
            offs = off_ref.at[pl.ds(base + start, SUBLANES)]
            for k in range(SUBLANES):
                wv = jnp.broadcast_to(wl_ref[t, pl.ds(start + k, 1), :], (SUBLANES, LANES))
                accs[k % nacc] = accs[k % nacc] + wv * _table_row(tab_ref, offs[ks[k]])
            return tuple(accs)

        accs = tuple(jnp.zeros((SUBLANES, LANES), F32) for _ in range(nacc))
        for g in range(PEER_STATIC_GROUPS):
            accs = group(g * SUBLANES, accs)
        accs = lax.fori_loop(PEER_STATIC_GROUPS, ngrp_ref[base],
                             lambda g, a: group(pl.multiple_of(g * SUBLANES, SUBLANES), a), accs)
        f_ref[t] = (accs[0] + accs[1]) + (accs[2] + accs[3])
        return carry

    lax.fori_loop(0, tb, spread, 0, unroll=PEER_XLU_UNROLL)

    @pl.when(jnp.max(gate_ref[:, xr:npick]) > 0.0)
    def _():
        lax.fori_loop(0, tb, spread_rest, 0)

    lax.fori_loop(0, tb, token, 0)


def _peer_tiles(n, nb, tile_off_rows):
    tb = PEER_TOK
    nt_all = n // nb // tb
    off = tile_off_rows // tb
    nt = nt_all - off
    return tb, nb * nt, _flat_tile(nt_all, nt, off)


def _smem_token_block(tb, npick, blocks_per_half, ft):
    return pl.BlockSpec((tb * npick,), lambda hf, i: (hf * blocks_per_half + ft(i),), memory_space=pltpu.SMEM)


def _peer_dot(off_flat, ngrp_flat, gate2, h2v, tab2, nb, tile_off_rows):
    _, n, npick = gate2.shape
    tb, steps, ft = _peer_tiles(n, nb, tile_off_rows)
    half = tab2.shape[0] // 2
    smem_blk = _smem_token_block(tb, npick, n // tb, ft)
    return pl.pallas_call(
        _peer_dot_kernel,
        grid=(2, steps),
        in_specs=[pl.BlockSpec(memory_space=pltpu.SMEM), smem_blk, smem_blk,
                  pl.BlockSpec((None, tb, npick), lambda hf, i: (hf, ft(i), 0)),
                  pl.BlockSpec((tb, SUBLANES, LANES), lambda hf, i: (ft(i), 0, 0)),
                  pl.BlockSpec(memory_space=pl.ANY)],
        out_specs=pl.BlockSpec((None, npick, tb), lambda hf, i: (hf, 0, ft(i))),
        out_shape=jax.ShapeDtypeStruct((2, npick, n), F32),
        scratch_shapes=[pltpu.VMEM((half, LANES), F32),
                        pltpu.VMEM((tb, npick, LANES), F32),
                        pltpu.SemaphoreType.DMA],
        compiler_params=_cparams(("arbitrary", "arbitrary"), VMEM_LIMIT_TABLE),
        name="peer_dot",
    )(jnp.arange(SUBLANES, dtype=I32), off_flat, ngrp_flat, gate2, h2v, tab2)


def _peer_sum(off_flat, ngrp_flat, gate2, dpart, tab2, nb, tile_off_rows):
    _, n, npick = gate2.shape
    tb, steps, ft = _peer_tiles(n, nb, tile_off_rows)
    half = tab2.shape[0] // 2
    smem_blk = _smem_token_block(tb, npick, n // tb, ft)
    return pl.pallas_call(
        _peer_sum_kernel,
        grid=(2, steps),
        in_specs=[pl.BlockSpec(memory_space=pltpu.SMEM), smem_blk, smem_blk,
                  pl.BlockSpec((None, tb, npick), lambda hf, i: (hf, ft(i), 0)),
                  pl.BlockSpec((None, npick, tb), lambda hf, i: (hf, 0, ft(i))),
                  pl.BlockSpec(memory_space=pl.ANY)],
        out_specs=pl.BlockSpec((None, tb, SUBLANES, LANES), lambda hf, i: (hf, ft(i), 0, 0)),
        out_shape=jax.ShapeDtypeStruct((2, n, SUBLANES, LANES), F32),
        scratch_shapes=[pltpu.VMEM((half, LANES), F32),
                        pltpu.VMEM((tb, npick, LANES), F32),
                        pltpu.SemaphoreType.DMA],
        compiler_params=_cparams(("arbitrary", "arbitrary"), VMEM_LIMIT_TABLE),
        name="peer_sum",
    )(jnp.arange(SUBLANES, dtype=I32), off_flat, ngrp_flat, gate2, dpart, tab2)


def _mixer_out(f_ref, shape):
    return (f_ref[0] + f_ref[1]).reshape(shape)


def _residual_kernel(x_ref, f_ref, mod_ref, o_ref):
    o_ref[...] = x_ref[...] + mod_ref[5:6, :] * _mixer_out(f_ref, o_ref.shape)


def _final_kernel(x_ref, f_ref, mod_ref, g_ref, o_ref):
    o_ref[...] = _rms(x_ref[...] + mod_ref[5:6, :] * _mixer_out(f_ref, o_ref.shape)) * g_ref[...]


def _residual(x1, f2, mods):
    nb, l, d = x1.shape
    tm = ROW_TILE
    return pl.pallas_call(
        _residual_kernel,
        grid=(nb, l // tm),
        in_specs=[pl.BlockSpec((None, tm, d), lambda b, t: (b, t, 0)),
                  pl.BlockSpec((2, None, tm, SUBLANES, LANES), lambda b, t: (0, b, t, 0, 0)),
                  pl.BlockSpec((None, 6, d), _mod_index(nb))],
        out_specs=pl.BlockSpec((None, tm, d), lambda b, t: (b, t, 0)),
        out_shape=jax.ShapeDtypeStruct((nb, l, d), F32),
        compiler_params=_cparams(("parallel", "parallel")),
        name="peer_residual",
    )(x1, f2, mods)


def _final(x1, f2, mods, g, tile_off):
    nb, l, d = x1.shape
    tm = ROW_TILE
    nt = l // tm - tile_off
    return pl.pallas_call(
        _final_kernel,
        grid=(nb, nt),
        in_specs=[pl.BlockSpec((None, tm, d), lambda b, t: (b, t + tile_off, 0)),
                  pl.BlockSpec((2, None, tm, SUBLANES, LANES), lambda b, t: (0, b, t + tile_off, 0, 0)),
                  pl.BlockSpec((None, 6, d), lambda b, t: (b, 0, 0)),
                  pl.BlockSpec((1, d), lambda b, t: (0, 0))],
        out_specs=pl.BlockSpec((None, tm, d), lambda b, t: (b, t, 0)),
        out_shape=jax.ShapeDtypeStruct((nb, nt * tm, d), F32),
        compiler_params=_cparams(("parallel", "parallel")),
        name="final_norm",
    )(x1, f2, mods, g)


def _rope_tables(seq, ctx_len):
    t = jnp.arange(seq)
    row_id = (t // GRID_W).astype(F32)
    col_id = (t % GRID_W).astype(F32)
    axis_dim = HEAD_DIM // 2
    inv = ROPE_THETA ** (-jnp.arange(0, axis_dim, 2, dtype=F32) / axis_dim)
    ang_r = row_id[:, None] * inv[None, :]
    ang_c = col_id[:, None] * inv[None, :]
    cos = jnp.concatenate([jnp.cos(ang_r)] * 2 + [jnp.cos(ang_c)] * 2, axis=-1)
    sin = jnp.concatenate([-jnp.sin(ang_r), jnp.sin(ang_r), -jnp.sin(ang_c), jnp.sin(ang_c)], axis=-1)
    cos = jnp.concatenate([jnp.ones((ctx_len, HEAD_DIM), F32), cos], axis=0)
    sin = jnp.concatenate([jnp.zeros((ctx_len, HEAD_DIM), F32), sin], axis=0)
    return jnp.tile(cos, (1, 2)), jnp.tile(sin, (1, 2))


def _peer(h2, h2t, x1, mods, wq, keys, u_tab, v_tab, nb, tile_off_rows, final_g):
    b, l, d = x1.shape
    n = b * l
    nexp = u_tab.shape[0]
    npick = PEER_HEADS * PEER_TOPK
    off2, gate2, ngrp2 = _peer_topk(h2.reshape(n, d), wq.astype(BF16),
                                    keys.reshape(PEER_HEADS * 2, PEER_NKEYS, -1).astype(BF16),
                                    nexp // 2, nb, tile_off_rows // ROW_TILE)
    off_flat = off2.reshape(2 * n * npick)
    ngrp_flat = ngrp2.reshape(2 * n * npick)
    h2v = h2t.reshape(n, SUBLANES, LANES)
    dpart = _peer_dot(off_flat, ngrp_flat, gate2, h2v, _split_table(u_tab), nb, tile_off_rows)
    f2 = _peer_sum(off_flat, ngrp_flat, gate2, dpart, _split_table(v_tab), nb, tile_off_rows)
    f2 = f2.reshape(2, b, l, SUBLANES, LANES)
    if final_g is None:
        return _residual(x1, f2, mods)
    return _final(x1, f2, mods, final_g.reshape(1, d), tile_off_rows // ROW_TILE)


def kernel(x, c, ctx, c_ctx, ada_w, ada_b, norm1_g, norm2_g, ev_w_in, ev_w_out, gqa_q_norm_g,
           gqa_k_norm_g, ret_log_rate, od_w_in, od_w_out, diff_lambda, diff_subln_g, peer_w_q,
           peer_keys, peer_u, peer_v, final_g):
    nb, seq, d = x.shape
    ctx_len = ctx.shape[1]
    depth = ada_w.shape[0]
    assert ctx_len == ROW_TILE and seq % ROW_TILE == 0 and d == SUBLANES * LANES
    xs = jnp.concatenate([ctx, x], axis=1)
    cos, sin = _rope_tables(seq, ctx_len)
    cc = jnp.concatenate([c, c_ctx[None, :]], axis=0)
    for layer in range(depth):
        last = layer == depth - 1
        mods = _ada(cc, ada_w[layer], ada_b[layer]).reshape(nb + 1, 6, d)
        g1 = norm1_g[layer].reshape(1, d)
        g2 = norm2_g[layer].reshape(1, d)
        lat_off = 1 if last else 0
        if layer % 2 == 0:
            e = layer // 2
            qa, ka, va, qb, kb, vb, gb = _even_inproj(
                xs, g1, mods, ev_w_in[e].astype(BF16), cos, sin,
                jnp.tile(gqa_q_norm_g[e], 2).reshape(1, LANES), jnp.tile(gqa_k_norm_g[e], 2).reshape(1, LANES))
            ya = _gqa(qa, ka, va, ctx_len, 0)
            o_f, o_r = _retention(ret_log_rate[e], qb, kb, vb, ctx_len)
            x1, h2, h2t = _even_outproj(xs, ya, o_f, o_r, gb, ev_w_out[e].astype(BF16), mods, g2)
        else:
            o = layer // 2
            lam_init = 0.8 - 0.6 * math.exp(-0.3 * layer)
            q, k, v = _odd_inproj(xs, g1, mods, od_w_in[o].astype(BF16), cos, sin)
            mix = _diff_attention(q, k, v, diff_lambda[o], diff_subln_g[o].reshape(1, LANES),
                                  ctx_len, lat_off, lam_init)
            x1, h2, h2t = _odd_outproj(xs, mix, od_w_out[o].astype(BF16), mods, g2, lat_off)
        xs = _peer(h2, h2t, x1, mods, peer_w_q[layer], peer_keys[layer], peer_u[layer], peer_v[layer],
                   nb, lat_off * ROW_TILE, final_g if last else None)
    return xs
```

```python
import functools
import math

import jax
import jax.numpy as jnp
from jax import lax
from jax.experimental import pallas as pl
from jax.experimental.pallas import tpu as pltpu

F32 = jnp.float32
BF16 = jnp.bfloat16
I32 = jnp.int32

LANES = 128
SUBLANES = 8
HEAD_DIM = 64
GRID_W = 64
ROPE_THETA = 10000.0
NORM_EPS = 1e-6
RET_CHUNK = 128
PEER_NKEYS = 128
PEER_TOPK = 16
PEER_HEADS = 8
ROW_TILE = 256
PEER_TOK = 256
PEER_STATIC_GROUPS = 9
PEER_XLU_GROUPS = 11
PEER_XLU_UNROLL = 64
KV_CHUNK = 256
VMEM_LIMIT = 48 * 1024 * 1024
VMEM_LIMIT_TABLE = 56 * 1024 * 1024


def _cparams(sem, limit=VMEM_LIMIT):
    return pltpu.CompilerParams(dimension_semantics=sem, vmem_limit_bytes=limit)


def _rms(x):
    return x * lax.rsqrt(jnp.mean(x * x, axis=-1, keepdims=True) + NORM_EPS)


def _dot(a, b):
    return jnp.dot(a, b, preferred_element_type=F32)


def _dot_nt(a, b):
    return lax.dot_general(a, b, (((1,), (1,)), ((), ())), preferred_element_type=F32)


def _sigmoid(x):
    return 1.0 / (1.0 + jnp.exp(-x))


def _ada_kernel(c_ref, w_ref, b_ref, o_ref):
    c = c_ref[...]
    a = (c * _sigmoid(c)).astype(BF16)
    o_ref[...] = _dot(a, w_ref[...].astype(BF16)) + b_ref[...]


def _ada(cc, w, b):
    m, d = cc.shape
    n = w.shape[1]
    tn = 1024
    return pl.pallas_call(
        _ada_kernel,
        grid=(n // tn,),
        in_specs=[pl.BlockSpec((m, d), lambda j: (0, 0)),
                  pl.BlockSpec((d, tn), lambda j: (0, j)),
                  pl.BlockSpec((1, tn), lambda j: (0, j))],
        out_specs=pl.BlockSpec((m, tn), lambda j: (0, j)),
        out_shape=jax.ShapeDtypeStruct((m, n), F32),
        compiler_params=_cparams(("arbitrary",)),
        name="ada_mod",
    )(cc, w, b.reshape(1, n))


def _rope_cols(x, cos, sin_signed):
    lane = lax.broadcasted_iota(I32, x.shape, 1)
    partner = jnp.where((lane % 32) < 16, pltpu.roll(x, LANES - 16, 1), pltpu.roll(x, 16, 1))
    return x * cos + partner * sin_signed


def _head_mean_sq(x):
    r = lax.broadcasted_iota(I32, (LANES, LANES), 0) // HEAD_DIM
    c = lax.broadcasted_iota(I32, (LANES, LANES), 1) // HEAD_DIM
    ones_bd = jnp.where(r == c, 1.0, 0.0).astype(BF16)
    sq = x * x
    hi = sq.astype(BF16)
    lo = (sq - hi.astype(F32)).astype(BF16)
    return (_dot(hi, ones_bd) + _dot(lo, ones_bd)) * (1.0 / HEAD_DIM)


def _prologue(x_ref, g_ref, mod_ref, shift_row, scale_row):
    h = _rms(x_ref[...]) * g_ref[...]
    return h * (1.0 + mod_ref[scale_row:scale_row + 1, :]) + mod_ref[shift_row:shift_row + 1, :]


def _even_inproj_kernel(x_ref, g_ref, mod_ref, w_ref, cos_ref, sin_ref, qg_ref, kg_ref,
                        qa_ref, ka_ref, va_ref, qb_ref, kb_ref, vb_ref, gb_ref):
    h = _prologue(x_ref, g_ref, mod_ref, 0, 1).astype(BF16)
    y = _dot(h, w_ref[...])
    cos = cos_ref[...]
    sin = sin_ref[...]
    scale = HEAD_DIM ** -0.5

    def col(c):
        return y[:, c * LANES:(c + 1) * LANES]

    for c in range(4):
        x = col(c)
        x = x * lax.rsqrt(_head_mean_sq(x) + NORM_EPS) * qg_ref[...]
        qa_ref[:, c * LANES:(c + 1) * LANES] = (_rope_cols(x, cos, sin) * scale).astype(BF16)
    x = col(4)
    x = x * lax.rsqrt(_head_mean_sq(x) + NORM_EPS) * kg_ref[...]
    ka_ref[...] = _rope_cols(x, cos, sin).astype(BF16)
    va_ref[...] = col(5).astype(BF16)
    for c in range(2):
        qb_ref[:, c * LANES:(c + 1) * LANES] = _rope_cols(col(6 + c), cos, sin).astype(BF16)
        kb_ref[:, c * LANES:(c + 1) * LANES] = _rope_cols(col(8 + c) * scale, cos, sin).astype(BF16)
    vb_ref[...] = y[:, 10 * LANES:14 * LANES].astype(BF16)
    gb_ref[...] = y[:, 14 * LANES:18 * LANES]


def _mod_index(nb):
    return lambda b, t: (jnp.where(t == 0, nb, b), 0, 0)


def _even_inproj(x, g, mods, w, cos, sin, qg, kg):
    nb, l, d = x.shape
    tm = ROW_TILE
    n_in = w.shape[1]
    row = lambda width: pl.BlockSpec((None, tm, width), lambda b, t: (b, t, 0))
    widths = (512, 128, 128, 256, 256, 512, 512)
    dts = (BF16, BF16, BF16, BF16, BF16, BF16, F32)
    return pl.pallas_call(
        _even_inproj_kernel,
        grid=(nb, l // tm),
        in_specs=[row(d),
                  pl.BlockSpec((1, d), lambda b, t: (0, 0)),
                  pl.BlockSpec((None, 6, d), _mod_index(nb)),
                  pl.BlockSpec((d, n_in), lambda b, t: (0, 0)),
                  pl.BlockSpec((tm, LANES), lambda b, t: (t, 0)),
                  pl.BlockSpec((tm, LANES), lambda b, t: (t, 0)),
                  pl.BlockSpec((1, LANES), lambda b, t: (0, 0)),
                  pl.BlockSpec((1, LANES), lambda b, t: (0, 0))],
        out_specs=[row(wd) for wd in widths],
        out_shape=[jax.ShapeDtypeStruct((nb, l, wd), dt) for wd, dt in zip(widths, dts)],
        compiler_params=_cparams(("parallel", "parallel")),
        name="even_inproj",
    )(x, g, mods, w, cos, sin, qg, kg)


def _odd_inproj_kernel(x_ref, g_ref, mod_ref, w_ref, cos_ref, sin_ref, q_ref, k_ref, v_ref):
    h = _prologue(x_ref, g_ref, mod_ref, 0, 1).astype(BF16)
    y = _dot(h, w_ref[...])
    cos = cos_ref[...]
    sin = sin_ref[...]
    scale = HEAD_DIM ** -0.5
    for c in range(8):
        sl = slice(c * LANES, (c + 1) * LANES)
        q_ref[:, sl] = (_rope_cols(y[:, sl], cos, sin) * scale).astype(BF16)
        k_ref[:, sl] = _rope_cols(y[:, 8 * LANES + c * LANES:8 * LANES + (c + 1) * LANES], cos, sin).astype(BF16)
    v_ref[...] = y[:, 16 * LANES:24 * LANES].astype(BF16)


def _odd_inproj(x, g, mods, w, cos, sin):
    nb, l, d = x.shape
    tm = ROW_TILE
    n_in = w.shape[1]
    row = lambda width: pl.BlockSpec((None, tm, width), lambda b, t: (b, t, 0))
    return pl.pallas_call(
        _odd_inproj_kernel,
        grid=(nb, l // tm),
        in_specs=[row(d),
                  pl.BlockSpec((1, d), lambda b, t: (0, 0)),
                  pl.BlockSpec((None, 6, d), _mod_index(nb)),
                  pl.BlockSpec((d, n_in), lambda b, t: (0, 0)),
                  pl.BlockSpec((tm, LANES), lambda b, t: (t, 0)),
                  pl.BlockSpec((tm, LANES), lambda b, t: (t, 0))],
        out_specs=[row(d), row(d), row(d)],
        out_shape=[jax.ShapeDtypeStruct((nb, l, d), BF16)] * 3,
        compiler_params=_cparams(("parallel", "parallel")),
        name="odd_inproj",
    )(x, g, mods, w, cos, sin)


def _scores(q, k_ref, nchunks):
    kc = KV_CHUNK
    scores, run = [], None
    for c in range(nchunks):
        s = _dot_nt(q, k_ref[c * kc:(c + 1) * kc, :])
        scores.append(s)
        for j in range(kc // LANES):
            col = s[:, j * LANES:(j + 1) * LANES]
            run = col if run is None else jnp.maximum(run, col)
    return scores, jnp.max(run, axis=-1, keepdims=True)


def _softmax_pv(scores_max, v_ref, next_q=None, k_ref=None):
    scores, m = scores_max
    kc = KV_CHUNK
    acc = jnp.zeros((scores[0].shape[0], v_ref.shape[1]), F32)
    den = jnp.zeros((scores[0].shape[0], LANES), F32)
    nxt, run = [], None
    for c, s in enumerate(scores):
        if next_q is not None:
            sn = _dot_nt(next_q, k_ref[c * kc:(c + 1) * kc, :])
            nxt.append(sn)
            for j in range(kc // LANES):
                col = sn[:, j * LANES:(j + 1) * LANES]
                run = col if run is None else jnp.maximum(run, col)
        e = jnp.exp(s - m)
        for j in range(kc // LANES):
            den = den + e[:, j * LANES:(j + 1) * LANES]
        acc = acc + _dot(e.astype(BF16), v_ref[c * kc:(c + 1) * kc, :])
    den = jnp.sum(den, axis=-1, keepdims=True)
    if next_q is None:
        return acc, den
    return acc, den, (nxt, jnp.max(run, axis=-1, keepdims=True))


def _key_chunk_branches(qi, ctx_len, total, body):
    @pl.when(qi == 0)
    def _():
        body(ctx_len // KV_CHUNK)

    @pl.when(qi != 0)
    def _():
        body(total // KV_CHUNK)


def _gqa_kernel(q_ref, k_ref, v_ref, o_ref, *, ctx_len, tile_off):
    g = pl.program_id(1)
    qi = pl.program_id(2) + tile_off
    tq = q_ref.shape[0]
    half = lax.broadcasted_iota(I32, (tq, LANES), 1) // HEAD_DIM

    def head_q(i):
        qh = q_ref[:, (i // 2) * LANES:(i // 2 + 1) * LANES]
        qsel = jnp.where(half == (i % 2), qh, jnp.zeros_like(qh)).astype(F32)
        return jnp.where(g == (i % 2), qsel, pltpu.roll(qsel, HEAD_DIM, 1)).astype(BF16)

    def body(nchunks):
        cols = [jnp.zeros((tq, LANES), F32), jnp.zeros((tq, LANES), F32)]
        cur = _scores(head_q(0), k_ref, nchunks)
        for i in range(4):
            if i + 1 < 4:
                acc, den, cur = _softmax_pv(cur, v_ref, head_q(i + 1), k_ref)
            else:
                acc, den = _softmax_pv(cur, v_ref)
            o = acc / den
            osel = jnp.where(half == g, o, 0.0)
            cols[i // 2] = cols[i // 2] + jnp.where(g == (i % 2), osel, pltpu.roll(osel, HEAD_DIM, 1))
        o_ref[:, 0:LANES] = cols[0].astype(BF16)
        o_ref[:, LANES:2 * LANES] = cols[1].astype(BF16)

    if tile_off == 0:
        _key_chunk_branches(qi, ctx_len, k_ref.shape[0], body)
    else:
        body(k_ref.shape[0] // KV_CHUNK)


def _gqa(q, k, v, ctx_len, tile_off):
    nb, l, _ = q.shape
    tq = ROW_TILE
    nt = l // tq - tile_off
    return pl.pallas_call(
        functools.partial(_gqa_kernel, ctx_len=ctx_len, tile_off=tile_off),
        grid=(nb, 2, nt),
        in_specs=[pl.BlockSpec((None, tq, 2 * LANES), lambda b, g, t: (b, t + tile_off, g)),
                  pl.BlockSpec((None, l, LANES), lambda b, g, t: (b, 0, 0)),
                  pl.BlockSpec((None, l, LANES), lambda b, g, t: (b, 0, 0))],
        out_specs=pl.BlockSpec((None, tq, 2 * LANES), lambda b, g, t: (b, t + tile_off, g)),
        out_shape=jax.ShapeDtypeStruct((nb, l, 4 * LANES), BF16),
        compiler_params=_cparams(("parallel", "parallel", "parallel")),
        name="gqa_attention",
    )(q, k, v)


def _diff_kernel(q_ref, k_ref, v_ref, lam_ref, g_ref, o_ref, *, ctx_len, tile_off, lam_init):
    qi = pl.program_id(2) + tile_off
    tq = q_ref.shape[0]
    lf = lam_ref[...]
    lam = (jnp.exp(jnp.sum(lf[0:1, :] * lf[1:2, :], axis=-1, keepdims=True))
           - jnp.exp(jnp.sum(lf[2:3, :] * lf[3:4, :], axis=-1, keepdims=True)) + lam_init)
    half = lax.broadcasted_iota(I32, (tq, LANES), 1) // HEAD_DIM

    def body(nchunks):
        q = q_ref[...]
        zero = jnp.zeros_like(q)
        s1 = _scores(jnp.where(half == 0, q, zero), k_ref, nchunks)
        a1, d1, s2 = _softmax_pv(s1, v_ref, jnp.where(half == 1, q, zero), k_ref)
        a2, d2 = _softmax_pv(s2, v_ref)
        o = a1 / d1 - lam * (a2 / d2)
        o_ref[...] = (_rms(o) * g_ref[...] * (1.0 - lam_init)).astype(BF16)

    if tile_off == 0:
        _key_chunk_branches(qi, ctx_len, k_ref.shape[0], body)
    else:
        body(k_ref.shape[0] // KV_CHUNK)


def _diff_attention(q, k, v, lam_p, subln_g, ctx_len, tile_off, lam_init):
    nb, l, d = q.shape
    nh = d // LANES
    tq = ROW_TILE
    nt = l // tq - tile_off
    return pl.pallas_call(
        functools.partial(_diff_kernel, ctx_len=ctx_len, tile_off=tile_off, lam_init=lam_init),
        grid=(nb, nh, nt),
        in_specs=[pl.BlockSpec((None, tq, LANES), lambda b, h, t: (b, t + tile_off, h)),
                  pl.BlockSpec((None, l, LANES), lambda b, h, t: (b, 0, h)),
                  pl.BlockSpec((None, l, LANES), lambda b, h, t: (b, 0, h)),
                  pl.BlockSpec((4, HEAD_DIM), lambda b, h, t: (0, 0)),
                  pl.BlockSpec((1, LANES), lambda b, h, t: (0, 0))],
        out_specs=pl.BlockSpec((None, tq, LANES), lambda b, h, t: (b, t + tile_off, h)),
        out_shape=jax.ShapeDtypeStruct((nb, l, d), BF16),
        compiler_params=_cparams(("parallel", "parallel", "parallel")),
        name="diff_attention",
    )(q, k, v, lam_p, subln_g)


def _retention_kernel(rate_ref, qf_ref, kf_ref, vf_ref, qr_ref, kr_ref, vr_ref,
                      of_ref, or_ref, sf_ref, sr_ref):
    n = pl.program_id(1)
    c = RET_CHUNK
    nh = vf_ref.shape[1] // LANES

    @pl.when(n == 0)
    def _():
        sf_ref[...] = jnp.zeros_like(sf_ref)
        sr_ref[...] = jnp.zeros_like(sr_ref)

    half = lax.broadcasted_iota(I32, (c, LANES), 1) // HEAD_DIM
    ri = lax.broadcasted_iota(I32, (c, c), 0)
    ci = lax.broadcasted_iota(I32, (c, c), 1)
    pos = lax.broadcasted_iota(I32, (c, 1), 0).astype(F32)

    def one(direction, h, q_ref, k_ref, v_ref, o_ref, s_ref):
        qk = slice((h // 2) * LANES, (h // 2 + 1) * LANES)
        vo = slice(h * LANES, (h + 1) * LANES)
        head_ok = half == (h % 2)
        log_g = -jnp.exp(jnp.full((1, 1), rate_ref[direction, h], F32))
        rel = ((ri - ci) if direction == 0 else (ci - ri)).astype(F32)
        dmask = jnp.where(rel >= 0, jnp.exp(log_g * jnp.maximum(rel, 0.0)), 0.0)
        q = jnp.where(head_ok, q_ref[:, qk], jnp.zeros((c, LANES), BF16))
        k = jnp.where(head_ok, k_ref[:, qk], jnp.zeros((c, LANES), BF16))
        v = v_ref[:, vo]
        sc = _dot_nt(q, k) * dmask
        intra = _dot(sc.astype(BF16), v)
        q_pow = (pos + 1.0) if direction == 0 else (c - pos)
        k_pow = (c - 1.0 - pos) if direction == 0 else pos
        state = s_ref[h]
        cross = _dot((q.astype(F32) * jnp.exp(log_g * q_pow)).astype(BF16), state.astype(BF16))
        o_ref[:, vo] = intra + cross
        kd = (k.astype(F32) * jnp.exp(log_g * k_pow)).T.astype(BF16)
        s_ref[h] = jnp.exp(log_g * c) * state + _dot(kd, v)

    for h in range(nh):
        one(0, h, qf_ref, kf_ref, vf_ref, of_ref, sf_ref)
        one(1, h, qr_ref, kr_ref, vr_ref, or_ref, sr_ref)


def _retention(rates, q, k, v, ctx_len):
    nb, l, dv_all = v.shape
    c = RET_CHUNK
    nh = dv_all // LANES
    nctx = ctx_len // c
    ntot = l // c

    def rev(n):
        return jnp.where(n < nctx, nctx - 1 - n, ntot - 1 - (n - nctx))

    dqk = q.shape[2]
    qk_f = pl.BlockSpec((None, c, dqk), lambda b, n: (b, n, 0))
    qk_r = pl.BlockSpec((None, c, dqk), lambda b, n: (b, rev(n), 0))
    v_f = pl.BlockSpec((None, c, dv_all), lambda b, n: (b, n, 0))
    v_r = pl.BlockSpec((None, c, dv_all), lambda b, n: (b, rev(n), 0))
    return pl.pallas_call(
        _retention_kernel,
        grid=(nb, ntot),
        in_specs=[pl.BlockSpec(memory_space=pltpu.SMEM), qk_f, qk_f, v_f, qk_r, qk_r, v_r],
        out_specs=[v_f, v_r],
        out_shape=[jax.ShapeDtypeStruct((nb, l, dv_all), F32)] * 2,
        scratch_shapes=[pltpu.VMEM((nh, LANES, LANES), F32), pltpu.VMEM((nh, LANES, LANES), F32)],
        compiler_params=_cparams(("parallel", "arbitrary")),
        name="retention",
    )(rates, q, k, v, q, k, v)


def _finish_outproj(y, x_ref, mod_ref, g2_ref, x1_ref, h2_ref, h2t_ref):
    x1 = x_ref[...] + mod_ref[2:3, :] * y
    x1_ref[...] = x1
    h2 = (_rms(x1) * g2_ref[...]) * (1.0 + mod_ref[4:5, :]) + mod_ref[3:4, :]
    h2_ref[...] = h2
    h2t_ref[...] = h2.reshape(h2.shape[0], SUBLANES, LANES)


def _even_outproj_kernel(x_ref, ya_ref, of_ref, or_ref, gb_ref, w_ref, mod_ref, g2_ref, x1_ref, h2_ref, h2t_ref):
    half = ya_ref.shape[1]
    y = _dot(ya_ref[...], w_ref[0:half, :])
    for c in range(half // LANES):
        sl = slice(c * LANES, (c + 1) * LANES)
        gate = gb_ref[:, sl]
        r = _rms(of_ref[:, sl] + or_ref[:, sl]) * (gate * _sigmoid(gate))
        y = y + _dot(r.astype(BF16), w_ref[half + c * LANES:half + (c + 1) * LANES, :])
    _finish_outproj(y, x_ref, mod_ref, g2_ref, x1_ref, h2_ref, h2t_ref)


def _outproj_outputs(nb, l, d, tm, tile_off):
    row = pl.BlockSpec((None, tm, d), lambda b, t: (b, t + tile_off, 0))
    tiles = pl.BlockSpec((None, tm, SUBLANES, LANES), lambda b, t: (b, t + tile_off, 0, 0))
    shapes = [jax.ShapeDtypeStruct((nb, l, d), F32)] * 2 + [jax.ShapeDtypeStruct((nb, l, SUBLANES, LANES), F32)]
    return [row, row, tiles], shapes


def _even_outproj(x, ya, o_f, o_r, gb, w, mods, g2):
    nb, l, d = x.shape
    tm = ROW_TILE
    out_specs, out_shape = _outproj_outputs(nb, l, d, tm, 0)
    row = lambda width: pl.BlockSpec((None, tm, width), lambda b, t: (b, t, 0))
    return pl.pallas_call(
        _even_outproj_kernel,
        grid=(nb, l // tm),
        in_specs=[row(d), row(512), row(512), row(512), row(512),
                  pl.BlockSpec((d, d), lambda b, t: (0, 0)),
                  pl.BlockSpec((None, 6, d), _mod_index(nb)),
                  pl.BlockSpec((1, d), lambda b, t: (0, 0))],
        out_specs=out_specs,
        out_shape=out_shape,
        compiler_params=_cparams(("parallel", "parallel")),
        name="even_outproj",
    )(x, ya, o_f, o_r, gb, w, mods, g2)


def _odd_outproj_kernel(x_ref, mix_ref, w_ref, mod_ref, g2_ref, x1_ref, h2_ref, h2t_ref):
    _finish_outproj(_dot(mix_ref[...], w_ref[...]), x_ref, mod_ref, g2_ref, x1_ref, h2_ref, h2t_ref)


def _odd_outproj(x, mix, w, mods, g2, tile_off):
    nb, l, d = x.shape
    tm = ROW_TILE
    out_specs, out_shape = _outproj_outputs(nb, l, d, tm, tile_off)
    row = pl.BlockSpec((None, tm, d), lambda b, t: (b, t + tile_off, 0))
    return pl.pallas_call(
        _odd_outproj_kernel,
        grid=(nb, l // tm - tile_off),
        in_specs=[row, row,
                  pl.BlockSpec((d, d), lambda b, t: (0, 0)),
                  pl.BlockSpec((None, 6, d), lambda b, t: (b, 0, 0)),
                  pl.BlockSpec((1, d), lambda b, t: (0, 0))],
        out_specs=out_specs,
        out_shape=out_shape,
        compiler_params=_cparams(("parallel", "parallel")),
        name="odd_outproj",
    )(x, mix, w, mods, g2)


def _top_rows(s, k, payload=None):
    rows = lax.broadcasted_iota(I32, s.shape, 0).astype(F32)
    n = float(s.shape[0])
    vals, pays = [], []
    for _ in range(k):
        m = jnp.max(s, axis=0, keepdims=True)
        am = jnp.min(jnp.where(s == m, rows, n), axis=0, keepdims=True)
        hit = rows == am
        vals.append(m)
        pays.append(am if payload is None else jnp.max(jnp.where(hit, payload, -1.0), axis=0, keepdims=True))
        s = jnp.where(hit, -jnp.inf, s)
    return jnp.concatenate(vals, axis=0), jnp.concatenate(pays, axis=0)


def _pair_candidates(s1, i1, s2, i2):
    kk = PEER_TOPK
    sub = lax.broadcasted_iota(I32, (SUBLANES, s1.shape[1]), 0)
    cand = [s1[0:1, :] + s2]
    cid = [i1[0:1, :] * PEER_NKEYS + i2]
    for a in range(1, SUBLANES):
        ok = sub < kk // (a + 1)
        cand.append(jnp.where(ok, s1[a:a + 1, :] + s2[0:SUBLANES, :], -jnp.inf))
        cid.append(i1[a:a + 1, :] * PEER_NKEYS + i2[0:SUBLANES, :])
    cand.append(s1[SUBLANES:kk, :] + s2[0:1, :])
    cid.append(i1[SUBLANES:kk, :] * PEER_NKEYS + i2[0:1, :])
    return jnp.concatenate(cand, axis=0), jnp.concatenate(cid, axis=0)


def _compress_rows(sel, skip, arrays):
    n = sel.shape[0]
    cc = jnp.where(sel, skip, 0)
    vals = [jnp.where(sel, a, jnp.zeros_like(a)) for a in arrays]
    for k in range(n.bit_length() - 1):
        mv = ((cc >> k) & 1) == 1

        def step(a):
            zero = jnp.zeros_like(a)
            return jnp.where(mv, zero, a) + pltpu.roll(jnp.where(mv, a, zero), n - (1 << k), 0)

        vals = [step(a) for a in vals]
        cc = step(cc)
    return vals


def _peer_topk_kernel(h_ref, wq_ref, keys_ref, off_ref, gate_ref, ngrp_ref, *, half_rows):
    kk = PEER_TOPK
    q = _dot(h_ref[...].astype(BF16), wq_ref[...])
    eids, gates = [], []
    for hd in range(PEER_HEADS):
        tops = []
        for part in range(2):
            c = 2 * hd + part
            qs = q[:, c * LANES:(c + 1) * LANES].astype(BF16)
            tops.append(_top_rows(_dot_nt(keys_ref[c], qs), kk))
        (s1, i1), (s2, i2) = tops
        cand, cid = _pair_candidates(s1, i1, s2, i2)
        best, eid = _top_rows(cand, kk, cid)
        e = jnp.exp(best - best[0:1, :])
        eids.append(eid)
        gates.append(e / jnp.sum(e, axis=0, keepdims=True))
    eid = jnp.concatenate(eids, axis=0).astype(I32)
    gate = jnp.concatenate(gates, axis=0)
    npick, tm = eid.shape
    top = eid // half_rows
    upper = ((top ^ eid) & 1) == 1
    off = (top * (half_rows // 2) + (eid & (half_rows - 1)) // 2) * SUBLANES
    row = lax.broadcasted_iota(I32, (npick, tm), 0)
    before = (lax.broadcasted_iota(I32, (npick, npick), 1) < lax.broadcasted_iota(I32, (npick, npick), 0))
    upper_before = _dot(before.astype(BF16), upper.astype(BF16)).astype(I32)
    for hf, (sel, skip) in enumerate(((~upper, upper_before), (upper, row - upper_before))):
        off_c, gate_c = _compress_rows(sel, skip, [off, gate])
        off_ref[hf] = off_c.T
        gate_ref[hf] = gate_c.T
        count = jnp.sum(sel.astype(F32), axis=0, keepdims=True)
        ngrp = ((count + (SUBLANES - 1)) * (1.0 / SUBLANES)).astype(I32)
        ngrp_ref[hf] = jnp.broadcast_to(ngrp, (npick, tm)).T


def _flat_tile(nt_all, nt, tile_off):
    return lambda i: (i // nt) * nt_all + (i % nt) + tile_off


def _peer_topk(h2, wq, keys, half_rows, nb, tile_off):
    n, d = h2.shape
    tm = ROW_TILE
    nt_all = n // nb // tm
    nt = nt_all - tile_off
    ft = _flat_tile(nt_all, nt, tile_off)
    npick = PEER_HEADS * PEER_TOPK
    out = pl.BlockSpec((2, tm, npick), lambda i: (0, ft(i), 0))
    return pl.pallas_call(
        functools.partial(_peer_topk_kernel, half_rows=half_rows),
        grid=(nb * nt,),
        in_specs=[pl.BlockSpec((tm, d), lambda i: (ft(i), 0)),
                  pl.BlockSpec(wq.shape, lambda i: (0, 0)),
                  pl.BlockSpec(keys.shape, lambda i: (0, 0, 0))],
        out_specs=[out, out, out],
        out_shape=[jax.ShapeDtypeStruct((2, n, npick), I32), jax.ShapeDtypeStruct((2, n, npick), F32),
                   jax.ShapeDtypeStruct((2, n, npick), I32)],
        compiler_params=_cparams(("parallel",)),
        name="peer_topk",
    )(h2, wq, keys)


def _split_table(tab):
    nexp, d = tab.shape
    t = tab.reshape(2, nexp // 4, 2, d)
    halves = [jnp.stack([t[0, :, h], t[1, :, 1 - h]]) for h in range(2)]
    return jnp.stack(halves).reshape(nexp * d // LANES, LANES)


def _load_table_half(tab_hbm, tab_ref, sem):
    hf = pl.program_id(0)
    half_rows = tab_ref.shape[0]

    @pl.when(pl.program_id(1) == 0)
    def _():
        cp = pltpu.make_async_copy(tab_hbm.at[pl.ds(hf * half_rows, half_rows)], tab_ref, sem)
        cp.start()
        cp.wait()


def _table_row(tab_ref, off):
    return tab_ref[pl.ds(pl.multiple_of(off, SUBLANES), SUBLANES), :]


def _sublane_sum_masks():
    sub = lax.broadcasted_iota(I32, (SUBLANES, LANES), 0)
    return ((sub + 2) % 8 < 4,
            (sub + 1) % 8 < 4,
            sub < 4,
            (sub + 7) % 8 < 4,
            sub % 4 < 2,
            (sub + 3) % 4 < 2,
            sub % 2 == 1)


def _sublane_sums(prods, masks):
    def merge(a, b, m, shift):
        return jnp.where(m, a, b) + pltpu.roll(jnp.where(m, b, a), shift, 0)

    s0 = merge(prods[0], prods[4], masks[0], 4)
    s1 = merge(prods[1], prods[5], masks[1], 4)
    s2 = merge(prods[2], prods[6], masks[2], 4)
    s3 = merge(prods[3], prods[7], masks[3], 4)
    u0 = merge(s0, s2, masks[4], 2)
    u1 = merge(s1, s3, masks[5], 2)
    return merge(u0, u1, masks[6], 1)


def _peer_dot_kernel(iota_ref, off_ref, ngrp_ref, gate_ref, x_ref, tab_hbm, d_ref, tab_ref, q_ref, sem):
    _load_table_half(tab_hbm, tab_ref, sem)
    npick, tb = d_ref.shape
    xr = PEER_XLU_GROUPS * SUBLANES
    lane_lo = lax.broadcasted_iota(I32, (xr, tb), 1)
    lane_hi = lax.broadcasted_iota(I32, (npick - xr, tb), 1)
    masks = _sublane_sum_masks()
    ks = [iota_ref[k] for k in range(SUBLANES)]

    @pl.when(pl.program_id(1) == 0)
    def _():
        def clear(t, carry):
            q_ref[t] = jnp.zeros((npick, LANES), F32)
            return carry
        lax.fori_loop(0, tb, clear, 0)

    def gather(t, carry):
        x = x_ref[t]
        base = pl.multiple_of(t * npick, npick)

        def group(start):
            offs = off_ref.at[pl.ds(base + start, SUBLANES)]
            prods = [_table_row(tab_ref, offs[ks[(k + 1) % SUBLANES]]) * x for k in range(SUBLANES)]
            q_ref[t, pl.ds(start, SUBLANES), :] = _sublane_sums(prods, masks)

        for g in range(PEER_STATIC_GROUPS):
            group(g * SUBLANES)

        def extra(g, c):
            group(pl.multiple_of(g * SUBLANES, SUBLANES))
            return c

        lax.fori_loop(PEER_STATIC_GROUPS, ngrp_ref[base], extra, 0)
        return carry

    def lane_reduce(t, acc):
        col = jnp.sum(q_ref[t, 0:xr, :], axis=1, keepdims=True)
        return jnp.where(lane_lo == t, col, acc)

    def lane_reduce_rest(t, carry):
        @pl.when(ngrp_ref[t * npick] > PEER_XLU_GROUPS)
        def _():
            col = jnp.sum(q_ref[t, xr:npick, :], axis=1, keepdims=True)
            d_ref[xr:npick, :] = jnp.where(lane_hi == t, col, d_ref[xr:npick, :])
        return carry

    lax.fori_loop(0, tb, gather, 0)
    d_ref[0:xr, :] = lax.fori_loop(0, tb, lane_reduce, jnp.zeros((xr, tb), F32), unroll=PEER_XLU_UNROLL)
    d_ref[xr:npick, :] = jnp.zeros((npick - xr, tb), F32)

    @pl.when(jnp.max(gate_ref[:, xr:npick]) > 0.0)
    def _():
        lax.fori_loop(0, tb, lane_reduce_rest, 0)


def _peer_sum_kernel(iota_ref, off_ref, ngrp_ref, gate_ref, d_ref, tab_hbm, f_ref, tab_ref, wl_ref, sem):
    _load_table_half(tab_hbm, tab_ref, sem)
    tb, npick = gate_ref.shape
    xr = PEER_XLU_GROUPS * SUBLANES
    nacc = 4
    ks = [iota_ref[k] for k in range(SUBLANES)]
    gate = gate_ref[...].T
    d = d_ref[...]
    act = 0.5 * d * (1.0 + lax.erf(d * (2.0 ** -0.5)))
    wt = jnp.where(gate > 0.0, gate * act, 0.0)
    lane_lo = lax.broadcasted_iota(I32, (xr, tb), 1)
    lane_hi = lax.broadcasted_iota(I32, (npick - xr, tb), 1)

    def spread(t, carry):
        col = jnp.sum(jnp.where(lane_lo == t, wt[0:xr], 0.0), axis=1, keepdims=True)
        wl_ref[t, 0:xr, :] = jnp.broadcast_to(col, (xr, LANES))
        return carry

    def spread_rest(t, carry):
        @pl.when(ngrp_ref[t * npick] > PEER_XLU_GROUPS)
        def _():
            col = jnp.sum(jnp.where(lane_hi == t, wt[xr:npick], 0.0), axis=1, keepdims=True)
            wl_ref[t, xr:npick, :] = jnp.broadcast_to(col, (npick - xr, LANES))
        return carry

    def token(t, carry):
        base = pl.multiple_of(t * npick, npick)

        def group(start, accs):
            accs = list(accs)
            offs = off_ref.at[pl.ds(base + start, SUBLANES)]
            for k in range(SUBLANES):
                wv = jnp.broadcast_to(wl_ref[t, pl.ds(start + k, 1), :], (SUBLANES, LANES))
                accs[k % nacc] = accs[k % nacc] + wv * _table_row(tab_ref, offs[ks[k]])
            return tuple(accs)

        accs = tuple(jnp.zeros((SUBLANES, LANES), F32) for _ in range(nacc))
        for g in range(PEER_STATIC_GROUPS):
            accs = group(g * SUBLANES, accs)
        accs = lax.fori_loop(PEER_STATIC_GROUPS, ngrp_ref[base],
                             lambda g, a: group(pl.multiple_of(g * SUBLANES, SUBLANES), a), accs)
        f_ref[t] = (accs[0] + accs[1]) + (accs[2] + accs[3])
        return carry

    lax.fori_loop(0, tb, spread, 0, unroll=PEER_XLU_UNROLL)

    @pl.when(jnp.max(gate_ref[:, xr:npick]) > 0.0)
    def _():
        lax.fori_loop(0, tb, spread_rest, 0)

    lax.fori_loop(0, tb, token, 0)


def _peer_tiles(n, nb, tile_off_rows):
    tb = PEER_TOK
    nt_all = n // nb // tb
    off = tile_off_rows // tb
    nt = nt_all - off
    return tb, nb * nt, _flat_tile(nt_all, nt, off)


def _smem_token_block(tb, npick, blocks_per_half, ft):
    return pl.BlockSpec((tb * npick,), lambda hf, i: (hf * blocks_per_half + ft(i),), memory_space=pltpu.SMEM)


def _peer_dot(off_flat, ngrp_flat, gate2, h2v, tab2, nb, tile_off_rows):
    _, n, npick = gate2.shape
    tb, steps, ft = _peer_tiles(n, nb, tile_off_rows)
    half = tab2.shape[0] // 2
    smem_blk = _smem_token_block(tb, npick, n // tb, ft)
    return pl.pallas_call(
        _peer_dot_kernel,
        grid=(2, steps),
        in_specs=[pl.BlockSpec(memory_space=pltpu.SMEM), smem_blk, smem_blk,
                  pl.BlockSpec((None, tb, npick), lambda hf, i: (hf, ft(i), 0)),
                  pl.BlockSpec((tb, SUBLANES, LANES), lambda hf, i: (ft(i), 0, 0)),
                  pl.BlockSpec(memory_space=pl.ANY)],
        out_specs=pl.BlockSpec((None, npick, tb), lambda hf, i: (hf, 0, ft(i))),
        out_shape=jax.ShapeDtypeStruct((2, npick, n), F32),
        scratch_shapes=[pltpu.VMEM((half, LANES), F32),
                        pltpu.VMEM((tb, npick, LANES), F32),
                        pltpu.SemaphoreType.DMA],
        compiler_params=_cparams(("arbitrary", "arbitrary"), VMEM_LIMIT_TABLE),
        name="peer_dot",
    )(jnp.arange(SUBLANES, dtype=I32), off_flat, ngrp_flat, gate2, h2v, tab2)


def _peer_sum(off_flat, ngrp_flat, gate2, dpart, tab2, nb, tile_off_rows):
    _, n, npick = gate2.shape
    tb, steps, ft = _peer_tiles(n, nb, tile_off_rows)
    half = tab2.shape[0] // 2
    smem_blk = _smem_token_block(tb, npick, n // tb, ft)
    return pl.pallas_call(
        _peer_sum_kernel,
        grid=(2, steps),
        in_specs=[pl.BlockSpec(memory_space=pltpu.SMEM), smem_blk, smem_blk,
                  pl.BlockSpec((None, tb, npick), lambda hf, i: (hf, ft(i), 0)),
                  pl.BlockSpec((None, npick, tb), lambda hf, i: (hf, 0, ft(i))),
                  pl.BlockSpec(memory_space=pl.ANY)],
        out_specs=pl.BlockSpec((None, tb, SUBLANES, LANES), lambda hf, i: (hf, ft(i), 0, 0)),
        out_shape=jax.ShapeDtypeStruct((2, n, SUBLANES, LANES), F32),
        scratch_shapes=[pltpu.VMEM((half, LANES), F32),
                        pltpu.VMEM((tb, npick, LANES), F32),
                        pltpu.SemaphoreType.DMA],
        compiler_params=_cparams(("arbitrary", "arbitrary"), VMEM_LIMIT_TABLE),
        name="peer_sum",
    )(jnp.arange(SUBLANES, dtype=I32), off_flat, ngrp_flat, gate2, dpart, tab2)


def _mixer_out(f_ref, shape):
    return (f_ref[0] + f_ref[1]).reshape(shape)


def _residual_kernel(x_ref, f_ref, mod_ref, o_ref):
    o_ref[...] = x_ref[...] + mod_ref[5:6, :] * _mixer_out(f_ref, o_ref.shape)


def _final_kernel(x_ref, f_ref, mod_ref, g_ref, o_ref):
    o_ref[...] = _rms(x_ref[...] + mod_ref[5:6, :] * _mixer_out(f_ref, o_ref.shape)) * g_ref[...]


def _residual(x1, f2, mods):
    nb, l, d = x1.shape
    tm = ROW_TILE
    return pl.pallas_call(
        _residual_kernel,
        grid=(nb, l // tm),
        in_specs=[pl.BlockSpec((None, tm, d), lambda b, t: (b, t, 0)),
                  pl.BlockSpec((2, None, tm, SUBLANES, LANES), lambda b, t: (0, b, t, 0, 0)),
                  pl.BlockSpec((None, 6, d), _mod_index(nb))],
        out_specs=pl.BlockSpec((None, tm, d), lambda b, t: (b, t, 0)),
        out_shape=jax.ShapeDtypeStruct((nb, l, d), F32),
        compiler_params=_cparams(("parallel", "parallel")),
        name="peer_residual",
    )(x1, f2, mods)


def _final(x1, f2, mods, g, tile_off):
    nb, l, d = x1.shape
    tm = ROW_TILE
    nt = l // tm - tile_off
    return pl.pallas_call(
        _final_kernel,
        grid=(nb, nt),
        in_specs=[pl.BlockSpec((None, tm, d), lambda b, t: (b, t + tile_off, 0)),
                  pl.BlockSpec((2, None, tm, SUBLANES, LANES), lambda b, t: (0, b, t + tile_off, 0, 0)),
                  pl.BlockSpec((None, 6, d), lambda b, t: (b, 0, 0)),
                  pl.BlockSpec((1, d), lambda b, t: (0, 0))],
        out_specs=pl.BlockSpec((None, tm, d), lambda b, t: (b, t, 0)),
        out_shape=jax.ShapeDtypeStruct((nb, nt * tm, d), F32),
        compiler_params=_cparams(("parallel", "parallel")),
        name="final_norm",
    )(x1, f2, mods, g)


def _rope_tables(seq, ctx_len):
    t = jnp.arange(seq)
    row_id = (t // GRID_W).astype(F32)
    col_id = (t % GRID_W).astype(F32)
    axis_dim = HEAD_DIM // 2
    inv = ROPE_THETA ** (-jnp.arange(0, axis_dim, 2, dtype=F32) / axis_dim)
    ang_r = row_id[:, None] * inv[None, :]
    ang_c = col_id[:, None] * inv[None, :]
    cos = jnp.concatenate([jnp.cos(ang_r)] * 2 + [jnp.cos(ang_c)] * 2, axis=-1)
    sin = jnp.concatenate([-jnp.sin(ang_r), jnp.sin(ang_r), -jnp.sin(ang_c), jnp.sin(ang_c)], axis=-1)
    cos = jnp.concatenate([jnp.ones((ctx_len, HEAD_DIM), F32), cos], axis=0)
    sin = jnp.concatenate([jnp.zeros((ctx_len, HEAD_DIM), F32), sin], axis=0)
    return jnp.tile(cos, (1, 2)), jnp.tile(sin, (1, 2))


def _peer(h2, h2t, x1, mods, wq, keys, u_tab, v_tab, nb, tile_off_rows, final_g):
    b, l, d = x1.shape
    n = b * l
    nexp = u_tab.shape[0]
    npick = PEER_HEADS * PEER_TOPK
    off2, gate2, ngrp2 = _peer_topk(h2.reshape(n, d), wq.astype(BF16),
                                    keys.reshape(PEER_HEADS * 2, PEER_NKEYS, -1).astype(BF16),
                                    nexp // 2, nb, tile_off_rows // ROW_TILE)
    off_flat = off2.reshape(2 * n * npick)
    ngrp_flat = ngrp2.reshape(2 * n * npick)
    h2v = h2t.reshape(n, SUBLANES, LANES)
    dpart = _peer_dot(off_flat, ngrp_flat, gate2, h2v, _split_table(u_tab), nb, tile_off_rows)
    f2 = _peer_sum(off_flat, ngrp_flat, gate2, dpart, _split_table(v_tab), nb, tile_off_rows)
    f2 = f2.reshape(2, b, l, SUBLANES, LANES)
    if final_g is None:
        return _residual(x1, f2, mods)
    return _final(x1, f2, mods, final_g.reshape(1, d), tile_off_rows // ROW_TILE)


def kernel(x, c, ctx, c_ctx, ada_w, ada_b, norm1_g, norm2_g, ev_w_in, ev_w_out, gqa_q_norm_g,
           gqa_k_norm_g, ret_log_rate, od_w_in, od_w_out, diff_lambda, diff_subln_g, peer_w_q,
           peer_keys, peer_u, peer_v, final_g):
    nb, seq, d = x.shape
    ctx_len = ctx.shape[1]
    depth = ada_w.shape[0]
    assert ctx_len == ROW_TILE and seq % ROW_TILE == 0 and d == SUBLANES * LANES
    xs = jnp.concatenate([ctx, x], axis=1)
    cos, sin = _rope_tables(seq, ctx_len)
    cc = jnp.concatenate([c, c_ctx[None, :]], axis=0)
    for layer in range(depth):
        last = layer == depth - 1
        mods = _ada(cc, ada_w[layer], ada_b[layer]).reshape(nb + 1, 6, d)
        g1 = norm1_g[layer].reshape(1, d)
        g2 = norm2_g[layer].reshape(1, d)
        lat_off = 1 if last else 0
        if layer % 2 == 0:
            e = layer // 2
            qa, ka, va, qb, kb, vb, gb = _even_inproj(
                xs, g1, mods, ev_w_in[e].astype(BF16), cos, sin,
                jnp.tile(gqa_q_norm_g[e], 2).reshape(1, LANES), jnp.tile(gqa_k_norm_g[e], 2).reshape(1, LANES))
            ya = _gqa(qa, ka, va, ctx_len, 0)
            o_f, o_r = _retention(ret_log_rate[e], qb, kb, vb, ctx_len)
            x1, h2, h2t = _even_outproj(xs, ya, o_f, o_r, gb, ev_w_out[e].astype(BF16), mods, g2)
        else:
            o = layer // 2
            lam_init = 0.8 - 0.6 * math.exp(-0.3 * layer)
            q, k, v = _odd_inproj(xs, g1, mods, od_w_in[o].astype(BF16), cos, sin)
            mix = _diff_attention(q, k, v, diff_lambda[o], diff_subln_g[o].reshape(1, LANES),
                                  ctx_len, lat_off, lam_init)
            x1, h2, h2t = _odd_outproj(xs, mix, od_w_out[o].astype(BF16), mods, g2, lat_off)
        xs = _peer(h2, h2t, x1, mods, peer_w_q[layer], peer_keys[layer], peer_u[layer], peer_v[layer],
                   nb, lat_off * ROW_TILE, final_g if last else None)
    return xs
```

```python
import functools
import math

import jax
import jax.numpy as jnp
from jax import lax
from jax.experimental import pallas as pl
from jax.experimental.pallas import tpu as pltpu

F32 = jnp.float32
BF16 = jnp.bfloat16
I32 = jnp.int32

LANES = 128
SUBLANES = 8
HEAD_DIM = 64
GRID_W = 64
ROPE_THETA = 10000.0
NORM_EPS = 1e-6
RET_CHUNK = 128
PEER_NKEYS = 128
PEER_TOPK = 16
PEER_HEADS = 8
ROW_TILE = 256
PEER_TOK = 256
PEER_STATIC_GROUPS = 9
PEER_XLU_GROUPS = 11
PEER_XLU_UNROLL = 64
KV_CHUNK = 256
VMEM_LIMIT = 48 * 1024 * 1024
VMEM_LIMIT_TABLE = 56 * 1024 * 1024


def _cparams(sem, limit=VMEM_LIMIT):
    return pltpu.CompilerParams(dimension_semantics=sem, vmem_limit_bytes=limit)


def _rms(x):
    return x * lax.rsqrt(jnp.mean(x * x, axis=-1, keepdims=True) + NORM_EPS)


def _dot(a, b):
    return jnp.dot(a, b, preferred_element_type=F32)


def _dot_nt(a, b):
    return lax.dot_general(a, b, (((1,), (1,)), ((), ())), preferred_element_type=F32)


def _sigmoid(x):
    return 1.0 / (1.0 + jnp.exp(-x))


def _ada_kernel(c_ref, w_ref, b_ref, o_ref):
    c = c_ref[...]
    a = (c * _sigmoid(c)).astype(BF16)
    o_ref[...] = _dot(a, w_ref[...].astype(BF16)) + b_ref[...]


def _ada(cc, w, b):
    m, d = cc.shape
    n = w.shape[1]
    tn = 1024
    return pl.pallas_call(
        _ada_kernel,
        grid=(n // tn,),
        in_specs=[pl.BlockSpec((m, d), lambda j: (0, 0)),
                  pl.BlockSpec((d, tn), lambda j: (0, j)),
                  pl.BlockSpec((1, tn), lambda j: (0, j))],
        out_specs=pl.BlockSpec((m, tn), lambda j: (0, j)),
        out_shape=jax.ShapeDtypeStruct((m, n), F32),
        compiler_params=_cparams(("arbitrary",)),
        name="ada_mod",
    )(cc, w, b.reshape(1, n))


def _rope_cols(x, cos, sin_signed):
    lane = lax.broadcasted_iota(I32, x.shape, 1)
    partner = jnp.where((lane % 32) < 16, pltpu.roll(x, LANES - 16, 1), pltpu.roll(x, 16, 1))
    return x * cos + partner * sin_signed


def _head_mean_sq(x):
    r = lax.broadcasted_iota(I32, (LANES, LANES), 0) // HEAD_DIM
    c = lax.broadcasted_iota(I32, (LANES, LANES), 1) // HEAD_DIM
    ones_bd = jnp.where(r == c, 1.0, 0.0).astype(BF16)
    sq = x * x
    hi = sq.astype(BF16)
    lo = (sq - hi.astype(F32)).astype(BF16)
    return (_dot(hi, ones_bd) + _dot(lo, ones_bd)) * (1.0 / HEAD_DIM)


def _prologue(x_ref, g_ref, mod_ref, shift_row, scale_row):
    h = _rms(x_ref[...]) * g_ref[...]
    return h * (1.0 + mod_ref[scale_row:scale_row + 1, :]) + mod_ref[shift_row:shift_row + 1, :]


def _even_inproj_kernel(x_ref, g_ref, mod_ref, w_ref, cos_ref, sin_ref, qg_ref, kg_ref,
                        qa_ref, ka_ref, va_ref, qb_ref, kb_ref, vb_ref, gb_ref):
    h = _prologue(x_ref, g_ref, mod_ref, 0, 1).astype(BF16)
    y = _dot(h, w_ref[...])
    cos = cos_ref[...]
    sin = sin_ref[...]
    scale = HEAD_DIM ** -0.5

    def col(c):
        return y[:, c * LANES:(c + 1) * LANES]

    for c in range(4):
        x = col(c)
        x = x * lax.rsqrt(_head_mean_sq(x) + NORM_EPS) * qg_ref[...]
        qa_ref[:, c * LANES:(c + 1) * LANES] = (_rope_cols(x, cos, sin) * scale).astype(BF16)
    x = col(4)
    x = x * lax.rsqrt(_head_mean_sq(x) + NORM_EPS) * kg_ref[...]
    ka_ref[...] = _rope_cols(x, cos, sin).astype(BF16)
    va_ref[...] = col(5).astype(BF16)
    for c in range(2):
        qb_ref[:, c * LANES:(c + 1) * LANES] = _rope_cols(col(6 + c), cos, sin).astype(BF16)
        kb_ref[:, c * LANES:(c + 1) * LANES] = _rope_cols(col(8 + c) * scale, cos, sin).astype(BF16)
    vb_ref[...] = y[:, 10 * LANES:14 * LANES].astype(BF16)
    gb_ref[...] = y[:, 14 * LANES:18 * LANES]


def _mod_index(nb):
    return lambda b, t: (jnp.where(t == 0, nb, b), 0, 0)


def _even_inproj(x, g, mods, w, cos, sin, qg, kg):
    nb, l, d = x.shape
    tm = ROW_TILE
    n_in = w.shape[1]
    row = lambda width: pl.BlockSpec((None, tm, width), lambda b, t: (b, t, 0))
    widths = (512, 128, 128, 256, 256, 512, 512)
    dts = (BF16, BF16, BF16, BF16, BF16, BF16, F32)
    return pl.pallas_call(
        _even_inproj_kernel,
        grid=(nb, l // tm),
        in_specs=[row(d),
                  pl.BlockSpec((1, d), lambda b, t: (0, 0)),
                  pl.BlockSpec((None, 6, d), _mod_index(nb)),
                  pl.BlockSpec((d, n_in), lambda b, t: (0, 0)),
                  pl.BlockSpec((tm, LANES), lambda b, t: (t, 0)),
                  pl.BlockSpec((tm, LANES), lambda b, t: (t, 0)),
                  pl.BlockSpec((1, LANES), lambda b, t: (0, 0)),
                  pl.BlockSpec((1, LANES), lambda b, t: (0, 0))],
        out_specs=[row(wd) for wd in widths],
        out_shape=[jax.ShapeDtypeStruct((nb, l, wd), dt) for wd, dt in zip(widths, dts)],
        compiler_params=_cparams(("parallel", "parallel")),
        name="even_inproj",
    )(x, g, mods, w, cos, sin, qg, kg)


def _odd_inproj_kernel(x_ref, g_ref, mod_ref, w_ref, cos_ref, sin_ref, q_ref, k_ref, v_ref):
    h = _prologue(x_ref, g_ref, mod_ref, 0, 1).astype(BF16)
    y = _dot(h, w_ref[...])
    cos = cos_ref[...]
    sin = sin_ref[...]
    scale = HEAD_DIM ** -0.5
    for c in range(8):
        sl = slice(c * LANES, (c + 1) * LANES)
        q_ref[:, sl] = (_rope_cols(y[:, sl], cos, sin) * scale).astype(BF16)
        k_ref[:, sl] = _rope_cols(y[:, 8 * LANES + c * LANES:8 * LANES + (c + 1) * LANES], cos, sin).astype(BF16)
    v_ref[...] = y[:, 16 * LANES:24 * LANES].astype(BF16)


def _odd_inproj(x, g, mods, w, cos, sin):
    nb, l, d = x.shape
    tm = ROW_TILE
    n_in = w.shape[1]
    row = lambda width: pl.BlockSpec((None, tm, width), lambda b, t: (b, t, 0))
    return pl.pallas_call(
        _odd_inproj_kernel,
        grid=(nb, l // tm),
        in_specs=[row(d),
                  pl.BlockSpec((1, d), lambda b, t: (0, 0)),
                  pl.BlockSpec((None, 6, d), _mod_index(nb)),
                  pl.BlockSpec((d, n_in), lambda b, t: (0, 0)),
                  pl.BlockSpec((tm, LANES), lambda b, t: (t, 0)),
                  pl.BlockSpec((tm, LANES), lambda b, t: (t, 0))],
        out_specs=[row(d), row(d), row(d)],
        out_shape=[jax.ShapeDtypeStruct((nb, l, d), BF16)] * 3,
        compiler_params=_cparams(("parallel", "parallel")),
        name="odd_inproj",
    )(x, g, mods, w, cos, sin)


def _scores(q, k_ref, nchunks):
    kc = KV_CHUNK
    scores, run = [], None
    for c in range(nchunks):
        s = _dot_nt(q, k_ref[c * kc:(c + 1) * kc, :])
        scores.append(s)
        for j in range(kc // LANES):
            col = s[:, j * LANES:(j + 1) * LANES]
            run = col if run is None else jnp.maximum(run, col)
    return scores, jnp.max(run, axis=-1, keepdims=True)


def _softmax_pv(scores_max, v_ref, next_q=None, k_ref=None):
    scores, m = scores_max
    kc = KV_CHUNK
    acc = jnp.zeros((scores[0].shape[0], v_ref.shape[1]), F32)
    den = jnp.zeros((scores[0].shape[0], LANES), F32)
    nxt, run = [], None
    for c, s in enumerate(scores):
        if next_q is not None:
            sn = _dot_nt(next_q, k_ref[c * kc:(c + 1) * kc, :])
            nxt.append(sn)
            for j in range(kc // LANES):
                col = sn[:, j * LANES:(j + 1) * LANES]
                run = col if run is None else jnp.maximum(run, col)
        e = jnp.exp(s - m)
        for j in range(kc // LANES):
            den = den + e[:, j * LANES:(j + 1) * LANES]
        acc = acc + _dot(e.astype(BF16), v_ref[c * kc:(c + 1) * kc, :])
    den = jnp.sum(den, axis=-1, keepdims=True)
    if next_q is None:
        return acc, den
    return acc, den, (nxt, jnp.max(run, axis=-1, keepdims=True))


def _key_chunk_branches(qi, ctx_len, total, body):
    @pl.when(qi == 0)
    def _():
        body(ctx_len // KV_CHUNK)

    @pl.when(qi != 0)
    def _():
        body(total // KV_CHUNK)


def _gqa_kernel(q_ref, k_ref, v_ref, o_ref, *, ctx_len, tile_off):
    g = pl.program_id(1)
    qi = pl.program_id(2) + tile_off
    tq = q_ref.shape[0]
    half = lax.broadcasted_iota(I32, (tq, LANES), 1) // HEAD_DIM

    def head_q(i):
        qh = q_ref[:, (i // 2) * LANES:(i // 2 + 1) * LANES]
        qsel = jnp.where(half == (i % 2), qh, jnp.zeros_like(qh)).astype(F32)
        return jnp.where(g == (i % 2), qsel, pltpu.roll(qsel, HEAD_DIM, 1)).astype(BF16)

    def body(nchunks):
        cols = [jnp.zeros((tq, LANES), F32), jnp.zeros((tq, LANES), F32)]
        q_all = jnp.concatenate([head_q(i) for i in range(4)], axis=0)
        acc_all, den_all = _softmax_pv(_scores(q_all, k_ref, nchunks), v_ref)
        o_all = acc_all / den_all
        for i in range(4):
            o = o_all[i * tq:(i + 1) * tq, :]
            osel = jnp.where(half == g, o, 0.0)
            cols[i // 2] = cols[i // 2] + jnp.where(g == (i % 2), osel, pltpu.roll(osel, HEAD_DIM, 1))
        o_ref[:, 0:LANES] = cols[0].astype(BF16)
        o_ref[:, LANES:2 * LANES] = cols[1].astype(BF16)

    if tile_off == 0:
        _key_chunk_branches(qi, ctx_len, k_ref.shape[0], body)
    else:
        body(k_ref.shape[0] // KV_CHUNK)


def _gqa(q, k, v, ctx_len, tile_off):
    nb, l, _ = q.shape
    tq = ROW_TILE
    nt = l // tq - tile_off
    return pl.pallas_call(
        functools.partial(_gqa_kernel, ctx_len=ctx_len, tile_off=tile_off),
        grid=(nb, 2, nt),
        in_specs=[pl.BlockSpec((None, tq, 2 * LANES), lambda b, g, t: (b, t + tile_off, g)),
                  pl.BlockSpec((None, l, LANES), lambda b, g, t: (b, 0, 0)),
                  pl.BlockSpec((None, l, LANES), lambda b, g, t: (b, 0, 0))],
        out_specs=pl.BlockSpec((None, tq, 2 * LANES), lambda b, g, t: (b, t + tile_off, g)),
        out_shape=jax.ShapeDtypeStruct((nb, l, 4 * LANES), BF16),
        compiler_params=_cparams(("parallel", "parallel", "parallel")),
        name="gqa_attention",
    )(q, k, v)


def _diff_kernel(q_ref, k_ref, v_ref, lam_ref, g_ref, o_ref, *, ctx_len, tile_off, lam_init):
    qi = pl.program_id(2) + tile_off
    tq = q_ref.shape[0]
    lf = lam_ref[...]
    lam = (jnp.exp(jnp.sum(lf[0:1, :] * lf[1:2, :], axis=-1, keepdims=True))
           - jnp.exp(jnp.sum(lf[2:3, :] * lf[3:4, :], axis=-1, keepdims=True)) + lam_init)
    half = lax.broadcasted_iota(I32, (tq, LANES), 1) // HEAD_DIM

    def body(nchunks):
        q = q_ref[...]
        zero = jnp.zeros_like(q)
        s1 = _scores(jnp.where(half == 0, q, zero), k_ref, nchunks)
        a1, d1, s2 = _softmax_pv(s1, v_ref, jnp.where(half == 1, q, zero), k_ref)
        a2, d2 = _softmax_pv(s2, v_ref)
        o = a1 / d1 - lam * (a2 / d2)
        o_ref[...] = (_rms(o) * g_ref[...] * (1.0 - lam_init)).astype(BF16)

    if tile_off == 0:
        _key_chunk_branches(qi, ctx_len, k_ref.shape[0], body)
    else:
        body(k_ref.shape[0] // KV_CHUNK)


def _diff_attention(q, k, v, lam_p, subln_g, ctx_len, tile_off, lam_init):
    nb, l, d = q.shape
    nh = d // LANES
    tq = ROW_TILE
    nt = l // tq - tile_off
    return pl.pallas_call(
        functools.partial(_diff_kernel, ctx_len=ctx_len, tile_off=tile_off, lam_init=lam_init),
        grid=(nb, nh, nt),
        in_specs=[pl.BlockSpec((None, tq, LANES), lambda b, h, t: (b, t + tile_off, h)),
                  pl.BlockSpec((None, l, LANES), lambda b, h, t: (b, 0, h)),
                  pl.BlockSpec((None, l, LANES), lambda b, h, t: (b, 0, h)),
                  pl.BlockSpec((4, HEAD_DIM), lambda b, h, t: (0, 0)),
                  pl.BlockSpec((1, LANES), lambda b, h, t: (0, 0))],
        out_specs=pl.BlockSpec((None, tq, LANES), lambda b, h, t: (b, t + tile_off, h)),
        out_shape=jax.ShapeDtypeStruct((nb, l, d), BF16),
        compiler_params=_cparams(("parallel", "parallel", "parallel")),
        name="diff_attention",
    )(q, k, v, lam_p, subln_g)


def _retention_kernel(rate_ref, qf_ref, kf_ref, vf_ref, qr_ref, kr_ref, vr_ref,
                      of_ref, or_ref, sf_ref, sr_ref):
    n = pl.program_id(1)
    c = RET_CHUNK
    nh = vf_ref.shape[1] // LANES

    @pl.when(n == 0)
    def _():
        sf_ref[...] = jnp.zeros_like(sf_ref)
        sr_ref[...] = jnp.zeros_like(sr_ref)

    half = lax.broadcasted_iota(I32, (c, LANES), 1) // HEAD_DIM
    ri = lax.broadcasted_iota(I32, (c, c), 0)
    ci = lax.broadcasted_iota(I32, (c, c), 1)
    pos = lax.broadcasted_iota(I32, (c, 1), 0).astype(F32)

    def one(direction, h, q_ref, k_ref, v_ref, o_ref, s_ref):
        qk = slice((h // 2) * LANES, (h // 2 + 1) * LANES)
        vo = slice(h * LANES, (h + 1) * LANES)
        head_ok = half == (h % 2)
        log_g = -jnp.exp(jnp.full((1, 1), rate_ref[direction, h], F32))
        rel = ((ri - ci) if direction == 0 else (ci - ri)).astype(F32)
        dmask = jnp.where(rel >= 0, jnp.exp(log_g * jnp.maximum(rel, 0.0)), 0.0)
        q = jnp.where(head_ok, q_ref[:, qk], jnp.zeros((c, LANES), BF16))
        k = jnp.where(head_ok, k_ref[:, qk], jnp.zeros((c, LANES), BF16))
        v = v_ref[:, vo]
        sc = _dot_nt(q, k) * dmask
        intra = _dot(sc.astype(BF16), v)
        q_pow = (pos + 1.0) if direction == 0 else (c - pos)
        k_pow = (c - 1.0 - pos) if direction == 0 else pos
        state = s_ref[h]
        cross = _dot((q.astype(F32) * jnp.exp(log_g * q_pow)).astype(BF16), state.astype(BF16))
        o_ref[:, vo] = intra + cross
        kd = (k.astype(F32) * jnp.exp(log_g * k_pow)).T.astype(BF16)
        s_ref[h] = jnp.exp(log_g * c) * state + _dot(kd, v)

    for h in range(nh):
        one(0, h, qf_ref, kf_ref, vf_ref, of_ref, sf_ref)
        one(1, h, qr_ref, kr_ref, vr_ref, or_ref, sr_ref)


def _retention(rates, q, k, v, ctx_len):
    nb, l, dv_all = v.shape
    c = RET_CHUNK
    nh = dv_all // LANES
    nctx = ctx_len // c
    ntot = l // c

    def rev(n):
        return jnp.where(n < nctx, nctx - 1 - n, ntot - 1 - (n - nctx))

    dqk = q.shape[2]
    qk_f = pl.BlockSpec((None, c, dqk), lambda b, n: (b, n, 0))
    qk_r = pl.BlockSpec((None, c, dqk), lambda b, n: (b, rev(n), 0))
    v_f = pl.BlockSpec((None, c, dv_all), lambda b, n: (b, n, 0))
    v_r = pl.BlockSpec((None, c, dv_all), lambda b, n: (b, rev(n), 0))
    return pl.pallas_call(
        _retention_kernel,
        grid=(nb, ntot),
        in_specs=[pl.BlockSpec(memory_space=pltpu.SMEM), qk_f, qk_f, v_f, qk_r, qk_r, v_r],
        out_specs=[v_f, v_r],
        out_shape=[jax.ShapeDtypeStruct((nb, l, dv_all), F32)] * 2,
        scratch_shapes=[pltpu.VMEM((nh, LANES, LANES), F32), pltpu.VMEM((nh, LANES, LANES), F32)],
        compiler_params=_cparams(("parallel", "arbitrary")),
        name="retention",
    )(rates, q, k, v, q, k, v)


def _finish_outproj(y, x_ref, mod_ref, g2_ref, x1_ref, h2_ref, h2t_ref):
    x1 = x_ref[...] + mod_ref[2:3, :] * y
    x1_ref[...] = x1
    h2 = (_rms(x1) * g2_ref[...]) * (1.0 + mod_ref[4:5, :]) + mod_ref[3:4, :]
    h2_ref[...] = h2
    h2t_ref[...] = h2.reshape(h2.shape[0], SUBLANES, LANES)


def _even_outproj_kernel(x_ref, ya_ref, of_ref, or_ref, gb_ref, w_ref, mod_ref, g2_ref, x1_ref, h2_ref, h2t_ref):
    half = ya_ref.shape[1]
    y = _dot(ya_ref[...], w_ref[0:half, :])
    for c in range(half // LANES):
        sl = slice(c * LANES, (c + 1) * LANES)
        gate = gb_ref[:, sl]
        r = _rms(of_ref[:, sl] + or_ref[:, sl]) * (gate * _sigmoid(gate))
        y = y + _dot(r.astype(BF16), w_ref[half + c * LANES:half + (c + 1) * LANES, :])
    _finish_outproj(y, x_ref, mod_ref, g2_ref, x1_ref, h2_ref, h2t_ref)


def _outproj_outputs(nb, l, d, tm, tile_off):
    row = pl.BlockSpec((None, tm, d), lambda b, t: (b, t + tile_off, 0))
    tiles = pl.BlockSpec((None, tm, SUBLANES, LANES), lambda b, t: (b, t + tile_off, 0, 0))
    shapes = [jax.ShapeDtypeStruct((nb, l, d), F32)] * 2 + [jax.ShapeDtypeStruct((nb, l, SUBLANES, LANES), F32)]
    return [row, row, tiles], shapes


def _even_outproj(x, ya, o_f, o_r, gb, w, mods, g2):
    nb, l, d = x.shape
    tm = ROW_TILE
    out_specs, out_shape = _outproj_outputs(nb, l, d, tm, 0)
    row = lambda width: pl.BlockSpec((None, tm, width), lambda b, t: (b, t, 0))
    return pl.pallas_call(
        _even_outproj_kernel,
        grid=(nb, l // tm),
        in_specs=[row(d), row(512), row(512), row(512), row(512),
                  pl.BlockSpec((d, d), lambda b, t: (0, 0)),
                  pl.BlockSpec((None, 6, d), _mod_index(nb)),
                  pl.BlockSpec((1, d), lambda b, t: (0, 0))],
        out_specs=out_specs,
        out_shape=out_shape,
        compiler_params=_cparams(("parallel", "parallel")),
        name="even_outproj",
    )(x, ya, o_f, o_r, gb, w, mods, g2)


def _odd_outproj_kernel(x_ref, mix_ref, w_ref, mod_ref, g2_ref, x1_ref, h2_ref, h2t_ref):
    _finish_outproj(_dot(mix_ref[...], w_ref[...]), x_ref, mod_ref, g2_ref, x1_ref, h2_ref, h2t_ref)


def _odd_outproj(x, mix, w, mods, g2, tile_off):
    nb, l, d = x.shape
    tm = ROW_TILE
    out_specs, out_shape = _outproj_outputs(nb, l, d, tm, tile_off)
    row = pl.BlockSpec((None, tm, d), lambda b, t: (b, t + tile_off, 0))
    return pl.pallas_call(
        _odd_outproj_kernel,
        grid=(nb, l // tm - tile_off),
        in_specs=[row, row,
                  pl.BlockSpec((d, d), lambda b, t: (0, 0)),
                  pl.BlockSpec((None, 6, d), lambda b, t: (b, 0, 0)),
                  pl.BlockSpec((1, d), lambda b, t: (0, 0))],
        out_specs=out_specs,
        out_shape=out_shape,
        compiler_params=_cparams(("parallel", "parallel")),
        name="odd_outproj",
    )(x, mix, w, mods, g2)


def _top_rows(s, k, payload=None):
    rows = lax.broadcasted_iota(I32, s.shape, 0).astype(F32)
    n = float(s.shape[0])
    vals, pays = [], []
    for _ in range(k):
        m = jnp.max(s, axis=0, keepdims=True)
        am = jnp.min(jnp.where(s == m, rows, n), axis=0, keepdims=True)
        hit = rows == am
        vals.append(m)
        pays.append(am if payload is None else jnp.max(jnp.where(hit, payload, -1.0), axis=0, keepdims=True))
        s = jnp.where(hit, -jnp.inf, s)
    return jnp.concatenate(vals, axis=0), jnp.concatenate(pays, axis=0)


def _pair_candidates(s1, i1, s2, i2):
    kk = PEER_TOPK
    sub = lax.broadcasted_iota(I32, (SUBLANES, s1.shape[1]), 0)
    cand = [s1[0:1, :] + s2]
    cid = [i1[0:1, :] * PEER_NKEYS + i2]
    for a in range(1, SUBLANES):
        ok = sub < kk // (a + 1)
        cand.append(jnp.where(ok, s1[a:a + 1, :] + s2[0:SUBLANES, :], -jnp.inf))
        cid.append(i1[a:a + 1, :] * PEER_NKEYS + i2[0:SUBLANES, :])
    cand.append(s1[SUBLANES:kk, :] + s2[0:1, :])
    cid.append(i1[SUBLANES:kk, :] * PEER_NKEYS + i2[0:1, :])
    return jnp.concatenate(cand, axis=0), jnp.concatenate(cid, axis=0)


def _compress_rows(sel, skip, arrays):
    n = sel.shape[0]
    cc = jnp.where(sel, skip, 0)
    vals = [jnp.where(sel, a, jnp.zeros_like(a)) for a in arrays]
    for k in range(n.bit_length() - 1):
        mv = ((cc >> k) & 1) == 1

        def step(a):
            zero = jnp.zeros_like(a)
            return jnp.where(mv, zero, a) + pltpu.roll(jnp.where(mv, a, zero), n - (1 << k), 0)

        vals = [step(a) for a in vals]
        cc = step(cc)
    return vals


def _peer_topk_kernel(h_ref, wq_ref, keys_ref, off_ref, gate_ref, ngrp_ref, *, half_rows):
    kk = PEER_TOPK
    q = _dot(h_ref[...].astype(BF16), wq_ref[...])
    eids, gates = [], []
    for hd in range(PEER_HEADS):
        tops = []
        for part in range(2):
            c = 2 * hd + part
            qs = q[:, c * LANES:(c + 1) * LANES].astype(BF16)
            tops.append(_top_rows(_dot_nt(keys_ref[c], qs), kk))
        (s1, i1), (s2, i2) = tops
        cand, cid = _pair_candidates(s1, i1, s2, i2)
        best, eid = _top_rows(cand, kk, cid)
        e = jnp.exp(best - best[0:1, :])
        eids.append(eid)
        gates.append(e / jnp.sum(e, axis=0, keepdims=True))
    eid = jnp.concatenate(eids, axis=0).astype(I32)
    gate = jnp.concatenate(gates, axis=0)
    npick, tm = eid.shape
    top = eid // half_rows
    upper = ((top ^ eid) & 1) == 1
    off = (top * (half_rows // 2) + (eid & (half_rows - 1)) // 2) * SUBLANES
    row = lax.broadcasted_iota(I32, (npick, tm), 0)
    before = (lax.broadcasted_iota(I32, (npick, npick), 1) < lax.broadcasted_iota(I32, (npick, npick), 0))
    upper_before = _dot(before.astype(BF16), upper.astype(BF16)).astype(I32)
    for hf, (sel, skip) in enumerate(((~upper, upper_before), (upper, row - upper_before))):
        off_c, gate_c = _compress_rows(sel, skip, [off, gate])
        off_ref[hf] = off_c.T
        gate_ref[hf] = gate_c.T
        count = jnp.sum(sel.astype(F32), axis=0, keepdims=True)
        ngrp = ((count + (SUBLANES - 1)) * (1.0 / SUBLANES)).astype(I32)
        ngrp_ref[hf] = jnp.broadcast_to(ngrp, (npick, tm)).T


def _flat_tile(nt_all, nt, tile_off):
    return lambda i: (i // nt) * nt_all + (i % nt) + tile_off


def _peer_topk(h2, wq, keys, half_rows, nb, tile_off):
    n, d = h2.shape
    tm = ROW_TILE
    nt_all = n // nb // tm
    nt = nt_all - tile_off
    ft = _flat_tile(nt_all, nt, tile_off)
    npick = PEER_HEADS * PEER_TOPK
    out = pl.BlockSpec((2, tm, npick), lambda i: (0, ft(i), 0))
    return pl.pallas_call(
        functools.partial(_peer_topk_kernel, half_rows=half_rows),
        grid=(nb * nt,),
        in_specs=[pl.BlockSpec((tm, d), lambda i: (ft(i), 0)),
                  pl.BlockSpec(wq.shape, lambda i: (0, 0)),
                  pl.BlockSpec(keys.shape, lambda i: (0, 0, 0))],
        out_specs=[out, out, out],
        out_shape=[jax.ShapeDtypeStruct((2, n, npick), I32), jax.ShapeDtypeStruct((2, n, npick), F32),
                   jax.ShapeDtypeStruct((2, n, npick), I32)],
        compiler_params=_cparams(("parallel",)),
        name="peer_topk",
    )(h2, wq, keys)


def _split_table(tab):
    nexp, d = tab.shape
    t = tab.reshape(2, nexp // 4, 2, d)
    halves = [jnp.stack([t[0, :, h], t[1, :, 1 - h]]) for h in range(2)]
    return jnp.stack(halves).reshape(nexp * d // LANES, LANES)


def _load_table_half(tab_hbm, tab_ref, sem):
    hf = pl.program_id(0)
    half_rows = tab_ref.shape[0]

    @pl.when(pl.program_id(1) == 0)
    def _():
        cp = pltpu.make_async_copy(tab_hbm.at[pl.ds(hf * half_rows, half_rows)], tab_ref, sem)
        cp.start()
        cp.wait()


def _table_row(tab_ref, off):
    return tab_ref[pl.ds(pl.multiple_of(off, SUBLANES), SUBLANES), :]


def _sublane_sum_masks():
    sub = lax.broadcasted_iota(I32, (SUBLANES, LANES), 0)
    return ((sub + 2) % 8 < 4,
            (sub + 1) % 8 < 4,
            sub < 4,
            (sub + 7) % 8 < 4,
            sub % 4 < 2,
            (sub + 3) % 4 < 2,
            sub % 2 == 1)


def _sublane_sums(prods, masks):
    def merge(a, b, m, shift):
        return jnp.where(m, a, b) + pltpu.roll(jnp.where(m, b, a), shift, 0)

    s0 = merge(prods[0], prods[4], masks[0], 4)
    s1 = merge(prods[1], prods[5], masks[1], 4)
    s2 = merge(prods[2], prods[6], masks[2], 4)
    s3 = merge(prods[3], prods[7], masks[3], 4)
    u0 = merge(s0, s2, masks[4], 2)
    u1 = merge(s1, s3, masks[5], 2)
    return merge(u0, u1, masks[6], 1)


def _peer_dot_kernel(iota_ref, off_ref, ngrp_ref, gate_ref, x_ref, tab_hbm, d_ref, tab_ref, q_ref, sem):
    _load_table_half(tab_hbm, tab_ref, sem)
    npick, tb = d_ref.shape
    xr = PEER_XLU_GROUPS * SUBLANES
    lane_lo = lax.broadcasted_iota(I32, (xr, tb), 1)
    lane_hi = lax.broadcasted_iota(I32, (npick - xr, tb), 1)
    masks = _sublane_sum_masks()
    ks = [iota_ref[k] for k in range(SUBLANES)]

    @pl.when(pl.program_id(1) == 0)
    def _():
        def clear(t, carry):
            q_ref[t] = jnp.zeros((npick, LANES), F32)
            return carry
        lax.fori_loop(0, tb, clear, 0)

    def gather(t, carry):
        x = x_ref[t]
        base = pl.multiple_of(t * npick, npick)

        def group(start):
            offs = off_ref.at[pl.ds(base + start, SUBLANES)]
            prods = [_table_row(tab_ref, offs[ks[(k + 1) % SUBLANES]]) * x for k in range(SUBLANES)]
            q_ref[t, pl.ds(start, SUBLANES), :] = _sublane_sums(prods, masks)

        for g in range(PEER_STATIC_GROUPS):
            group(g * SUBLANES)

        def extra(g, c):
            group(pl.multiple_of(g * SUBLANES, SUBLANES))
            return c

        lax.fori_loop(PEER_STATIC_GROUPS, ngrp_ref[base], extra, 0)
        return carry

    def lane_reduce(t, acc):
        col = jnp.sum(q_ref[t, 0:xr, :], axis=1, keepdims=True)
        return jnp.where(lane_lo == t, col, acc)

    def lane_reduce_rest(t, carry):
        @pl.when(ngrp_ref[t * npick] > PEER_XLU_GROUPS)
        def _():
            col = jnp.sum(q_ref[t, xr:npick, :], axis=1, keepdims=True)
            d_ref[xr:npick, :] = jnp.where(lane_hi == t, col, d_ref[xr:npick, :])
        return carry

    lax.fori_loop(0, tb, gather, 0)
    d_ref[0:xr, :] = lax.fori_loop(0, tb, lane_reduce, jnp.zeros((xr, tb), F32), unroll=PEER_XLU_UNROLL)
    d_ref[xr:npick, :] = jnp.zeros((npick - xr, tb), F32)

    @pl.when(jnp.max(gate_ref[:, xr:npick]) > 0.0)
    def _():
        lax.fori_loop(0, tb, lane_reduce_rest, 0)


def _peer_sum_kernel(iota_ref, off_ref, ngrp_ref, gate_ref, d_ref, tab_hbm, f_ref, tab_ref, wl_ref, sem):
    _load_table_half(tab_hbm, tab_ref, sem)
    tb, npick = gate_ref.shape
    xr = PEER_XLU_GROUPS * SUBLANES
    nacc = 4
    ks = [iota_ref[k] for k in range(SUBLANES)]
    gate = gate_ref[...].T
    d = d_ref[...]
    act = 0.5 * d * (1.0 + lax.erf(d * (2.0 ** -0.5)))
    wt = jnp.where(gate > 0.0, gate * act, 0.0)
    lane_lo = lax.broadcasted_iota(I32, (xr, tb), 1)
    lane_hi = lax.broadcasted_iota(I32, (npick - xr, tb), 1)

    def spread(t, carry):
        col = jnp.sum(jnp.where(lane_lo == t, wt[0:xr], 0.0), axis=1, keepdims=True)
        wl_ref[t, 0:xr, :] = jnp.broadcast_to(col, (xr, LANES))
        return carry

    def spread_rest(t, carry):
        @pl.when(ngrp_ref[t * npick] > PEER_XLU_GROUPS)
        def _():
            col = jnp.sum(jnp.where(lane_hi == t, wt[xr:npick], 0.0), axis=1, keepdims=True)
            wl_ref[t, xr:npick, :] = jnp.broadcast_to(col, (npick - xr, LANES))
        return carry

    def token(t, carry):
        base = pl.multiple_of(t * npick, npick)

        def group(start, accs):
            accs = list(accs)
            offs = off_ref.at[pl.ds(base + start, SUBLANES)]
            for k in range(SUBLANES):
                wv = jnp.broadcast_to(wl_ref[t, pl.ds(start + k, 1), :], (SUBLANES, LANES))
                accs[k % nacc] = accs[k % nacc] + wv * _table_row(tab_ref, offs[ks[k]])
            return tuple(accs)

        accs = tuple(jnp.zeros((SUBLANES, LANES), F32) for _ in range(nacc))
        for g in range(PEER_STATIC_GROUPS):
            accs = group(g * SUBLANES, accs)
        accs = lax.fori_loop(PEER_STATIC_GROUPS, ngrp_ref[base],
                             lambda g, a: group(pl.multiple_of(g * SUBLANES, SUBLANES), a), accs)
        f_ref[t] = (accs[0] + accs[1]) + (accs[2] + accs[3])
        return carry

    lax.fori_loop(0, tb, spread, 0, unroll=PEER_XLU_UNROLL)

    @pl.when(jnp.max(gate_ref[:, xr:npick]) > 0.0)
    def _():
        lax.fori_loop(0, tb, spread_rest, 0)

    lax.fori_loop(0, tb, token, 0)


def _peer_tiles(n, nb, tile_off_rows):
    tb = PEER_TOK
    nt_all = n // nb // tb
    off = tile_off_rows // tb
    nt = nt_all - off
    return tb, nb * nt, _flat_tile(nt_all, nt, off)


def _smem_token_block(tb, npick, blocks_per_half, ft):
    return pl.BlockSpec((tb * npick,), lambda hf, i: (hf * blocks_per_half + ft(i),), memory_space=pltpu.SMEM)


def _peer_dot(off_flat, ngrp_flat, gate2, h2v, tab2, nb, tile_off_rows):
    _, n, npick = gate2.shape
    tb, steps, ft = _peer_tiles(n, nb, tile_off_rows)
    half = tab2.shape[0] // 2
    smem_blk = _smem_token_block(tb, npick, n // tb, ft)
    return pl.pallas_call(
        _peer_dot_kernel,
        grid=(2, steps),
        in_specs=[pl.BlockSpec(memory_space=pltpu.SMEM), smem_blk, smem_blk,
                  pl.BlockSpec((None, tb, npick), lambda hf, i: (hf, ft(i), 0)),
                  pl.BlockSpec((tb, SUBLANES, LANES), lambda hf, i: (ft(i), 0, 0)),
                  pl.BlockSpec(memory_space=pl.ANY)],
        out_specs=pl.BlockSpec((None, npick, tb), lambda hf, i: (hf, 0, ft(i))),
        out_shape=jax.ShapeDtypeStruct((2, npick, n), F32),
        scratch_shapes=[pltpu.VMEM((half, LANES), F32),
                        pltpu.VMEM((tb, npick, LANES), F32),
                        pltpu.SemaphoreType.DMA],
        compiler_params=_cparams(("arbitrary", "arbitrary"), VMEM_LIMIT_TABLE),
        name="peer_dot",
    )(jnp.arange(SUBLANES, dtype=I32), off_flat, ngrp_flat, gate2, h2v, tab2)


def _peer_sum(off_flat, ngrp_flat, gate2, dpart, tab2, nb, tile_off_rows):
    _, n, npick = gate2.shape
    tb, steps, ft = _peer_tiles(n, nb, tile_off_rows)
    half = tab2.shape[0] // 2
    smem_blk = _smem_token_block(tb, npick, n // tb, ft)
    return pl.pallas_call(
        _peer_sum_kernel,
        grid=(2, steps),
        in_specs=[pl.BlockSpec(memory_space=pltpu.SMEM), smem_blk, smem_blk,
                  pl.BlockSpec((None, tb, npick), lambda hf, i: (hf, ft(i), 0)),
                  pl.BlockSpec((None, npick, tb), lambda hf, i: (hf, 0, ft(i))),
                  pl.BlockSpec(memory_space=pl.ANY)],
        out_specs=pl.BlockSpec((None, tb, SUBLANES, LANES), lambda hf, i: (hf, ft(i), 0, 0)),
        out_shape=jax.ShapeDtypeStruct((2, n, SUBLANES, LANES), F32),
        scratch_shapes=[pltpu.VMEM((half, LANES), F32),
                        pltpu.VMEM((tb, npick, LANES), F32),
                        pltpu.SemaphoreType.DMA],
        compiler_params=_cparams(("arbitrary", "arbitrary"), VMEM_LIMIT_TABLE),
        name="peer_sum",
    )(jnp.arange(SUBLANES, dtype=I32), off_flat, ngrp_flat, gate2, dpart, tab2)


def _mixer_out(f_ref, shape):
    return (f_ref[0] + f_ref[1]).reshape(shape)


def _residual_kernel(x_ref, f_ref, mod_ref, o_ref):
    o_ref[...] = x_ref[...] + mod_ref[5:6, :] * _mixer_out(f_ref, o_ref.shape)


def _final_kernel(x_ref, f_ref, mod_ref, g_ref, o_ref):
    o_ref[...] = _rms(x_ref[...] + mod_ref[5:6, :] * _mixer_out(f_ref, o_ref.shape)) * g_ref[...]


def _residual(x1, f2, mods):
    nb, l, d = x1.shape
    tm = ROW_TILE
    return pl.pallas_call(
        _residual_kernel,
        grid=(nb, l // tm),
        in_specs=[pl.BlockSpec((None, tm, d), lambda b, t: (b, t, 0)),
                  pl.BlockSpec((2, None, tm, SUBLANES, LANES), lambda b, t: (0, b, t, 0, 0)),
                  pl.BlockSpec((None, 6, d), _mod_index(nb))],
        out_specs=pl.BlockSpec((None, tm, d), lambda b, t: (b, t, 0)),
        out_shape=jax.ShapeDtypeStruct((nb, l, d), F32),
        compiler_params=_cparams(("parallel", "parallel")),
        name="peer_residual",
    )(x1, f2, mods)


def _final(x1, f2, mods, g, tile_off):
    nb, l, d = x1.shape
    tm = ROW_TILE
    nt = l // tm - tile_off
    return pl.pallas_call(
        _final_kernel,
        grid=(nb, nt),
        in_specs=[pl.BlockSpec((None, tm, d), lambda b, t: (b, t + tile_off, 0)),
                  pl.BlockSpec((2, None, tm, SUBLANES, LANES), lambda b, t: (0, b, t + tile_off, 0, 0)),
                  pl.BlockSpec((None, 6, d), lambda b, t: (b, 0, 0)),
                  pl.BlockSpec((1, d), lambda b, t: (0, 0))],
        out_specs=pl.BlockSpec((None, tm, d), lambda b, t: (b, t, 0)),
        out_shape=jax.ShapeDtypeStruct((nb, nt * tm, d), F32),
        compiler_params=_cparams(("parallel", "parallel")),
        name="final_norm",
    )(x1, f2, mods, g)


def _rope_tables(seq, ctx_len):
    t = jnp.arange(seq)
    row_id = (t // GRID_W).astype(F32)
    col_id = (t % GRID_W).astype(F32)
    axis_dim = HEAD_DIM // 2
    inv = ROPE_THETA ** (-jnp.arange(0, axis_dim, 2, dtype=F32) / axis_dim)
    ang_r = row_id[:, None] * inv[None, :]
    ang_c = col_id[:, None] * inv[None, :]
    cos = jnp.concatenate([jnp.cos(ang_r)] * 2 + [jnp.cos(ang_c)] * 2, axis=-1)
    sin = jnp.concatenate([-jnp.sin(ang_r), jnp.sin(ang_r), -jnp.sin(ang_c), jnp.sin(ang_c)], axis=-1)
    cos = jnp.concatenate([jnp.ones((ctx_len, HEAD_DIM), F32), cos], axis=0)
    sin = jnp.concatenate([jnp.zeros((ctx_len, HEAD_DIM), F32), sin], axis=0)
    return jnp.tile(cos, (1, 2)), jnp.tile(sin, (1, 2))


def _peer(h2, h2t, x1, mods, wq, keys, u_tab, v_tab, nb, tile_off_rows, final_g):
    b, l, d = x1.shape
    n = b * l
    nexp = u_tab.shape[0]
    npick = PEER_HEADS * PEER_TOPK
    off2, gate2, ngrp2 = _peer_topk(h2.reshape(n, d), wq.astype(BF16),
                                    keys.reshape(PEER_HEADS * 2, PEER_NKEYS, -1).astype(BF16),
                                    nexp // 2, nb, tile_off_rows // ROW_TILE)
    off_flat = off2.reshape(2 * n * npick)
    ngrp_flat = ngrp2.reshape(2 * n * npick)
    h2v = h2t.reshape(n, SUBLANES, LANES)
    dpart = _peer_dot(off_flat, ngrp_flat, gate2, h2v, _split_table(u_tab), nb, tile_off_rows)
    f2 = _peer_sum(off_flat, ngrp_flat, gate2, dpart, _split_table(v_tab), nb, tile_off_rows)
    f2 = f2.reshape(2, b, l, SUBLANES, LANES)
    if final_g is None:
        return _residual(x1, f2, mods)
    return _final(x1, f2, mods, final_g.reshape(1, d), tile_off_rows // ROW_TILE)


def kernel(x, c, ctx, c_ctx, ada_w, ada_b, norm1_g, norm2_g, ev_w_in, ev_w_out, gqa_q_norm_g,
           gqa_k_norm_g, ret_log_rate, od_w_in, od_w_out, diff_lambda, diff_subln_g, peer_w_q,
           peer_keys, peer_u, peer_v, final_g):
    nb, seq, d = x.shape
    ctx_len = ctx.shape[1]
    depth = ada_w.shape[0]
    assert ctx_len == ROW_TILE and seq % ROW_TILE == 0 and d == SUBLANES * LANES
    xs = jnp.concatenate([ctx, x], axis=1)
    cos, sin = _rope_tables(seq, ctx_len)
    cc = jnp.concatenate([c, c_ctx[None, :]], axis=0)
    for layer in range(depth):
        last = layer == depth - 1
        mods = _ada(cc, ada_w[layer], ada_b[layer]).reshape(nb + 1, 6, d)
        g1 = norm1_g[layer].reshape(1, d)
        g2 = norm2_g[layer].reshape(1, d)
        lat_off = 1 if last else 0
        if layer % 2 == 0:
            e = layer // 2
            qa, ka, va, qb, kb, vb, gb = _even_inproj(
                xs, g1, mods, ev_w_in[e].astype(BF16), cos, sin,
                jnp.tile(gqa_q_norm_g[e], 2).reshape(1, LANES), jnp.tile(gqa_k_norm_g[e], 2).reshape(1, LANES))
            ya = _gqa(qa, ka, va, ctx_len, 0)
            o_f, o_r = _retention(ret_log_rate[e], qb, kb, vb, ctx_len)
            x1, h2, h2t = _even_outproj(xs, ya, o_f, o_r, gb, ev_w_out[e].astype(BF16), mods, g2)
        else:
            o = layer // 2
            lam_init = 0.8 - 0.6 * math.exp(-0.3 * layer)
            q, k, v = _odd_inproj(xs, g1, mods, od_w_in[o].astype(BF16), cos, sin)
            mix = _diff_attention(q, k, v, diff_lambda[o], diff_subln_g[o].reshape(1, LANES),
                                  ctx_len, lat_off, lam_init)
            x1, h2, h2t = _odd_outproj(xs, mix, od_w_out[o].astype(BF16), mods, g2, lat_off)
        xs = _peer(h2, h2t, x1, mods, peer_w_q[layer], peer_keys[layer], peer_u[layer], peer_v[layer],
                   nb, lat_off * ROW_TILE, final_g if last else None)
    return xs
```
